```python
import math
import jax, jax.numpy as jnp
from jax import lax
import numpy as np

D_MODEL = 2048
BATCH = 2
SEQ = 4096
DEPTH = 4

HEAD_DIM = 64
Q_BLOCK = 128
RMS_EPS = 1e-6
MASK_VALUE = -1e30
ALIBI_MAX_EXP = 8.0
FOX_HEADS = 7
NSA_HEADS = 8
NSA_KV_GROUPS = 2
NSA_CMP_LEN = 32
NSA_CMP_STRIDE = 16
NSA_SEL_BLOCK = 64
NSA_TOP_N = 16
NSA_WINDOW = 512
NSA_FORCE_SCORE = 1e4
DIL_CONFIGS = ((128, 1), (512, 4), (2048, 16))
DIL_HEADS_PER_GROUP = 3
DIL_HEADS = DIL_HEADS_PER_GROUP * len(DIL_CONFIGS)
DIFF_HEADS = 4
DIFF_QK_DIM = 64
DIFF_V_DIM = 2 * DIFF_QK_DIM
MIX_WIDTH = (FOX_HEADS + NSA_HEADS + DIL_HEADS) * HEAD_DIM + DIFF_HEADS * DIFF_V_DIM
D_FF = ((8 * D_MODEL + 3 * 256 - 1) // (3 * 256)) * 256
NSA_KV = NSA_KV_GROUPS * HEAD_DIM
IN_SPLITS = (
    ('fox_q', FOX_HEADS * HEAD_DIM), ('fox_k', FOX_HEADS * HEAD_DIM), ('fox_v', FOX_HEADS * HEAD_DIM), ('fox_f', FOX_HEADS),
    ('nsa_q', NSA_HEADS * HEAD_DIM),
    ('nsa_cmp_k', NSA_KV), ('nsa_cmp_v', NSA_KV), ('nsa_slc_k', NSA_KV), ('nsa_slc_v', NSA_KV), ('nsa_win_k', NSA_KV), ('nsa_win_v', NSA_KV),
    ('nsa_gate', NSA_HEADS * 3),
    ('dil_q', DIL_HEADS * HEAD_DIM), ('dil_k', DIL_HEADS * HEAD_DIM), ('dil_v', DIL_HEADS * HEAD_DIM),
    ('diff_q', DIFF_HEADS * 2 * DIFF_QK_DIM), ('diff_k', DIFF_HEADS * 2 * DIFF_QK_DIM), ('diff_v', DIFF_HEADS * DIFF_V_DIM),
)
N_IN = sum(w for _, w in IN_SPLITS)

kernel_name = 'hybrid_parallel_head_decoder'


def rms_norm(x, gain):
    xf = x.astype(jnp.float32)
    y = xf * lax.rsqrt(jnp.mean(xf * xf, axis=-1, keepdims=True) + RMS_EPS)
    return (y * gain.astype(jnp.float32)).astype(x.dtype)


def masked_softmax(s, mask):
    p = jax.nn.softmax(jnp.where(mask, s, MASK_VALUE), axis=-1)
    return jnp.where(mask, p, 0.0)


def alibi_slopes(n_heads):
    return jnp.asarray(2.0 ** (-ALIBI_MAX_EXP * np.arange(1, n_heads + 1) / n_heads), jnp.float32)


def blocks_to_seq(o):
    o = jnp.moveaxis(o, 0, 1)
    return o.reshape(o.shape[0], o.shape[1] * o.shape[2], *o.shape[3:])


def fox_attention(q, k, v, f_logit, f_bias):
    B, S, H, dh = q.shape
    scale = dh ** -0.5
    log_f = jax.nn.log_sigmoid(f_logit.astype(jnp.float32) + f_bias.astype(jnp.float32))
    cum = jnp.cumsum(log_f, axis=1).transpose(0, 2, 1)
    pos = jnp.arange(S)

    def block(i):
        qs = i * Q_BLOCK
        tq = qs + jnp.arange(Q_BLOCK)
        qb = lax.dynamic_slice_in_dim(q, qs, Q_BLOCK, 1)
        cq = lax.dynamic_slice_in_dim(cum, qs, Q_BLOCK, 2)
        s = jnp.einsum('bqhd,bkhd->bhqk', qb, k).astype(jnp.float32) * scale
        s = s + cq[..., None] - cum[:, :, None, :]
        p = jax.nn.softmax(jnp.where(pos[None, :] <= tq[:, None], s, MASK_VALUE), axis=-1)
        return jnp.einsum('bhqk,bkhd->bqhd', p.astype(v.dtype), v)

    o = blocks_to_seq(lax.map(block, jnp.arange(S // Q_BLOCK)))
    return o.reshape(B, S, H * dh)


def cmp_to_sel_matrix(n_cmp, n_sel):
    a = NSA_SEL_BLOCK // NSA_CMP_STRIDE
    b = NSA_CMP_LEN // NSA_CMP_STRIDE
    j = np.arange(n_sel)[:, None, None]
    idx = a * j + np.arange(a)[None, :, None] - np.arange(b)[None, None, :]
    jj = np.broadcast_to(j, idx.shape)
    ok = (idx >= 0) & (idx < n_cmp)
    m = np.zeros((n_cmp, n_sel), np.float32)
    np.add.at(m, (idx[ok], jj[ok]), 1.0)
    return m


def nsa_attention(q, k_cmp, v_cmp, k_slc, v_slc, k_win, v_win, gate_logit, cmp_w1, cmp_w2, cmp_pos):
    B, S, H, dh = q.shape
    G = k_cmp.shape[2]
    hpg = H // G
    scale = dh ** -0.5
    slopes = alibi_slopes(H).reshape(G, hpg)[None, :, :, None, None]
    ratio = NSA_CMP_LEN // NSA_CMP_STRIDE
    n_chunk = S // NSA_CMP_STRIDE
    n_cmp = n_chunk - ratio + 1
    n_sel = S // NSA_SEL_BLOCK
    n_top = min(NSA_TOP_N, n_sel)

    def compress(t, w1, w2, pos_emb):
        ch = t.reshape(B, n_chunk, NSA_CMP_STRIDE, G, dh)
        blk = jnp.concatenate([ch[:, r:r + n_cmp] for r in range(ratio)], axis=2)
        blk = (blk + pos_emb[:, None, :]).transpose(0, 1, 3, 2, 4).reshape(B, n_cmp, G, NSA_CMP_LEN * dh)
        return jax.nn.gelu(blk @ w1) @ w2

    kc = compress(k_cmp, cmp_w1[0], cmp_w2[0], cmp_pos[0])
    vc = compress(v_cmp, cmp_w1[1], cmp_w2[1], cmp_pos[1])
    cmp_end = jnp.arange(n_cmp) * NSA_CMP_STRIDE + (NSA_CMP_LEN - 1)
    cmp_to_sel = jnp.asarray(cmp_to_sel_matrix(n_cmp, n_sel), jnp.float32)
    k_blocks = k_slc.transpose(0, 2, 1, 3).reshape(B, G, n_sel, NSA_SEL_BLOCK, dh)
    v_blocks = v_slc.transpose(0, 2, 1, 3).reshape(B, G, n_sel, NSA_SEL_BLOCK, dh)
    pad = ((0, 0), (NSA_WINDOW, 0), (0, 0), (0, 0))
    k_win_p = jnp.pad(k_win, pad)
    v_win_p = jnp.pad(v_win, pad)
    gates = jax.nn.sigmoid(gate_logit).reshape(B, S, G, hpg, 3)
    sel_ids = jnp.arange(n_sel)
    win_offsets = jnp.arange(Q_BLOCK + NSA_WINDOW) - NSA_WINDOW
    sel_offsets = jnp.arange(NSA_SEL_BLOCK)

    def block(i):
        qs = i * Q_BLOCK
        tq = qs + jnp.arange(Q_BLOCK)
        qb = lax.dynamic_slice_in_dim(q, qs, Q_BLOCK, 1).reshape(B, Q_BLOCK, G, hpg, dh)
        dist_c = (tq[:, None] - cmp_end[None, :]).astype(jnp.float32)
        s_c = jnp.einsum('bqgjd,bcgd->bgjqc', qb, kc).astype(jnp.float32) * scale - slopes * dist_c
        p_c = masked_softmax(s_c, dist_c >= 0)
        o_c = jnp.einsum('bgjqc,bcgd->bqgjd', p_c.astype(vc.dtype), vc)
        imp = jnp.einsum('bgjqc,cn->bgqn', p_c, cmp_to_sel)
        cur = (tq // NSA_SEL_BLOCK)[:, None]
        forced = (sel_ids == 0) | (sel_ids == cur) | (sel_ids == cur - 1)
        causal = sel_ids * NSA_SEL_BLOCK <= tq[:, None]
        score = jnp.where(causal, jnp.where(forced, NSA_FORCE_SCORE, imp), MASK_VALUE)
        _, top_idx = lax.top_k(score, n_top)
        flat = top_idx.reshape(B, G, Q_BLOCK * n_top)[..., None, None]
        ks = jnp.take_along_axis(k_blocks, flat, axis=2).reshape(B, G, Q_BLOCK, n_top * NSA_SEL_BLOCK, dh)
        vs = jnp.take_along_axis(v_blocks, flat, axis=2).reshape(B, G, Q_BLOCK, n_top * NSA_SEL_BLOCK, dh)
        kpos = (top_idx[..., None] * NSA_SEL_BLOCK + sel_offsets).reshape(B, G, Q_BLOCK, n_top * NSA_SEL_BLOCK)
        dist_s = (tq[:, None] - kpos).astype(jnp.float32)[:, :, None]
        s_s = jnp.einsum('bqgjd,bgqkd->bgjqk', qb, ks).astype(jnp.float32) * scale - slopes * dist_s
        p_s = masked_softmax(s_s, dist_s >= 0)
        o_s = jnp.einsum('bgjqk,bgqkd->bqgjd', p_s.astype(vs.dtype), vs)
        kw = lax.dynamic_slice_in_dim(k_win_p, qs, Q_BLOCK + NSA_WINDOW, 1)
        vw = lax.dynamic_slice_in_dim(v_win_p, qs, Q_BLOCK + NSA_WINDOW, 1)
        kpos_w = qs + win_offsets
        dist_w = tq[:, None] - kpos_w[None, :]
        mask_w = (dist_w >= 0) & (dist_w < NSA_WINDOW) & (kpos_w[None, :] >= 0)
        s_w = jnp.einsum('bqgjd,bkgd->bgjqk', qb, kw).astype(jnp.float32) * scale - slopes * dist_w.astype(jnp.float32)
        p_w = masked_softmax(s_w, mask_w)
        o_w = jnp.einsum('bgjqk,bkgd->bqgjd', p_w.astype(vw.dtype), vw)
        g = lax.dynamic_slice_in_dim(gates, qs, Q_BLOCK, 1)
        return g[..., 0:1] * o_c + g[..., 1:2] * o_s + g[..., 2:3] * o_w

    o = blocks_to_seq(lax.map(block, jnp.arange(S // Q_BLOCK)))
    return o.reshape(B, S, H * dh)


def dilated_attention(q, k, v):
    B, S, H, dh = q.shape
    scale = dh ** -0.5
    slopes = alibi_slopes(H)
    hpg = DIL_HEADS_PER_GROUP
    k_groups = [k[:, :, g * hpg:(g + 1) * hpg] for g in range(len(DIL_CONFIGS))]
    v_groups = [v[:, :, g * hpg:(g + 1) * hpg] for g in range(len(DIL_CONFIGS))]

    def block(i):
        qs = i * Q_BLOCK
        tq = qs + jnp.arange(Q_BLOCK)
        qb = lax.dynamic_slice_in_dim(q, qs, Q_BLOCK, 1)
        outs, lses = [], []
        for g, (window, dilation) in enumerate(DIL_CONFIGS):
            steps = jnp.arange(window // dilation + 1) * dilation
            kidx = tq[:, None] - steps[None, :]
            valid = kidx >= 0
            kidx = jnp.maximum(kidx, 0)
            kg = k_groups[g][:, kidx]
            vg = v_groups[g][:, kidx]
            s = jnp.einsum('bqhd,bqkhd->bhqk', qb[:, :, g * hpg:(g + 1) * hpg], kg).astype(jnp.float32) * scale
            s = s - slopes[g * hpg:(g + 1) * hpg][:, None, None] * steps.astype(jnp.float32)
            s = jnp.where(valid, s, MASK_VALUE)
            lse = jax.nn.logsumexp(s, axis=-1)
            p = jnp.exp(s - lse[..., None])
            outs.append(jnp.einsum('bhqk,bqkhd->bqhd', p.astype(vg.dtype), vg))
            lses.append(lse)
        w = jax.nn.softmax(jnp.stack(lses, axis=0), axis=0).transpose(0, 1, 3, 2)[..., None]
        return jnp.concatenate([o * w[g].astype(o.dtype) for g, o in enumerate(outs)], axis=2)

    o = blocks_to_seq(lax.map(block, jnp.arange(S // Q_BLOCK)))
    return o.reshape(B, S, H * dh)


def diff_attention(q, k, v, lam_vecs, subln_g, lam_init):
    B, S, H, _, dq = q.shape
    dv = v.shape[-1]
    scale = dq ** -0.5
    lv = lam_vecs.astype(jnp.float32)
    lam = jnp.exp(jnp.sum(lv[0] * lv[1])) - jnp.exp(jnp.sum(lv[2] * lv[3])) + lam_init
    slopes = alibi_slopes(H)[None, :, None, None, None]
    pos = jnp.arange(S)

    def block(i):
        qs = i * Q_BLOCK
        tq = qs + jnp.arange(Q_BLOCK)
        qb = lax.dynamic_slice_in_dim(q, qs, Q_BLOCK, 1)
        dist = tq[:, None] - pos[None, :]
        s = jnp.einsum('bqhmd,bkhmd->bhmqk', qb, k).astype(jnp.float32) * scale - slopes * dist.astype(jnp.float32)
        p = jax.nn.softmax(jnp.where(dist >= 0, s, MASK_VALUE), axis=-1)
        a = p[:, :, 0] - lam * p[:, :, 1]
        return jnp.einsum('bhqk,bkhd->bqhd', a.astype(v.dtype), v)

    o = blocks_to_seq(lax.map(block, jnp.arange(S // Q_BLOCK)))
    o = rms_norm(o, subln_g) * (1.0 - lam_init)
    return o.reshape(B, S, H * dv)


def hybrid_mixer(u, w_in, fox_f_bias, cmp_w1, cmp_w2, cmp_pos, diff_lambda, diff_subln_g, diff_lam_init, w_out):
    B, S, _ = u.shape
    proj = u @ w_in
    offsets = [int(o) for o in np.cumsum([w for _, w in IN_SPLITS])[:-1]]
    p = dict(zip([n for n, _ in IN_SPLITS], jnp.split(proj, offsets, axis=-1)))

    def hd(name, *shape):
        return p[name].reshape(B, S, *shape)

    o_fox = fox_attention(hd('fox_q', FOX_HEADS, HEAD_DIM), hd('fox_k', FOX_HEADS, HEAD_DIM),
                          hd('fox_v', FOX_HEADS, HEAD_DIM), p['fox_f'], fox_f_bias)
    o_nsa = nsa_attention(hd('nsa_q', NSA_HEADS, HEAD_DIM),
                          hd('nsa_cmp_k', NSA_KV_GROUPS, HEAD_DIM), hd('nsa_cmp_v', NSA_KV_GROUPS, HEAD_DIM),
                          hd('nsa_slc_k', NSA_KV_GROUPS, HEAD_DIM), hd('nsa_slc_v', NSA_KV_GROUPS, HEAD_DIM),
                          hd('nsa_win_k', NSA_KV_GROUPS, HEAD_DIM), hd('nsa_win_v', NSA_KV_GROUPS, HEAD_DIM),
                          hd('nsa_gate', NSA_HEADS, 3), cmp_w1, cmp_w2, cmp_pos)
    o_dil = dilated_attention(hd('dil_q', DIL_HEADS, HEAD_DIM), hd('dil_k', DIL_HEADS, HEAD_DIM),
                              hd('dil_v', DIL_HEADS, HEAD_DIM))
    o_diff = diff_attention(hd('diff_q', DIFF_HEADS, 2, DIFF_QK_DIM), hd('diff_k', DIFF_HEADS, 2, DIFF_QK_DIM),
                            hd('diff_v', DIFF_HEADS, DIFF_V_DIM), diff_lambda, diff_subln_g, diff_lam_init)
    o = jnp.concatenate([o_fox, o_nsa, o_dil, o_diff], axis=-1)
    return o @ w_out


def swiglu(u, w_gate, w_up, w_down):
    return (jax.nn.silu(u @ w_gate) * (u @ w_up)) @ w_down


def setup_inputs(seed: int = 0) -> dict:
    key = jax.random.key(seed)
    ks = jax.random.split(key, 18)
    D = D_MODEL

    def nrm(k, shape, scale):
        return jax.random.normal(k, shape, jnp.float32) * scale

    return {
        'x': nrm(ks[0], (BATCH, SEQ, D), 1.0),
        'c': nrm(ks[1], (BATCH, D), 1.0),
        'ada_w': nrm(ks[2], (DEPTH, D, 6 * D), 0.5 * D ** -0.5),
        'ada_b': nrm(ks[3], (DEPTH, 6 * D), 0.02),
        'norm_mix_g': 1.0 + nrm(ks[4], (DEPTH, D), 0.02),
        'norm_ffn_g': 1.0 + nrm(ks[5], (DEPTH, D), 0.02),
        'w_in': nrm(ks[6], (DEPTH, D, N_IN), D ** -0.5),
        'fox_f_bias': 3.0 + nrm(ks[7], (DEPTH, FOX_HEADS), 0.5),
        'nsa_cmp_w1': nrm(ks[8], (DEPTH, 2, NSA_CMP_LEN * HEAD_DIM, HEAD_DIM), (NSA_CMP_LEN * HEAD_DIM) ** -0.5),
        'nsa_cmp_w2': nrm(ks[9], (DEPTH, 2, HEAD_DIM, HEAD_DIM), 1.5 * HEAD_DIM ** -0.5),
        'nsa_cmp_pos': nrm(ks[10], (DEPTH, 2, NSA_CMP_LEN, HEAD_DIM), 0.1),
        'diff_lambda': nrm(ks[11], (DEPTH, 4, DIFF_QK_DIM), 0.1),
        'diff_subln_g': 1.0 + nrm(ks[12], (DEPTH, DIFF_V_DIM), 0.02),
        'w_out': nrm(ks[13], (DEPTH, MIX_WIDTH, D), MIX_WIDTH ** -0.5),
        'ffn_w_gate': nrm(ks[14], (DEPTH, D, D_FF), D ** -0.5),
        'ffn_w_up': nrm(ks[15], (DEPTH, D, D_FF), D ** -0.5),
        'ffn_w_down': nrm(ks[16], (DEPTH, D_FF, D), D_FF ** -0.5),
        'final_norm_g': 1.0 + nrm(ks[17], (D,), 0.02),
    }


def reference(x, c, ada_w, ada_b, norm_mix_g, norm_ffn_g, w_in, fox_f_bias, nsa_cmp_w1, nsa_cmp_w2,
              nsa_cmp_pos, diff_lambda, diff_subln_g, w_out, ffn_w_gate, ffn_w_up, ffn_w_down, final_norm_g):
    c_act = jax.nn.silu(c)
    h = x
    for layer in range(DEPTH):
        mod = c_act @ ada_w[layer] + ada_b[layer]
        sh1, sc1, g1, sh2, sc2, g2 = [m[:, None, :] for m in jnp.split(mod, 6, axis=-1)]
        lam_init = 0.8 - 0.6 * math.exp(-0.3 * layer)
        u = rms_norm(h, norm_mix_g[layer]) * (1.0 + sc1) + sh1
        h = h + g1 * hybrid_mixer(u, w_in[layer], fox_f_bias[layer], nsa_cmp_w1[layer], nsa_cmp_w2[layer],
                                  nsa_cmp_pos[layer], diff_lambda[layer], diff_subln_g[layer], lam_init, w_out[layer])
        u = rms_norm(h, norm_ffn_g[layer]) * (1.0 + sc2) + sh2
        h = h + g2 * swiglu(u, ffn_w_gate[layer], ffn_w_up[layer], ffn_w_down[layer])
    return rms_norm(h, final_norm_g)
```

```python
import functools
import math

import numpy as np
import jax
import jax.numpy as jnp
from jax import lax
from jax.experimental import pallas as pl
from jax.experimental.pallas import tpu as pltpu

F32 = jnp.float32
BF16 = jnp.bfloat16

LANES = 128
HEAD_DIM = 64
HALF = HEAD_DIM
RMS_EPS = 1e-6
NEG = -1e30
PICKED = -3e38

FOX_HEADS = 7
NSA_HEADS = 8
NSA_GROUPS = 2
NSA_HPG = NSA_HEADS // NSA_GROUPS
NSA_CMP_LEN = 32
NSA_CMP_STRIDE = 16
NSA_SEL_BLOCK = 64
NSA_TOP_N = 16
NSA_WINDOW = 512
NSA_FORCE_SCORE = 1e4
DIL_CONFIGS = ((128, 1), (512, 4), (2048, 16))
DIL_HPG = 3
DIL_HEADS = DIL_HPG * len(DIL_CONFIGS)
DIFF_HEADS = 4
ALIBI_MAX_EXP = 8.0

_SPLITS = (
    ("fox_q", 448), ("fox_k", 448), ("fox_v", 448), ("fox_f", 7),
    ("nsa_q", 512), ("nsa_cmp_k", 128), ("nsa_cmp_v", 128), ("nsa_slc_k", 128),
    ("nsa_slc_v", 128), ("nsa_win_k", 128), ("nsa_win_v", 128), ("nsa_gate", 24),
    ("dil_q", 576), ("dil_k", 576), ("dil_v", 576),
    ("diff_q", 512), ("diff_k", 512), ("diff_v", 512),
)
_SRC = {}
_o = 0
for _n, _w in _SPLITS:
    _SRC[_n] = _o
    _o += _w
N_IN = _o

DIL_W = 768
P_DILQ, P_DILK, P_DILV = 0, 768, 1536
P_NSAKV = 2304
P_FOXQ, P_FOXK, P_FOXV = 3072, 3584, 4096
P_NSAQ = 4608
P_DIFQ, P_DIFK, P_DIFV = 5120, 5632, 6144
N_PB = 6656
GATE_COL = 8


def _proj_colmaps():
    pb = -np.ones(N_PB, np.int64)
    ar = np.arange
    for name, dst in (("fox_q", P_FOXQ), ("fox_k", P_FOXK), ("fox_v", P_FOXV)):
        pb[dst:dst + 448] = _SRC[name] + ar(448)
    for j in range(NSA_HPG):
        for g in range(NSA_GROUPS):
            d = P_NSAQ + j * LANES + g * HALF
            pb[d:d + HALF] = _SRC["nsa_q"] + (g * NSA_HPG + j) * HEAD_DIM + ar(HALF)
    for i, name in enumerate(("nsa_cmp_k", "nsa_cmp_v", "nsa_slc_k", "nsa_slc_v", "nsa_win_k", "nsa_win_v")):
        pb[P_NSAKV + i * LANES:P_NSAKV + (i + 1) * LANES] = _SRC[name] + ar(LANES)
    for name, dst in (("dil_q", P_DILQ), ("dil_k", P_DILK), ("dil_v", P_DILV)):
        for g in range(len(DIL_CONFIGS)):
            for jj in range(DIL_HPG):
                d = dst + (2 * g + jj // 2) * LANES + (jj % 2) * HALF
                pb[d:d + HALF] = _SRC[name] + (g * DIL_HPG + jj) * HEAD_DIM + ar(HALF)
    for name, dst in (("diff_q", P_DIFQ), ("diff_k", P_DIFK), ("diff_v", P_DIFV)):
        pb[dst:dst + 512] = _SRC[name] + ar(512)
    pf = -np.ones(LANES, np.int64)
    pf[0:FOX_HEADS] = _SRC["fox_f"] + ar(FOX_HEADS)
    pf[GATE_COL:GATE_COL + 24] = _SRC["nsa_gate"] + ar(24)
    return pb, pf


O_DIL, O_FOX, O_NSA, O_DIF = 0, 768, 1280, 1792
N_O = 2304


def _out_rowmap():
    m = -np.ones(N_O, np.int64)
    ar = np.arange
    m[O_FOX:O_FOX + 448] = ar(448)
    for j in range(NSA_HPG):
        for g in range(NSA_GROUPS):
            d = O_NSA + j * LANES + g * HALF
            m[d:d + HALF] = 448 + (g * NSA_HPG + j) * HEAD_DIM + ar(HALF)
    for g in range(len(DIL_CONFIGS)):
        for jj in range(DIL_HPG):
            d = O_DIL + (2 * g + jj // 2) * LANES + (jj % 2) * HALF
            m[d:d + HALF] = 960 + (g * DIL_HPG + jj) * HEAD_DIM + ar(HALF)
    m[O_DIF:O_DIF + 512] = 1536 + ar(512)
    return m


def _gather_axis(x, idx, axis):
    pieces = []
    i = 0
    n = len(idx)
    while i < n:
        j = i + 1
        if idx[i] < 0:
            while j < n and idx[j] < 0:
                j += 1
            shape = list(x.shape)
            shape[axis] = j - i
            pieces.append(jnp.zeros(shape, x.dtype))
        else:
            while j < n and idx[j] == idx[j - 1] + 1:
                j += 1
            pieces.append(lax.slice_in_dim(x, int(idx[i]), int(idx[j - 1]) + 1, axis=axis))
        i = j
    return jnp.concatenate(pieces, axis=axis)


def _cparams(sem, vmem_mb=48):
    return pltpu.CompilerParams(dimension_semantics=sem, vmem_limit_bytes=vmem_mb * 1024 * 1024)


def _alibi_slope(k, n_heads):
    return float(2.0 ** (-ALIBI_MAX_EXP * (k + 1) / n_heads))


def _mod_kernel(c_ref, w_ref, b_ref, o_ref):
    nb = c_ref.shape[0]
    tn = w_ref.shape[2]
    for b in range(nb):
        cb = c_ref[b]
        ca = cb * jax.nn.sigmoid(cb)
        for j in range(tn // LANES):
            sl = slice(j * LANES, (j + 1) * LANES)
            w = w_ref[0, :, sl]
            o_ref[0, b:b + 1, sl] = jnp.sum(w * ca, axis=0, keepdims=True) + b_ref[0, :, sl]


def _modulation(c, ada_w, ada_b, tn=512):
    depth, d, n = ada_w.shape
    nb = c.shape[0]
    c_rep = jnp.broadcast_to(c[:, :, None], (nb, d, LANES))
    return pl.pallas_call(
        _mod_kernel,
        grid=(depth, n // tn),
        in_specs=[
            pl.BlockSpec((nb, d, LANES), lambda l, j: (0, 0, 0)),
            pl.BlockSpec((1, d, tn), lambda l, j: (l, 0, j)),
            pl.BlockSpec((1, 1, tn), lambda l, j: (l, 0, j)),
        ],
        out_specs=pl.BlockSpec((1, nb, tn), lambda l, j: (l, 0, j)),
        out_shape=jax.ShapeDtypeStruct((depth, nb, n), F32),
        compiler_params=_cparams(("parallel", "parallel")),
        name="adaln_mod",
    )(c_rep, ada_w, ada_b.reshape(depth, 1, n))


def _norm_mod(h_ref, g_ref, sc_ref, sh_ref):
    x = h_ref[...]
    ms = jnp.mean(x * x, axis=-1, keepdims=True)
    y = x * lax.rsqrt(ms + RMS_EPS) * g_ref[...]
    return (y * (1.0 + sc_ref[0]) + sh_ref[0]).astype(BF16)


def _proj_kernel(h_ref, g_ref, sc_ref, sh_ref, w_ref, wf_ref, pb_ref, pf_ref, u_ref):
    @pl.when(pl.program_id(1) == 0)
    def _():
        u = _norm_mod(h_ref, g_ref, sc_ref, sh_ref)
        u_ref[...] = u
        pf_ref[...] = jnp.dot(u, wf_ref[...], preferred_element_type=F32)

    pb_ref[...] = jnp.dot(u_ref[...], w_ref[...], preferred_element_type=F32).astype(BF16)


def _norm_proj(h, gain, sc, sh, w, wf, seq, tm=1024, tn=512):
    m, d = h.shape
    n = w.shape[1]
    per = seq // tm
    return pl.pallas_call(
        _proj_kernel,
        grid=(m // tm, n // tn),
        in_specs=[
            pl.BlockSpec((tm, d), lambda i, j: (i, 0)),
            pl.BlockSpec((1, d), lambda i, j: (0, 0)),
            pl.BlockSpec((1, 1, d), lambda i, j: (i // per, 0, 0)),
            pl.BlockSpec((1, 1, d), lambda i, j: (i // per, 0, 0)),
            pl.BlockSpec((d, tn), lambda i, j: (0, j)),
            pl.BlockSpec((d, LANES), lambda i, j: (0, 0)),
        ],
        out_specs=[
            pl.BlockSpec((tm, tn), lambda i, j: (i, j)),
            pl.BlockSpec((tm, LANES), lambda i, j: (i, 0)),
        ],
        out_shape=[jax.ShapeDtypeStruct((m, n), BF16), jax.ShapeDtypeStruct((m, LANES), F32)],
        scratch_shapes=[pltpu.VMEM((tm, d), BF16)],
        compiler_params=_cparams(("parallel", "arbitrary")),
        name="norm_in_proj",
    )(h, gain.reshape(1, d), sc, sh, w, wf)


def _ffn_up_kernel(h_ref, g_ref, sc_ref, sh_ref, wg_ref, wu_ref, a_ref, u_ref):
    @pl.when(pl.program_id(1) == 0)
    def _():
        u_ref[...] = _norm_mod(h_ref, g_ref, sc_ref, sh_ref)

    u = u_ref[...]
    gate = jnp.dot(u, wg_ref[...], preferred_element_type=F32)
    up = jnp.dot(u, wu_ref[...], preferred_element_type=F32)
    a_ref[...] = (gate * jax.nn.sigmoid(gate) * up).astype(BF16)


def _norm_ffn_up(h, gain, sc, sh, wg, wu, seq, tm=1024, tn=512):
    m, d = h.shape
    n = wg.shape[1]
    per = seq // tm
    return pl.pallas_call(
        _ffn_up_kernel,
        grid=(m // tm, n // tn),
        in_specs=[
            pl.BlockSpec((tm, d), lambda i, j: (i, 0)),
            pl.BlockSpec((1, d), lambda i, j: (0, 0)),
            pl.BlockSpec((1, 1, d), lambda i, j: (i // per, 0, 0)),
            pl.BlockSpec((1, 1, d), lambda i, j: (i // per, 0, 0)),
            pl.BlockSpec((d, tn), lambda i, j: (0, j)),
            pl.BlockSpec((d, tn), lambda i, j: (0, j)),
        ],
        out_specs=pl.BlockSpec((tm, tn), lambda i, j: (i, j)),
        out_shape=jax.ShapeDtypeStruct((m, n), BF16),
        scratch_shapes=[pltpu.VMEM((tm, d), BF16)],
        compiler_params=_cparams(("parallel", "arbitrary")),
        name="norm_ffn_up",
    )(h, gain.reshape(1, d), sc, sh, wg, wu)


def _mm_res_kernel(a_ref, w_ref, h_ref, g_ref, o_ref):
    acc = jnp.dot(a_ref[...], w_ref[...], preferred_element_type=F32)
    o_ref[...] = h_ref[...] + g_ref[0] * acc


def _matmul_gated_residual(a, w, h, gate, seq, tm=512, tn=512, name="mm_res"):
    m, k = a.shape
    n = w.shape[1]
    per = seq // tm
    return pl.pallas_call(
        _mm_res_kernel,
        grid=(m // tm, n // tn),
        in_specs=[
            pl.BlockSpec((tm, k), lambda i, j: (i, 0)),
            pl.BlockSpec((k, tn), lambda i, j: (0, j)),
            pl.BlockSpec((tm, tn), lambda i, j: (i, j)),
            pl.BlockSpec((1, 1, tn), lambda i, j: (i // per, 0, j)),
        ],
        out_specs=pl.BlockSpec((tm, tn), lambda i, j: (i, j)),
        out_shape=jax.ShapeDtypeStruct((m, n), F32),
        compiler_params=_cparams(("parallel", "arbitrary")),
        name=name,
    )(a, w, h, gate)


def _final_norm_kernel(h_ref, g_ref, o_ref):
    x = h_ref[...]
    ms = jnp.mean(x * x, axis=-1, keepdims=True)
    o_ref[...] = x * lax.rsqrt(ms + RMS_EPS) * g_ref[...]


def _final_norm(h, gain, tm=512):
    m, d = h.shape
    return pl.pallas_call(
        _final_norm_kernel,
        grid=(m // tm,),
        in_specs=[pl.BlockSpec((tm, d), lambda i: (i, 0)), pl.BlockSpec((1, d), lambda i: (0, 0))],
        out_specs=pl.BlockSpec((tm, d), lambda i: (i, 0)),
        out_shape=jax.ShapeDtypeStruct((m, d), F32),
        compiler_params=_cparams(("parallel",)),
        name="final_norm",
    )(h, gain.reshape(1, d))


def _half_mask(shape, half):
    lane = lax.broadcasted_iota(jnp.int32, shape, 1)
    return (lane < HALF) if half == 0 else (lane >= HALF)


def _pick_half(qp, half):
    return jnp.where(_half_mask(qp.shape, half), qp, jnp.zeros_like(qp)) * (HEAD_DIM ** -0.5)


def _qk(qh, k):
    return lax.dot_general(qh, k, (((1,), (1,)), ((), ())), preferred_element_type=F32)


def _attend_tile(s, valid, v, m, l, acc):
    m_new = jnp.maximum(m, jnp.max(s, axis=1, keepdims=True))
    alpha = jnp.exp(m - m_new)
    p = jnp.exp(s - m_new)
    if valid is not None:
        p = jnp.where(valid, p, 0.0)
    l_new = alpha * l + jnp.sum(p, axis=1, keepdims=True)
    acc_new = alpha * acc + jnp.dot(p.astype(BF16), v, preferred_element_type=F32)
    return m_new, l_new, acc_new


def _flash_init(tq, width=LANES):
    return (jnp.full((tq, 1), NEG, F32), jnp.zeros((tq, 1), F32), jnp.zeros((tq, width), F32))


def _merge_halves(lo, hi):
    if hi is None:
        hi = jnp.zeros_like(lo)
    return jnp.where(_half_mask(lo.shape, 0), lo, hi)


def _cumsum_kernel(pf_ref, fb_ref, o_ref, *, chunk):
    seq = pf_ref.shape[1]
    r = lax.broadcasted_iota(jnp.int32, (chunk, chunk), 0)
    c = lax.broadcasted_iota(jnp.int32, (chunk, chunk), 1)
    tri = jnp.where(r >= c, 1.0, 0.0).astype(BF16)

    def body(i, carry):
        st = pl.multiple_of(i * chunk, chunk)
        x = pf_ref[0, pl.ds(st, chunk), :] + fb_ref[...]
        ls = jnp.minimum(x, 0.0) - jnp.log1p(jnp.exp(-jnp.abs(x)))
        hi = ls.astype(BF16)
        r1 = ls - hi.astype(F32)
        mid = r1.astype(BF16)
        lo = (r1 - mid.astype(F32)).astype(BF16)
        cs = (jnp.dot(tri, hi, preferred_element_type=F32)
              + jnp.dot(tri, mid, preferred_element_type=F32)
              + jnp.dot(tri, lo, preferred_element_type=F32))
        out = cs + carry
        o_ref[0, pl.ds(st, chunk), :] = out
        return out[chunk - 1:chunk, :]

    lax.fori_loop(0, seq // chunk, body, jnp.zeros((1, LANES), F32))


def _forget_cumsum(pf, f_bias, chunk=256):
    nb, seq, _ = pf.shape
    fb = jnp.zeros((1, LANES), F32).at[0, :FOX_HEADS].set(f_bias.astype(F32))
    return pl.pallas_call(
        functools.partial(_cumsum_kernel, chunk=chunk),
        grid=(nb,),
        in_specs=[pl.BlockSpec((1, seq, LANES), lambda b: (b, 0, 0)), pl.BlockSpec((1, LANES), lambda b: (0, 0))],
        out_specs=pl.BlockSpec((1, seq, LANES), lambda b: (b, 0, 0)),
        out_shape=jax.ShapeDtypeStruct((nb, seq, LANES), F32),
        compiler_params=_cparams(("parallel",)),
        name="forget_cumsum",
    )(pf, fb)


def _fox_kernel(q_ref, k_ref, v_ref, cc_ref, cr_ref, o_ref, *, tq):
    qi = pl.program_id(1)
    row = lax.broadcasted_iota(jnp.int32, (tq, tq), 0)
    col = lax.broadcasted_iota(jnp.int32, (tq, tq), 1)
    causal = row >= col
    for p in range(4):
        sl = slice(p * LANES, (p + 1) * LANES)
        qp = q_ref[0, :, sl]
        outs = []
        for half in range(2):
            h = 2 * p + half
            if h >= FOX_HEADS:
                outs.append(None)
                continue
            qh = _pick_half(qp, half)
            cq = cc_ref[0, :, h:h + 1]

            def body(ki, carry, masked=False, qh=qh, cq=cq, h=h, sl=sl):
                ks = pl.multiple_of(ki * tq, tq)
                k = k_ref[0, pl.ds(ks, tq), sl]
                v = v_ref[0, pl.ds(ks, tq), sl]
                ck = cr_ref[0, h:h + 1, pl.ds(ks, tq)]
                s = _qk(qh, k) + (cq - ck)
                if masked:
                    s = jnp.where(causal, s, NEG)
                return _attend_tile(s, None, v, *carry)

            carry = lax.fori_loop(0, qi, body, _flash_init(tq))
            m, l, acc = body(qi, carry, masked=True)
            outs.append(acc / l)
        o_ref[0, :, sl] = _merge_halves(outs[0], outs[1]).astype(BF16)


def _fox_attention(pb, cum, cumrow, tq=256):
    nb, seq, _ = pb.shape
    w = 4 * LANES
    return pl.pallas_call(
        functools.partial(_fox_kernel, tq=tq),
        grid=(nb, seq // tq),
        in_specs=[
            pl.BlockSpec((1, tq, w), lambda b, i: (b, i, P_FOXQ // w)),
            pl.BlockSpec((1, seq, w), lambda b, i: (b, 0, P_FOXK // w)),
            pl.BlockSpec((1, seq, w), lambda b, i: (b, 0, P_FOXV // w)),
            pl.BlockSpec((1, tq, LANES), lambda b, i: (b, i, 0)),
            pl.BlockSpec((1, 8, seq), lambda b, i: (b, 0, 0)),
        ],
        out_specs=pl.BlockSpec((1, tq, w), lambda b, i: (b, i, 0)),
        out_shape=jax.ShapeDtypeStruct((nb, seq, w), BF16),
        compiler_params=_cparams(("parallel", "arbitrary")),
        name="fox_attention",
    )(pb, pb, pb, cum, cumrow)


def _diff_kernel(q_ref, k_ref, v_ref, lam_ref, g_ref, o_ref, *, tq, lam_init):
    qi = pl.program_id(1)
    qs = qi * tq
    row = lax.broadcasted_iota(jnp.int32, (tq, tq), 0)
    col = lax.broadcasted_iota(jnp.int32, (tq, tq), 1)
    causal = row >= col
    dist0 = (row - col).astype(F32)
    lv = lam_ref[...]
    lam = (jnp.exp(jnp.sum(lv[0:1] * lv[1:2], axis=1, keepdims=True))
           - jnp.exp(jnp.sum(lv[2:3] * lv[3:4], axis=1, keepdims=True)) + lam_init)
    for h in range(DIFF_HEADS):
        sl = slice(h * LANES, (h + 1) * LANES)
        slope = _alibi_slope(h, DIFF_HEADS)
        base = dist0 * (-slope)
        qp = q_ref[0, :, sl]
        res = []
        for half in range(2):
            qh = _pick_half(qp, half)

            def body(ki, carry, masked=False, qh=qh, sl=sl, base=base, slope=slope):
                ks = pl.multiple_of(ki * tq, tq)
                k = k_ref[0, pl.ds(ks, tq), sl]
                v = v_ref[0, pl.ds(ks, tq), sl]
                off = (qs - ks).astype(F32) * (-slope)
                s = _qk(qh, k) + (base + off)
                if masked:
                    s = jnp.where(causal, s, NEG)
                return _attend_tile(s, None, v, *carry)

            carry = lax.fori_loop(0, qi, body, _flash_init(tq))
            m, l, acc = body(qi, carry, masked=True)
            res.append(acc / l)
        o = res[0] - lam * res[1]
        ms = jnp.mean(o * o, axis=-1, keepdims=True)
        y = o * lax.rsqrt(ms + RMS_EPS) * g_ref[...]
        o_ref[0, :, sl] = (y * (1.0 - lam_init)).astype(BF16)


def _diff_attention(pb, lam_vecs, subln_g, lam_init, tq=256):
    nb, seq, _ = pb.shape
    w = DIFF_HEADS * LANES
    return pl.pallas_call(
        functools.partial(_diff_kernel, tq=tq, lam_init=lam_init),
        grid=(nb, seq // tq),
        in_specs=[
            pl.BlockSpec((1, tq, w), lambda b, i: (b, i, P_DIFQ // w)),
            pl.BlockSpec((1, seq, w), lambda b, i: (b, 0, P_DIFK // w)),
            pl.BlockSpec((1, seq, w), lambda b, i: (b, 0, P_DIFV // w)),
            pl.BlockSpec((4, HEAD_DIM), lambda b, i: (0, 0)),
            pl.BlockSpec((1, LANES), lambda b, i: (0, 0)),
        ],
        out_specs=pl.BlockSpec((1, tq, w), lambda b, i: (b, i, 0)),
        out_shape=jax.ShapeDtypeStruct((nb, seq, w), BF16),
        compiler_params=_cparams(("parallel", "arbitrary")),
        name="diff_attention",
    )(pb, pb, pb, lam_vecs.astype(F32), subln_g.astype(F32).reshape(1, LANES))


def _dil_kernel(q_ref, k_ref, v_ref, o_ref, *, tq):
    qi = pl.program_id(1)
    qs = qi * tq
    row = lax.broadcasted_iota(jnp.int32, (tq, tq), 0)
    col = lax.broadcasted_iota(jnp.int32, (tq, tq), 1)
    dist0 = row - col
    outs = {}
    lses = {}
    for g, (window, dilation) in enumerate(DIL_CONFIGS):
        nback = -(-window // tq)
        lo = jnp.maximum(qi - nback, 0)
        for jj in range(DIL_HPG):
            blk = 2 * g + jj // 2
            sl = slice(blk * LANES, (blk + 1) * LANES)
            slope = _alibi_slope(g * DIL_HPG + jj, DIL_HEADS)
            qh = _pick_half(q_ref[0, :, sl], jj % 2)

            def body(ki, carry, qh=qh, sl=sl, slope=slope, window=window, dilation=dilation):
                ks = pl.multiple_of(ki * tq, tq)
                k = k_ref[0, pl.ds(ks, tq), sl]
                v = v_ref[0, pl.ds(ks, tq), sl]
                dist = dist0 + (qs - ks)
                bad = (dist & (dilation - 1)) | ((dist | (window - dist)) >> 31)
                valid = bad == 0
                s = _qk(qh, k) - slope * dist.astype(F32)
                s = jnp.where(valid, s, NEG)
                return _attend_tile(s, valid, v, *carry)

            m, l, acc = lax.fori_loop(lo, qi + 1, body, _flash_init(tq))
            outs[(g, jj)] = acc / l
            lses[(g, jj)] = m + jnp.log(l)
    ng = len(DIL_CONFIGS)
    for jj in range(DIL_HPG):
        mx = lses[(0, jj)]
        for g in range(1, ng):
            mx = jnp.maximum(mx, lses[(g, jj)])
        es = [jnp.exp(lses[(g, jj)] - mx) for g in range(ng)]
        tot = es[0]
        for g in range(1, ng):
            tot = tot + es[g]
        for g in range(ng):
            outs[(g, jj)] = outs[(g, jj)] * (es[g] / tot)
    for g in range(ng):
        o_ref[0, :, (2 * g) * LANES:(2 * g + 1) * LANES] = _merge_halves(outs[(g, 0)], outs[(g, 1)]).astype(BF16)
        o_ref[0, :, (2 * g + 1) * LANES:(2 * g + 2) * LANES] = _merge_halves(outs[(g, 2)], None).astype(BF16)


def _dil_attention(pb, tq=256):
    nb, seq, _ = pb.shape
    return pl.pallas_call(
        functools.partial(_dil_kernel, tq=tq),
        grid=(nb, seq // tq),
        in_specs=[
            pl.BlockSpec((1, tq, DIL_W), lambda b, i: (b, i, P_DILQ // DIL_W)),
            pl.BlockSpec((1, seq, DIL_W), lambda b, i: (b, 0, P_DILK // DIL_W)),
            pl.BlockSpec((1, seq, DIL_W), lambda b, i: (b, 0, P_DILV // DIL_W)),
        ],
        out_specs=pl.BlockSpec((1, tq, DIL_W), lambda b, i: (b, i, 0)),
        out_shape=jax.ShapeDtypeStruct((nb, seq, DIL_W), BF16),
        compiler_params=_cparams(("parallel", "arbitrary"), vmem_mb=56),
        name="dil_attention",
    )(pb, pb, pb)


def _gelu_tanh(x):
    return 0.5 * x * (1.0 + jnp.tanh(math.sqrt(2.0 / math.pi) * (x + 0.044715 * (x * x * x))))


def _compress_kernel(rk_ref, rv_ref, w1_ref, w2_ref, pos_ref, kc_ref, vc_ref):
    nc = rk_ref.shape[1]
    rowi = lax.broadcasted_iota(jnp.int32, (nc, LANES), 0)
    for kv, (r_ref, o_ref) in enumerate(((rk_ref, kc_ref), (rv_ref, vc_ref))):
        r = r_ref[0]
        a = jnp.dot(r, w1_ref[kv, 0], preferred_element_type=F32)
        b = jnp.dot(r, w1_ref[kv, 1], preferred_element_type=F32)
        pt = (jnp.dot(pos_ref[kv, 0], w1_ref[kv, 0], preferred_element_type=F32)
              + jnp.dot(pos_ref[kv, 1], w1_ref[kv, 1], preferred_element_type=F32))[0:1]
        pre = a + pltpu.roll(b, nc - 1, 0) + pt
        out = jnp.dot(_gelu_tanh(pre).astype(BF16), w2_ref[kv], preferred_element_type=F32)
        o_ref[0] = jnp.where(rowi < nc - 1, out, 0.0).astype(BF16)


def _compress_weights(cmp_w1, cmp_w2, cmp_pos):
    half = NSA_CMP_LEN // 2
    w1 = cmp_w1.reshape(2, 2, half, HEAD_DIM, HEAD_DIM)
    z = jnp.zeros_like(w1)
    w1x = jnp.concatenate([jnp.concatenate([w1, z], -1), jnp.concatenate([z, w1], -1)], -2)
    w1x = w1x.reshape(2, 2, half * LANES, LANES).astype(BF16)
    z2 = jnp.zeros_like(cmp_w2)
    w2x = jnp.concatenate([jnp.concatenate([cmp_w2, z2], -1), jnp.concatenate([z2, cmp_w2], -1)], -2).astype(BF16)
    pos = cmp_pos.reshape(2, 2, half, 1, HEAD_DIM)
    posx = jnp.broadcast_to(pos, (2, 2, half, NSA_GROUPS, HEAD_DIM)).reshape(2, 2, 1, half * LANES)
    posx = jnp.broadcast_to(posx, (2, 2, 8, half * LANES)).astype(BF16)
    return w1x, w2x, posx


def _nsa_compress(rk, rv, w1x, w2x, posx):
    nb, nc, width = rk.shape
    full = lambda a: pl.BlockSpec(a.shape, lambda b: (0,) * a.ndim)
    return pl.pallas_call(
        _compress_kernel,
        grid=(nb,),
        in_specs=[pl.BlockSpec((1, nc, width), lambda b: (b, 0, 0)), pl.BlockSpec((1, nc, width), lambda b: (b, 0, 0)),
                  full(w1x), full(w2x), full(posx)],
        out_specs=[pl.BlockSpec((1, nc, LANES), lambda b: (b, 0, 0))] * 2,
        out_shape=[jax.ShapeDtypeStruct((nb, nc, LANES), BF16)] * 2,
        compiler_params=_cparams(("parallel",)),
        name="nsa_compress",
    )(rk, rv, w1x, w2x, posx)


def _cmp_to_sel_matrix(n_cmp_pad, n_cmp, n_sel):
    a = NSA_SEL_BLOCK // NSA_CMP_STRIDE
    b = NSA_CMP_LEN // NSA_CMP_STRIDE
    j = np.arange(n_sel)[:, None, None]
    idx = a * j + np.arange(a)[None, :, None] - np.arange(b)[None, None, :]
    jj = np.broadcast_to(j, idx.shape)
    ok = (idx >= 0) & (idx < n_cmp)
    m = np.zeros((n_cmp_pad, LANES), np.float32)
    np.add.at(m, (idx[ok], jj[ok]), 1.0)
    return m


def _nsa_kernel(q_ref, kc_ref, vc_ref, ks_ref, vs_ref, kw_ref, vw_ref, pf_ref, e_ref, m2_ref, o_ref, mb_ref,
                *, tq, n_sel, n_top):
    qi = pl.program_id(1)
    qs = qi * tq
    nc = kc_ref.shape[1]
    lane = lax.broadcasted_iota(jnp.int32, (tq, LANES), 1)
    tpos = qs + lax.broadcasted_iota(jnp.int32, (tq, LANES), 0)
    row = lax.broadcasted_iota(jnp.int32, (tq, tq), 0)
    col = lax.broadcasted_iota(jnp.int32, (tq, tq), 1)
    dist0 = row - col

    def slope_of(g, j):
        return _alibi_slope(g * NSA_HPG + j, NSA_HEADS)

    qh = {}
    for j in range(NSA_HPG):
        qp = q_ref[0, :, j * LANES:(j + 1) * LANES]
        for g in range(NSA_GROUPS):
            qh[(g, j)] = _pick_half(qp, g)

    cidx = lax.broadcasted_iota(jnp.int32, (tq, nc), 1)
    dist_c = (qs + lax.broadcasted_iota(jnp.int32, (tq, nc), 0)) - (cidx * NSA_CMP_STRIDE + (NSA_CMP_LEN - 1))
    valid_c = (dist_c | ((nc - 2) - cidx)) >= 0
    dist_cf = dist_c.astype(F32)
    kc = kc_ref[0]
    vc = vc_ref[0]
    o_c = {}
    psum = [None] * NSA_GROUPS
    for g in range(NSA_GROUPS):
        for j in range(NSA_HPG):
            s = _qk(qh[(g, j)], kc) - slope_of(g, j) * dist_cf
            s = jnp.where(valid_c, s, NEG)
            m = jnp.max(s, axis=1, keepdims=True)
            p = jnp.where(valid_c, jnp.exp(s - m), 0.0)
            l = jnp.sum(p, axis=1, keepdims=True)
            pn = p / jnp.where(l > 0.0, l, 1.0)
            o_c[(g, j)] = jnp.dot(pn.astype(BF16), vc, preferred_element_type=F32)
            psum[g] = pn if psum[g] is None else psum[g] + pn

    lanef = lane.astype(F32)
    cur = tpos >> 6
    forced = (lane == 0) | (lane == cur) | (lane == cur - 1)
    causal_b = lane * NSA_SEL_BLOCK <= tpos
    o_s = {}
    for g in range(NSA_GROUPS):
        ph = psum[g].astype(BF16)
        plo = (psum[g] - ph.astype(F32)).astype(BF16)
        imp = (jnp.dot(ph, m2_ref[...], preferred_element_type=F32)
               + jnp.dot(plo, m2_ref[...], preferred_element_type=F32))
        score = jnp.where(causal_b, jnp.where(forced, NSA_FORCE_SCORE, imp), NEG)
        score = jnp.where(lane < n_sel, score, PICKED)
        sel = jnp.zeros((tq, LANES), F32)
        for _ in range(n_top):
            mx = jnp.max(score, axis=1, keepdims=True)
            idx = jnp.min(jnp.where(score == mx, lanef, float(LANES)), axis=1, keepdims=True)
            hit = lanef == idx
            sel = jnp.where(hit, 1.0, sel)
            score = jnp.where(hit, PICKED, score)
        sel_b = sel.astype(BF16)

        def mask_body(ki, c, sel_b=sel_b):
            ks = pl.multiple_of(ki * tq, tq)
            mf = jnp.dot(sel_b, e_ref[:, pl.ds(ks, tq)], preferred_element_type=F32)
            ok = jnp.where(dist0 + (qs - ks) >= 0, mf, 0.0) > 0.5
            mb_ref[:, pl.ds(ks, tq)] = jnp.where(ok, 0.0, NEG)
            return c

        lax.fori_loop(0, qi + 1, mask_body, 0)
        for j in range(NSA_HPG):
            def body(ki, carry, q=qh[(g, j)], slope=slope_of(g, j)):
                ks = pl.multiple_of(ki * tq, tq)
                k = ks_ref[0, pl.ds(ks, tq), :]
                v = vs_ref[0, pl.ds(ks, tq), :]
                mb = mb_ref[:, pl.ds(ks, tq)]
                s = _qk(q, k) - slope * (dist0 + (qs - ks)).astype(F32) + mb
                return _attend_tile(s, mb > -1.0, v, *carry)

            m, l, acc = lax.fori_loop(0, qi + 1, body, _flash_init(tq))
            o_s[(g, j)] = acc / l

    nback = -(-NSA_WINDOW // tq)
    lo = jnp.maximum(qi - nback, 0)
    o_w = {}
    for g in range(NSA_GROUPS):
        for j in range(NSA_HPG):
            def body(ki, carry, q=qh[(g, j)], slope=slope_of(g, j)):
                ks = pl.multiple_of(ki * tq, tq)
                k = kw_ref[0, pl.ds(ks, tq), :]
                v = vw_ref[0, pl.ds(ks, tq), :]
                dist = dist0 + (qs - ks)
                valid = (dist | ((NSA_WINDOW - 1) - dist)) >= 0
                s = jnp.where(valid, _qk(q, k) - slope * dist.astype(F32), NEG)
                return _attend_tile(s, valid, v, *carry)

            m, l, acc = lax.fori_loop(lo, qi + 1, body, _flash_init(tq))
            o_w[(g, j)] = acc / l

    gl = jax.nn.sigmoid(pf_ref[0])
    for j in range(NSA_HPG):
        comb = []
        for g in range(NSA_GROUPS):
            c0 = GATE_COL + (g * NSA_HPG + j) * 3
            comb.append(gl[:, c0:c0 + 1] * o_c[(g, j)] + gl[:, c0 + 1:c0 + 2] * o_s[(g, j)]
                        + gl[:, c0 + 2:c0 + 3] * o_w[(g, j)])
        o_ref[0, :, j * LANES:(j + 1) * LANES] = _merge_halves(comb[0], comb[1]).astype(BF16)


def _nsa_attention(pb, pf, kc, vc, tq=256):
    nb, seq, _ = pb.shape
    nc = kc.shape[1]
    n_sel = seq // NSA_SEL_BLOCK
    n_top = min(NSA_TOP_N, n_sel)
    n_cmp = seq // NSA_CMP_STRIDE - NSA_CMP_LEN // NSA_CMP_STRIDE + 1
    expand = (np.arange(LANES)[:, None] == (np.arange(seq)[None, :] // NSA_SEL_BLOCK))
    e = jnp.asarray(expand, BF16)
    m2 = jnp.asarray(_cmp_to_sel_matrix(nc, n_cmp, n_sel), BF16)
    w = NSA_HPG * LANES
    kvb = P_NSAKV // LANES
    kv_spec = lambda i: pl.BlockSpec((1, seq, LANES), lambda b, q, i=i: (b, 0, kvb + i))
    return pl.pallas_call(
        functools.partial(_nsa_kernel, tq=tq, n_sel=n_sel, n_top=n_top),
        grid=(nb, seq // tq),
        in_specs=[
            pl.BlockSpec((1, tq, w), lambda b, i: (b, i, P_NSAQ // w)),
            pl.BlockSpec((1, nc, LANES), lambda b, i: (b, 0, 0)),
            pl.BlockSpec((1, nc, LANES), lambda b, i: (b, 0, 0)),
            kv_spec(2), kv_spec(3), kv_spec(4), kv_spec(5),
            pl.BlockSpec((1, tq, LANES), lambda b, i: (b, i, 0)),
            pl.BlockSpec((LANES, seq), lambda b, i: (0, 0)),
            pl.BlockSpec((nc, LANES), lambda b, i: (0, 0)),
        ],
        out_specs=pl.BlockSpec((1, tq, w), lambda b, i: (b, i, 0)),
        out_shape=jax.ShapeDtypeStruct((nb, seq, w), BF16),
        scratch_shapes=[pltpu.VMEM((tq, seq), F32)],
        compiler_params=_cparams(("parallel", "arbitrary")),
        name="nsa_attention",
    )(pb, kc, vc, pb, pb, pb, pb, pf, e, m2)


def _mixers(pb, pf, f_bias, cmp_w1, cmp_w2, cmp_pos, lam_vecs, subln_g, lam_init):
    nb, seq, _ = pb.shape
    cum = _forget_cumsum(pf, f_bias)
    cumrow = jnp.transpose(cum[:, :, :8], (0, 2, 1))
    o_fox = _fox_attention(pb, cum, cumrow)
    nc = seq // NSA_CMP_STRIDE
    rk = pb[:, :, P_NSAKV:P_NSAKV + LANES].reshape(nb, nc, NSA_CMP_STRIDE * LANES)
    rv = pb[:, :, P_NSAKV + LANES:P_NSAKV + 2 * LANES].reshape(nb, nc, NSA_CMP_STRIDE * LANES)
    kc, vc = _nsa_compress(rk, rv, *_compress_weights(cmp_w1, cmp_w2, cmp_pos))
    o_nsa = _nsa_attention(pb, pf, kc, vc)
    o_dil = _dil_attention(pb)
    o_dif = _diff_attention(pb, lam_vecs, subln_g, lam_init)
    return jnp.concatenate([o_dil, o_fox, o_nsa, o_dif], axis=-1)


def kernel(x, c, ada_w, ada_b, norm_mix_g, norm_ffn_g, w_in, fox_f_bias, nsa_cmp_w1, nsa_cmp_w2, nsa_cmp_pos,
           diff_lambda, diff_subln_g, w_out, ffn_w_gate, ffn_w_up, ffn_w_down, final_norm_g):
    nb, seq, d = x.shape
    depth = ada_w.shape[0]
    m = nb * seq
    pb_map, pf_map = _proj_colmaps()
    o_map = _out_rowmap()
    mods = _modulation(c, ada_w, ada_b)
    h = x.reshape(m, d)
    for layer in range(depth):
        sh1, sc1, g1, sh2, sc2, g2 = [mods[layer, :, i * d:(i + 1) * d].reshape(nb, 1, d) for i in range(6)]
        lam_init = 0.8 - 0.6 * math.exp(-0.3 * layer)
        w_pb = _gather_axis(w_in[layer], pb_map, 1).astype(BF16)
        w_pf = _gather_axis(w_in[layer], pf_map, 1).astype(BF16)
        w_o = _gather_axis(w_out[layer], o_map, 0).astype(BF16)
        pb, pf = _norm_proj(h, norm_mix_g[layer], sc1, sh1, w_pb, w_pf, seq)
        o = _mixers(pb.reshape(nb, seq, N_PB), pf.reshape(nb, seq, LANES), fox_f_bias[layer],
                    nsa_cmp_w1[layer], nsa_cmp_w2[layer], nsa_cmp_pos[layer],
                    diff_lambda[layer], diff_subln_g[layer], lam_init)
        h = _matmul_gated_residual(o.reshape(m, N_O), w_o, h, g1, seq, name="out_proj_res")
        act = _norm_ffn_up(h, norm_ffn_g[layer], sc2, sh2,
                           ffn_w_gate[layer].astype(BF16), ffn_w_up[layer].astype(BF16), seq)
        h = _matmul_gated_residual(act, ffn_w_down[layer].astype(BF16), h, g2, seq, name="ffn_down_res")
    return _final_norm(h, final_norm_g).reshape(nb, seq, d)
```

```python
import functools
import math

import numpy as np
import jax
import jax.numpy as jnp
from jax import lax
from jax.experimental import pallas as pl
from jax.experimental.pallas import tpu as pltpu

F32 = jnp.float32
BF16 = jnp.bfloat16

LANES = 128
HEAD_DIM = 64
HALF = HEAD_DIM
RMS_EPS = 1e-6
NEG = -1e30
PICKED = -3e38

FOX_HEADS = 7
NSA_HEADS = 8
NSA_GROUPS = 2
NSA_HPG = NSA_HEADS // NSA_GROUPS
NSA_CMP_LEN = 32
NSA_CMP_STRIDE = 16
NSA_SEL_BLOCK = 64
NSA_TOP_N = 16
NSA_WINDOW = 512
NSA_FORCE_SCORE = 1e4
DIL_CONFIGS = ((128, 1), (512, 4), (2048, 16))
DIL_HPG = 3
DIL_HEADS = DIL_HPG * len(DIL_CONFIGS)
DIFF_HEADS = 4
ALIBI_MAX_EXP = 8.0

_SPLITS = (
    ("fox_q", 448), ("fox_k", 448), ("fox_v", 448), ("fox_f", 7),
    ("nsa_q", 512), ("nsa_cmp_k", 128), ("nsa_cmp_v", 128), ("nsa_slc_k", 128),
    ("nsa_slc_v", 128), ("nsa_win_k", 128), ("nsa_win_v", 128), ("nsa_gate", 24),
    ("dil_q", 576), ("dil_k", 576), ("dil_v", 576),
    ("diff_q", 512), ("diff_k", 512), ("diff_v", 512),
)
_SRC = {}
_o = 0
for _n, _w in _SPLITS:
    _SRC[_n] = _o
    _o += _w
N_IN = _o

DIL_GW = 256
DIL_W = 3 * DIL_GW
P_FOXQ, P_FOXK, P_FOXV = 0, 512, 1024
P_NSAQ = 1536
P_NSAKV = 2048
P_DILQ, P_DILK, P_DILV = 2816, 3584, 4352
P_DIFQ, P_DIFK, P_DIFV = 5120, 5632, 6144
N_PB = 6656
GATE_COL = 8
RELAYOUT_TILE = 512
P_FGATE, P_NGATE = N_PB, N_PB + RELAYOUT_TILE
N_WP = N_PB + 2 * RELAYOUT_TILE


def _proj_colmap():
    pb = -np.ones(N_WP, np.int64)
    ar = np.arange
    for name, dst in (("fox_q", P_FOXQ), ("fox_k", P_FOXK), ("fox_v", P_FOXV)):
        pb[dst:dst + 448] = _SRC[name] + ar(448)
    for j in range(NSA_HPG):
        for g in range(NSA_GROUPS):
            d = P_NSAQ + j * LANES + g * HALF
            pb[d:d + HALF] = _SRC["nsa_q"] + (g * NSA_HPG + j) * HEAD_DIM + ar(HALF)
    for i, name in enumerate(("nsa_cmp_k", "nsa_cmp_v", "nsa_slc_k", "nsa_slc_v", "nsa_win_k", "nsa_win_v")):
        pb[P_NSAKV + i * LANES:P_NSAKV + (i + 1) * LANES] = _SRC[name] + ar(LANES)
    for name, dst in (("dil_q", P_DILQ), ("dil_k", P_DILK), ("dil_v", P_DILV)):
        for g in range(len(DIL_CONFIGS)):
            for jj in range(DIL_HPG):
                d = dst + (2 * g + jj // 2) * LANES + (jj % 2) * HALF
                pb[d:d + HALF] = _SRC[name] + (g * DIL_HPG + jj) * HEAD_DIM + ar(HALF)
    for name, dst in (("diff_q", P_DIFQ), ("diff_k", P_DIFK), ("diff_v", P_DIFV)):
        pb[dst:dst + 512] = _SRC[name] + ar(512)
    pb[P_FGATE:P_FGATE + FOX_HEADS] = _SRC["fox_f"] + ar(FOX_HEADS)
    pb[P_NGATE + GATE_COL:P_NGATE + GATE_COL + 24] = _SRC["nsa_gate"] + ar(24)
    return pb


O_DIL, O_FOX, O_NSA, O_DIF = 0, 768, 1280, 1792
N_O = 2304


def _out_rowmap():
    m = -np.ones(N_O, np.int64)
    ar = np.arange
    m[O_FOX:O_FOX + 448] = ar(448)
    for j in range(NSA_HPG):
        for g in range(NSA_GROUPS):
            d = O_NSA + j * LANES + g * HALF
            m[d:d + HALF] = 448 + (g * NSA_HPG + j) * HEAD_DIM + ar(HALF)
    for g in range(len(DIL_CONFIGS)):
        for jj in range(DIL_HPG):
            d = O_DIL + (2 * g + jj // 2) * LANES + (jj % 2) * HALF
            m[d:d + HALF] = 960 + (g * DIL_HPG + jj) * HEAD_DIM + ar(HALF)
    m[O_DIF:O_DIF + 512] = 1536 + ar(512)
    return m


def _gather_axis(x, idx, axis):
    pieces = []
    i = 0
    n = len(idx)
    while i < n:
        j = i + 1
        if idx[i] < 0:
            while j < n and idx[j] < 0:
                j += 1
            shape = list(x.shape)
            shape[axis] = j - i
            pieces.append(jnp.zeros(shape, x.dtype))
        else:
            while j < n and idx[j] == idx[j - 1] + 1:
                j += 1
            pieces.append(lax.slice_in_dim(x, int(idx[i]), int(idx[j - 1]) + 1, axis=axis))
        i = j
    return jnp.concatenate(pieces, axis=axis)


def _cparams(sem, vmem_mb=48):
    return pltpu.CompilerParams(dimension_semantics=sem, vmem_limit_bytes=vmem_mb * 1024 * 1024)


def _alibi_slope(k, n_heads):
    return float(2.0 ** (-ALIBI_MAX_EXP * (k + 1) / n_heads))


def _relayout_plan(colmap, n_src, tile):
    n_tiles = len(colmap) // tile
    last = (n_src - 1) // tile
    win = np.zeros(n_tiles, np.int32)
    local = -np.ones((n_tiles, 1, tile), np.int32)
    for t in range(n_tiles):
        src = colmap[t * tile:(t + 1) * tile]
        used = src[src >= 0]
        a = min(int(used.min()) // tile, max(last - 1, 0))
        assert int(used.max()) < (a + 2) * tile, "tile sources exceed the two-block window"
        win[t] = a
        local[t, 0] = np.where(src >= 0, src - a * tile, -1)
    return win, local


def _relayout_kernel(win_ref, idx_ref, wa_ref, wb_ref, o_ref, *, n_src, tile):
    a = win_ref[pl.program_id(1)]
    idx = idx_ref[0]
    row = lax.broadcasted_iota(jnp.int32, (tile, tile), 0)
    col = lax.broadcasted_iota(jnp.int32, wa_ref.shape[1:], 1)
    acc = None
    for part, w_ref in enumerate((wa_ref, wb_ref)):
        sel = jnp.where(row + part * tile == idx, 1.0, 0.0).astype(BF16)
        w = jnp.where(col + (a + part) * tile < n_src, w_ref[0], 0.0).astype(BF16)
        d = jnp.dot(w, sel, preferred_element_type=F32)
        acc = d if acc is None else acc + d
    o_ref[0] = acc.astype(BF16)


def _relayout_w_in(w_in, colmap, tile=RELAYOUT_TILE):
    depth, d, n_src = w_in.shape
    win, local = _relayout_plan(colmap, n_src, tile)
    n_tiles = len(win)
    grid_spec = pltpu.PrefetchScalarGridSpec(
        num_scalar_prefetch=1,
        grid=(depth, n_tiles),
        in_specs=[
            pl.BlockSpec((1, 1, tile), lambda l, t, win: (t, 0, 0)),
            pl.BlockSpec((1, d, tile), lambda l, t, win: (l, 0, win[t])),
            pl.BlockSpec((1, d, tile), lambda l, t, win: (l, 0, win[t] + 1)),
        ],
        out_specs=pl.BlockSpec((1, d, tile), lambda l, t, win: (l, 0, t)),
    )
    return pl.pallas_call(
        functools.partial(_relayout_kernel, n_src=n_src, tile=tile),
        grid_spec=grid_spec,
        out_shape=jax.ShapeDtypeStruct((depth, d, n_tiles * tile), BF16),
        compiler_params=_cparams(("parallel", "arbitrary")),
        name="w_in_relayout",
    )(jnp.asarray(win), jnp.asarray(local), w_in, w_in)


def _mod_kernel(c_ref, w_ref, b_ref, o_ref):
    nb = c_ref.shape[0]
    tn = w_ref.shape[2]
    for b in range(nb):
        cb = c_ref[b]
        ca = cb * jax.nn.sigmoid(cb)
        for j in range(tn // LANES):
            sl = slice(j * LANES, (j + 1) * LANES)
            w = w_ref[0, :, sl]
            o_ref[0, b:b + 1, sl] = jnp.sum(w * ca, axis=0, keepdims=True) + b_ref[0, :, sl]


def _modulation(c, ada_w, ada_b, tn=512):
    depth, d, n = ada_w.shape
    nb = c.shape[0]
    c_rep = jnp.broadcast_to(c[:, :, None], (nb, d, LANES))
    return pl.pallas_call(
        _mod_kernel,
        grid=(depth, n // tn),
        in_specs=[
            pl.BlockSpec((nb, d, LANES), lambda l, j: (0, 0, 0)),
            pl.BlockSpec((1, d, tn), lambda l, j: (l, 0, j)),
            pl.BlockSpec((1, 1, tn), lambda l, j: (l, 0, j)),
        ],
        out_specs=pl.BlockSpec((1, nb, tn), lambda l, j: (l, 0, j)),
        out_shape=jax.ShapeDtypeStruct((depth, nb, n), F32),
        compiler_params=_cparams(("parallel", "parallel")),
        name="adaln_mod",
    )(c_rep, ada_w, ada_b.reshape(depth, 1, n))


def _norm_mod(h_ref, g_ref, sc_ref, sh_ref):
    x = h_ref[...]
    ms = jnp.mean(x * x, axis=-1, keepdims=True)
    y = x * lax.rsqrt(ms + RMS_EPS) * g_ref[...]
    return (y * (1.0 + sc_ref[0]) + sh_ref[0]).astype(BF16)


def _proj_kernel(h_ref, g_ref, sc_ref, sh_ref, w_ref, wf_ref, pb_ref, pf_ref, u_ref):
    @pl.when(pl.program_id(1) == 0)
    def _():
        u = _norm_mod(h_ref, g_ref, sc_ref, sh_ref)
        u_ref[...] = u
        pf_ref[...] = jnp.dot(u, wf_ref[...], preferred_element_type=F32)

    pb_ref[...] = jnp.dot(u_ref[...], w_ref[0], preferred_element_type=F32).astype(BF16)


def _norm_proj(h, gain, sc, sh, w_all, layer, wf, seq, tm=1024, tn=512):
    m, d = h.shape
    per = seq // tm
    return pl.pallas_call(
        _proj_kernel,
        grid=(m // tm, N_PB // tn),
        in_specs=[
            pl.BlockSpec((tm, d), lambda i, j: (i, 0)),
            pl.BlockSpec((1, d), lambda i, j: (0, 0)),
            pl.BlockSpec((1, 1, d), lambda i, j: (i // per, 0, 0)),
            pl.BlockSpec((1, 1, d), lambda i, j: (i // per, 0, 0)),
            pl.BlockSpec((1, d, tn), lambda i, j: (layer, 0, j)),
            pl.BlockSpec((d, LANES), lambda i, j: (0, 0)),
        ],
        out_specs=[
            pl.BlockSpec((tm, tn), lambda i, j: (i, j)),
            pl.BlockSpec((tm, LANES), lambda i, j: (i, 0)),
        ],
        out_shape=[jax.ShapeDtypeStruct((m, N_PB), BF16), jax.ShapeDtypeStruct((m, LANES), F32)],
        scratch_shapes=[pltpu.VMEM((tm, d), BF16)],
        compiler_params=_cparams(("parallel", "arbitrary")),
        name="norm_in_proj",
    )(h, gain.reshape(1, d), sc, sh, w_all, wf)


def _ffn_up_kernel(h_ref, g_ref, sc_ref, sh_ref, wg_ref, wu_ref, a_ref, u_ref):
    @pl.when(pl.program_id(1) == 0)
    def _():
        u_ref[...] = _norm_mod(h_ref, g_ref, sc_ref, sh_ref)

    u = u_ref[...]
    gate = jnp.dot(u, wg_ref[...], preferred_element_type=F32)
    up = jnp.dot(u, wu_ref[...], preferred_element_type=F32)
    a_ref[...] = (gate * jax.nn.sigmoid(gate) * up).astype(BF16)


def _norm_ffn_up(h, gain, sc, sh, wg, wu, seq, tm=1024, tn=512):
    m, d = h.shape
    n = wg.shape[1]
    per = seq // tm
    return pl.pallas_call(
        _ffn_up_kernel,
        grid=(m // tm, n // tn),
        in_specs=[
            pl.BlockSpec((tm, d), lambda i, j: (i, 0)),
            pl.BlockSpec((1, d), lambda i, j: (0, 0)),
            pl.BlockSpec((1, 1, d), lambda i, j: (i // per, 0, 0)),
            pl.BlockSpec((1, 1, d), lambda i, j: (i // per, 0, 0)),
            pl.BlockSpec((d, tn), lambda i, j: (0, j)),
            pl.BlockSpec((d, tn), lambda i, j: (0, j)),
        ],
        out_specs=pl.BlockSpec((tm, tn), lambda i, j: (i, j)),
        out_shape=jax.ShapeDtypeStruct((m, n), BF16),
        scratch_shapes=[pltpu.VMEM((tm, d), BF16)],
        compiler_params=_cparams(("parallel", "arbitrary")),
        name="norm_ffn_up",
    )(h, gain.reshape(1, d), sc, sh, wg, wu)


def _mm_res_kernel(a_ref, w_ref, h_ref, g_ref, o_ref):
    acc = jnp.dot(a_ref[...], w_ref[...], preferred_element_type=F32)
    o_ref[...] = h_ref[...] + g_ref[0] * acc


def _matmul_gated_residual(a, w, h, gate, seq, tm=512, tn=512, name="mm_res"):
    m, k = a.shape
    n = w.shape[1]
    per = seq // tm
    return pl.pallas_call(
        _mm_res_kernel,
        grid=(m // tm, n // tn),
        in_specs=[
            pl.BlockSpec((tm, k), lambda i, j: (i, 0)),
            pl.BlockSpec((k, tn), lambda i, j: (0, j)),
            pl.BlockSpec((tm, tn), lambda i, j: (i, j)),
            pl.BlockSpec((1, 1, tn), lambda i, j: (i // per, 0, j)),
        ],
        out_specs=pl.BlockSpec((tm, tn), lambda i, j: (i, j)),
        out_shape=jax.ShapeDtypeStruct((m, n), F32),
        compiler_params=_cparams(("parallel", "arbitrary")),
        name=name,
    )(a, w, h, gate)


def _final_norm_kernel(h_ref, g_ref, o_ref):
    x = h_ref[...]
    ms = jnp.mean(x * x, axis=-1, keepdims=True)
    o_ref[...] = x * lax.rsqrt(ms + RMS_EPS) * g_ref[...]


def _final_norm(h, gain, tm=512):
    m, d = h.shape
    return pl.pallas_call(
        _final_norm_kernel,
        grid=(m // tm,),
        in_specs=[pl.BlockSpec((tm, d), lambda i: (i, 0)), pl.BlockSpec((1, d), lambda i: (0, 0))],
        out_specs=pl.BlockSpec((tm, d), lambda i: (i, 0)),
        out_shape=jax.ShapeDtypeStruct((m, d), F32),
        compiler_params=_cparams(("parallel",)),
        name="final_norm",
    )(h, gain.reshape(1, d))


def _half_mask(shape, half):
    lane = lax.broadcasted_iota(jnp.int32, shape, 1)
    return (lane < HALF) if half == 0 else (lane >= HALF)


def _pick_half(qp, half):
    return jnp.where(_half_mask(qp.shape, half), qp, jnp.zeros_like(qp)) * (HEAD_DIM ** -0.5)


def _qk(qh, k):
    return lax.dot_general(qh, k, (((1,), (1,)), ((), ())), preferred_element_type=F32)


def _merge_halves(lo, hi):
    if hi is None:
        hi = jnp.zeros_like(lo)
    return jnp.where(_half_mask(lo.shape, 0), lo, hi)


def _flash_scratch(n_heads, tq):
    return [pltpu.VMEM((n_heads, tq, LANES), F32), pltpu.VMEM((n_heads, tq, LANES), F32),
            pltpu.VMEM((n_heads, tq, LANES), F32)]


def _flash_reset(m_ref, l_ref, acc_ref):
    m_ref[...] = jnp.full(m_ref.shape, NEG, F32)
    l_ref[...] = jnp.zeros(l_ref.shape, F32)
    acc_ref[...] = jnp.zeros(acc_ref.shape, F32)


def _flash_update(h, s, valid, v, m_ref, l_ref, acc_ref):
    m_old = m_ref[h]
    m_new = jnp.maximum(m_old, jnp.max(s, axis=1, keepdims=True))
    alpha = jnp.exp(m_old - m_new)
    p = jnp.exp(s - pltpu.repeat(m_new, s.shape[1] // LANES, axis=1))
    if valid is not None:
        p = jnp.where(valid, p, 0.0)
    l_ref[h] = alpha * l_ref[h] + jnp.sum(p, axis=1, keepdims=True)
    acc_ref[h] = alpha * acc_ref[h] + jnp.dot(p.astype(BF16), v, preferred_element_type=F32)
    m_ref[h] = m_new


def _softmax_pv(s, v):
    m = jnp.max(s, axis=1, keepdims=True)
    p = jnp.exp(s - m)
    l = jnp.sum(p, axis=1, keepdims=True)
    return jnp.dot(p.astype(BF16), v, preferred_element_type=F32) / l, m + jnp.log(l)


def _window_start(qs, back, width, seq, align):
    return pl.multiple_of(jnp.clip(qs - back, 0, seq - width), align)


def _cumsum_kernel(pf_ref, fb_ref, o_ref, *, chunk):
    seq = pf_ref.shape[1]
    r = lax.broadcasted_iota(jnp.int32, (chunk, chunk), 0)
    c = lax.broadcasted_iota(jnp.int32, (chunk, chunk), 1)
    tri = jnp.where(r >= c, 1.0, 0.0).astype(BF16)

    def body(i, carry):
        st = pl.multiple_of(i * chunk, chunk)
        x = pf_ref[0, pl.ds(st, chunk), :] + fb_ref[...]
        ls = jnp.minimum(x, 0.0) - jnp.log1p(jnp.exp(-jnp.abs(x)))
        hi = ls.astype(BF16)
        r1 = ls - hi.astype(F32)
        mid = r1.astype(BF16)
        lo = (r1 - mid.astype(F32)).astype(BF16)
        cs = (jnp.dot(tri, hi, preferred_element_type=F32)
              + jnp.dot(tri, mid, preferred_element_type=F32)
              + jnp.dot(tri, lo, preferred_element_type=F32))
        out = cs + carry
        o_ref[0, pl.ds(st, chunk), :] = out
        return out[chunk - 1:chunk, :]

    lax.fori_loop(0, seq // chunk, body, jnp.zeros((1, LANES), F32))


def _forget_cumsum(pf, f_bias, chunk=256):
    nb, seq, _ = pf.shape
    fb = jnp.zeros((1, LANES), F32).at[0, :FOX_HEADS].set(f_bias.astype(F32))
    return pl.pallas_call(
        functools.partial(_cumsum_kernel, chunk=chunk),
        grid=(nb,),
        in_specs=[pl.BlockSpec((1, seq, LANES), lambda b: (b, 0, 0)), pl.BlockSpec((1, LANES), lambda b: (0, 0))],
        out_specs=pl.BlockSpec((1, seq, LANES), lambda b: (b, 0, 0)),
        out_shape=jax.ShapeDtypeStruct((nb, seq, LANES), F32),
        compiler_params=_cparams(("parallel",)),
        name="forget_cumsum",
    )(pf, fb)


def _fox_kernel(q_ref, k_ref, v_ref, cc_ref, cr_ref, o_ref, qs_ref, cq_ref, m_ref, l_ref, acc_ref, *, t):
    qi = pl.program_id(1)
    _flash_reset(m_ref, l_ref, acc_ref)
    for h in range(FOX_HEADS):
        qs_ref[h] = _pick_half(q_ref[0, :, (h // 2) * LANES:(h // 2 + 1) * LANES], h % 2)
        cq_ref[h] = jnp.broadcast_to(cc_ref[0, :, h:h + 1], (t, LANES))

    def step(ki, masked):
        ks = pl.multiple_of(ki * t, t)
        if masked:
            causal = (lax.broadcasted_iota(jnp.int32, (t, t), 0) >= lax.broadcasted_iota(jnp.int32, (t, t), 1))
        for h in range(FOX_HEADS):
            sl = slice((h // 2) * LANES, (h // 2 + 1) * LANES)
            k = k_ref[0, pl.ds(ks, t), sl]
            v = v_ref[0, pl.ds(ks, t), sl]
            s = ((_qk(qs_ref[h], k) + pltpu.repeat(cq_ref[h], t // LANES, axis=1))
                 - cr_ref[0, h:h + 1, pl.ds(ks, t)])
            if masked:
                s = jnp.where(causal, s, NEG)
            _flash_update(h, s, None, v, m_ref, l_ref, acc_ref)

    def body(ki, c):
        step(ki, False)
        return c

    lax.fori_loop(0, qi, body, 0)
    step(qi, True)
    for p in range(4):
        lo = acc_ref[2 * p] / l_ref[2 * p]
        hi = acc_ref[2 * p + 1] / l_ref[2 * p + 1] if 2 * p + 1 < FOX_HEADS else None
        o_ref[0, :, p * LANES:(p + 1) * LANES] = _merge_halves(lo, hi).astype(BF16)


def _fox_attention(pb, cum, cumrow, t=512):
    nb, seq, _ = pb.shape
    t = min(t, seq)
    w = 4 * LANES
    return pl.pallas_call(
        functools.partial(_fox_kernel, t=t),
        grid=(nb, seq // t),
        in_specs=[
            pl.BlockSpec((1, t, w), lambda b, i: (b, i, P_FOXQ // w)),
            pl.BlockSpec((1, seq, w), lambda b, i: (b, 0, P_FOXK // w)),
            pl.BlockSpec((1, seq, w), lambda b, i: (b, 0, P_FOXV // w)),
            pl.BlockSpec((1, t, LANES), lambda b, i: (b, i, 0)),
            pl.BlockSpec((1, 8, seq), lambda b, i: (b, 0, 0)),
        ],
        out_specs=pl.BlockSpec((1, t, w), lambda b, i: (b, i, 0)),
        out_shape=jax.ShapeDtypeStruct((nb, seq, w), BF16),
        scratch_shapes=([pltpu.VMEM((FOX_HEADS, t, LANES), BF16), pltpu.VMEM((FOX_HEADS, t, LANES), F32)]
                        + _flash_scratch(FOX_HEADS, t)),
        compiler_params=_cparams(("parallel", "arbitrary")),
        name="fox_attention",
    )(pb, pb, pb, cum, cumrow)


def _diff_kernel(q_ref, k_ref, v_ref, lam_ref, g_ref, o_ref, qs_ref, m_ref, l_ref, acc_ref, *, t, lam_init):
    qi = pl.program_id(1)
    n_slots = 2 * DIFF_HEADS
    _flash_reset(m_ref, l_ref, acc_ref)
    for n in range(n_slots):
        qs_ref[n] = _pick_half(q_ref[0, :, (n // 2) * LANES:(n // 2 + 1) * LANES], n % 2)
    tpos = pltpu.repeat((qi * t + lax.broadcasted_iota(jnp.int32, (t, LANES), 0)).astype(F32), t // LANES, axis=1)

    def step(ki, masked):
        ks = pl.multiple_of(ki * t, t)
        kpos = (ks + lax.broadcasted_iota(jnp.int32, (1, t), 1)).astype(F32)
        if masked:
            causal = (lax.broadcasted_iota(jnp.int32, (t, t), 0) >= lax.broadcasted_iota(jnp.int32, (t, t), 1))
        for n in range(n_slots):
            hd = n // 2
            sl = slice(hd * LANES, (hd + 1) * LANES)
            slope = _alibi_slope(hd, DIFF_HEADS)
            k = k_ref[0, pl.ds(ks, t), sl]
            v = v_ref[0, pl.ds(ks, t), sl]
            s = (_qk(qs_ref[n], k) + tpos * (-slope)) + kpos * slope
            if masked:
                s = jnp.where(causal, s, NEG)
            _flash_update(n, s, None, v, m_ref, l_ref, acc_ref)

    def body(ki, c):
        step(ki, False)
        return c

    lax.fori_loop(0, qi, body, 0)
    step(qi, True)
    lv = lam_ref[...]
    lam = (jnp.exp(jnp.sum(lv[0:1] * lv[1:2], axis=1, keepdims=True))
           - jnp.exp(jnp.sum(lv[2:3] * lv[3:4], axis=1, keepdims=True)) + lam_init)
    for hd in range(DIFF_HEADS):
        o = acc_ref[2 * hd] / l_ref[2 * hd] - lam * (acc_ref[2 * hd + 1] / l_ref[2 * hd + 1])
        ms = jnp.mean(o * o, axis=-1, keepdims=True)
        y = o * lax.rsqrt(ms + RMS_EPS) * g_ref[...]
        o_ref[0, :, hd * LANES:(hd + 1) * LANES] = (y * (1.0 - lam_init)).astype(BF16)


def _diff_attention(pb, lam_vecs, subln_g, lam_init, t=512):
    nb, seq, _ = pb.shape
    t = min(t, seq)
    w = DIFF_HEADS * LANES
    return pl.pallas_call(
        functools.partial(_diff_kernel, t=t, lam_init=lam_init),
        grid=(nb, seq // t),
        in_specs=[
            pl.BlockSpec((1, t, w), lambda b, i: (b, i, P_DIFQ // w)),
            pl.BlockSpec((1, seq, w), lambda b, i: (b, 0, P_DIFK // w)),
            pl.BlockSpec((1, seq, w), lambda b, i: (b, 0, P_DIFV // w)),
            pl.BlockSpec((4, HEAD_DIM), lambda b, i: (0, 0)),
            pl.BlockSpec((1, LANES), lambda b, i: (0, 0)),
        ],
        out_specs=pl.BlockSpec((1, t, w), lambda b, i: (b, i, 0)),
        out_shape=jax.ShapeDtypeStruct((nb, seq, w), BF16),
        scratch_shapes=[pltpu.VMEM((2 * DIFF_HEADS, t, LANES), BF16)] + _flash_scratch(2 * DIFF_HEADS, t),
        compiler_params=_cparams(("parallel", "arbitrary")),
        name="diff_attention",
    )(pb, pb, pb, lam_vecs.astype(F32), subln_g.astype(F32).reshape(1, LANES))


def _dil_kernel(*refs, tq, seq):
    ng = len(DIL_CONFIGS)
    q_refs, k_refs, v_refs, o_ref = refs[0:ng], refs[ng:2 * ng], refs[2 * ng:3 * ng], refs[3 * ng]
    qs = pl.program_id(1) * tq
    outs = {}
    lses = {}
    for g, (window, dilation) in enumerate(DIL_CONFIGS):
        back = -(-window // tq) * tq
        width = min(back + tq, seq)
        start = _window_start(qs, back, width, seq, tq)
        dist = ((qs - start) + lax.broadcasted_iota(jnp.int32, (tq, width), 0)
                - lax.broadcasted_iota(jnp.int32, (tq, width), 1))
        valid = ((dist & (dilation - 1)) | ((dist | (window - dist)) >> 31)) == 0
        distf = dist.astype(F32)
        for jj in range(DIL_HPG):
            sl = slice((jj // 2) * LANES, (jj // 2 + 1) * LANES)
            slope = _alibi_slope(g * DIL_HPG + jj, DIL_HEADS)
            qh = _pick_half(q_refs[g][0, :, sl], jj % 2)
            k = k_refs[g][0, pl.ds(start, width), sl]
            v = v_refs[g][0, pl.ds(start, width), sl]
            s = jnp.where(valid, _qk(qh, k) - slope * distf, NEG)
            outs[(g, jj)], lses[(g, jj)] = _softmax_pv(s, v)
    for jj in range(DIL_HPG):
        mx = lses[(0, jj)]
        for g in range(1, ng):
            mx = jnp.maximum(mx, lses[(g, jj)])
        es = [jnp.exp(lses[(g, jj)] - mx) for g in range(ng)]
        tot = es[0]
        for g in range(1, ng):
            tot = tot + es[g]
        for g in range(ng):
            outs[(g, jj)] = outs[(g, jj)] * (es[g] / tot)
    for g in range(ng):
        o_ref[0, :, (2 * g) * LANES:(2 * g + 1) * LANES] = _merge_halves(outs[(g, 0)], outs[(g, 1)]).astype(BF16)
        o_ref[0, :, (2 * g + 1) * LANES:(2 * g + 2) * LANES] = _merge_halves(outs[(g, 2)], None).astype(BF16)


def _dil_attention(pb, tq=256):
    nb, seq, _ = pb.shape
    ng = len(DIL_CONFIGS)
    gw = DIL_GW
    q_specs = [pl.BlockSpec((1, tq, gw), lambda b, i, g=g: (b, i, P_DILQ // gw + g)) for g in range(ng)]
    k_specs = [pl.BlockSpec((1, seq, gw), lambda b, i, g=g: (b, 0, P_DILK // gw + g)) for g in range(ng)]
    v_specs = [pl.BlockSpec((1, seq, gw), lambda b, i, g=g: (b, 0, P_DILV // gw + g)) for g in range(ng)]
    return pl.pallas_call(
        functools.partial(_dil_kernel, tq=tq, seq=seq),
        grid=(nb, seq // tq),
        in_specs=q_specs + k_specs + v_specs,
        out_specs=pl.BlockSpec((1, tq, DIL_W), lambda b, i: (b, i, 0)),
        out_shape=jax.ShapeDtypeStruct((nb, seq, DIL_W), BF16),
        compiler_params=_cparams(("parallel", "arbitrary"), vmem_mb=56),
        name="dil_attention",
    )(*([pb] * (3 * ng)))


def _gelu_tanh(x):
    return 0.5 * x * (1.0 + jnp.tanh(math.sqrt(2.0 / math.pi) * (x + 0.044715 * (x * x * x))))


def _compress_kernel(rk_ref, rv_ref, w1_ref, w2_ref, pos_ref, kc_ref, vc_ref):
    nc = rk_ref.shape[1]
    rowi = lax.broadcasted_iota(jnp.int32, (nc, LANES), 0)
    for kv, (r_ref, o_ref) in enumerate(((rk_ref, kc_ref), (rv_ref, vc_ref))):
        r = r_ref[0]
        a = jnp.dot(r, w1_ref[kv, 0], preferred_element_type=F32)
        b = jnp.dot(r, w1_ref[kv, 1], preferred_element_type=F32)
        pt = (jnp.dot(pos_ref[kv, 0], w1_ref[kv, 0], preferred_element_type=F32)
              + jnp.dot(pos_ref[kv, 1], w1_ref[kv, 1], preferred_element_type=F32))[0:1]
        pre = a + pltpu.roll(b, nc - 1, 0) + pt
        out = jnp.dot(_gelu_tanh(pre).astype(BF16), w2_ref[kv], preferred_element_type=F32)
        o_ref[0] = jnp.where(rowi < nc - 1, out, 0.0).astype(BF16)


def _compress_weights(cmp_w1, cmp_w2, cmp_pos):
    half = NSA_CMP_LEN // 2
    w1 = cmp_w1.reshape(2, 2, half, HEAD_DIM, HEAD_DIM)
    z = jnp.zeros_like(w1)
    w1x = jnp.concatenate([jnp.concatenate([w1, z], -1), jnp.concatenate([z, w1], -1)], -2)
    w1x = w1x.reshape(2, 2, half * LANES, LANES).astype(BF16)
    z2 = jnp.zeros_like(cmp_w2)
    w2x = jnp.concatenate([jnp.concatenate([cmp_w2, z2], -1), jnp.concatenate([z2, cmp_w2], -1)], -2).astype(BF16)
    pos = cmp_pos.reshape(2, 2, half, 1, HEAD_DIM)
    posx = jnp.broadcast_to(pos, (2, 2, half, NSA_GROUPS, HEAD_DIM)).reshape(2, 2, 1, half * LANES)
    posx = jnp.broadcast_to(posx, (2, 2, 8, half * LANES)).astype(BF16)
    return w1x, w2x, posx


def _nsa_compress(rk, rv, w1x, w2x, posx):
    nb, nc, width = rk.shape
    full = lambda a: pl.BlockSpec(a.shape, lambda b: (0,) * a.ndim)
    return pl.pallas_call(
        _compress_kernel,
        grid=(nb,),
        in_specs=[pl.BlockSpec((1, nc, width), lambda b: (b, 0, 0)), pl.BlockSpec((1, nc, width), lambda b: (b, 0, 0)),
                  full(w1x), full(w2x), full(posx)],
        out_specs=[pl.BlockSpec((1, nc, LANES), lambda b: (b, 0, 0))] * 2,
        out_shape=[jax.ShapeDtypeStruct((nb, nc, LANES), BF16)] * 2,
        compiler_params=_cparams(("parallel",)),
        name="nsa_compress",
    )(rk, rv, w1x, w2x, posx)


def _cmp_to_sel_matrix(n_cmp_pad, n_cmp, n_sel):
    a = NSA_SEL_BLOCK // NSA_CMP_STRIDE
    b = NSA_CMP_LEN // NSA_CMP_STRIDE
    j = np.arange(n_sel)[:, None, None]
    idx = a * j + np.arange(a)[None, :, None] - np.arange(b)[None, None, :]
    jj = np.broadcast_to(j, idx.shape)
    ok = (idx >= 0) & (idx < n_cmp)
    m = np.zeros((n_cmp_pad, LANES), np.float32)
    np.add.at(m, (idx[ok], jj[ok]), 1.0)
    return m


def _nsa_kernel(q_ref, kc_ref, vc_ref, ks_ref, vs_ref, kw_ref, vw_ref, pf_ref, e_ref, m2_ref, o_ref,
                qs_ref, m_ref, l_ref, acc_ref, *, tq, tk, seq, n_sel, n_top):
    qi = pl.program_id(1)
    qs = qi * tq
    nc = kc_ref.shape[1]
    heads = [(g, j) for g in range(NSA_GROUPS) for j in range(NSA_HPG)]

    def slope_of(g, j):
        return _alibi_slope(g * NSA_HPG + j, NSA_HEADS)

    for n, (g, j) in enumerate(heads):
        qs_ref[n] = _pick_half(q_ref[0, :, j * LANES:(j + 1) * LANES], g)

    cidx = lax.broadcasted_iota(jnp.int32, (tq, nc), 1)
    dist_c = (qs + lax.broadcasted_iota(jnp.int32, (tq, nc), 0)) - (cidx * NSA_CMP_STRIDE + (NSA_CMP_LEN - 1))
    valid_c = (dist_c | ((nc - 2) - cidx)) >= 0
    dist_cf = dist_c.astype(F32)
    kc = kc_ref[0]
    vc = vc_ref[0]
    o_c = {}
    psum = [None] * NSA_GROUPS
    for n, (g, j) in enumerate(heads):
        s = jnp.where(valid_c, _qk(qs_ref[n], kc) - slope_of(g, j) * dist_cf, NEG)
        m = jnp.max(s, axis=1, keepdims=True)
        p = jnp.where(valid_c, jnp.exp(s - m), 0.0)
        l = jnp.sum(p, axis=1, keepdims=True)
        pn = p / jnp.where(l > 0.0, l, 1.0)
        o_c[(g, j)] = jnp.dot(pn.astype(BF16), vc, preferred_element_type=F32)
        psum[g] = pn if psum[g] is None else psum[g] + pn

    lane = lax.broadcasted_iota(jnp.int32, (tq, LANES), 1)
    lanef = lane.astype(F32)
    tpos = qs + lax.broadcasted_iota(jnp.int32, (tq, LANES), 0)
    cur = tpos >> 6
    forced = (lane == 0) | (lane == cur) | (lane == cur - 1)
    causal_b = lane * NSA_SEL_BLOCK <= tpos
    sel_b = []
    for g in range(NSA_GROUPS):
        ph = psum[g].astype(BF16)
        plo = (psum[g] - ph.astype(F32)).astype(BF16)
        imp = (jnp.dot(ph, m2_ref[...], preferred_element_type=F32)
               + jnp.dot(plo, m2_ref[...], preferred_element_type=F32))
        score = jnp.where(causal_b, jnp.where(forced, NSA_FORCE_SCORE, imp), NEG)
        score = jnp.where(lane < n_sel, score, PICKED)
        sel = jnp.zeros((tq, LANES), F32)
        for _ in range(n_top):
            mx = jnp.max(score, axis=1, keepdims=True)
            idx = jnp.min(jnp.where(score == mx, lanef, float(LANES)), axis=1, keepdims=True)
            hit = lanef == idx
            sel = jnp.where(hit, 1.0, sel)
            score = jnp.where(hit, PICKED, score)
        sel_b.append(sel.astype(BF16))

    _flash_reset(m_ref, l_ref, acc_ref)

    def slc_body(ki, c):
        ks = pl.multiple_of(ki * tk, tk)
        dist = ((qs - ks) + lax.broadcasted_iota(jnp.int32, (tq, tk), 0)
                - lax.broadcasted_iota(jnp.int32, (tq, tk), 1))
        distf = dist.astype(F32)
        k = ks_ref[0, pl.ds(ks, tk), :]
        v = vs_ref[0, pl.ds(ks, tk), :]
        for g in range(NSA_GROUPS):
            picked = jnp.dot(sel_b[g], e_ref[:, pl.ds(ks, tk)], preferred_element_type=F32)
            valid = jnp.where(dist >= 0, picked, 0.0) > 0.5
            for j in range(NSA_HPG):
                n = g * NSA_HPG + j
                s = jnp.where(valid, _qk(qs_ref[n], k) - slope_of(g, j) * distf, NEG)
                _flash_update(n, s, valid, v, m_ref, l_ref, acc_ref)
        return c

    lax.fori_loop(0, (qs + tq + tk - 1) // tk, slc_body, 0)

    back = -(-NSA_WINDOW // tq) * tq
    width = min(back + tq, seq)
    start = _window_start(qs, back, width, seq, tq)
    dist = ((qs - start) + lax.broadcasted_iota(jnp.int32, (tq, width), 0)
            - lax.broadcasted_iota(jnp.int32, (tq, width), 1))
    valid_w = (dist | ((NSA_WINDOW - 1) - dist)) >= 0
    distf = dist.astype(F32)
    kw = kw_ref[0, pl.ds(start, width), :]
    vw = vw_ref[0, pl.ds(start, width), :]
    o_w = {}
    for n, (g, j) in enumerate(heads):
        s = jnp.where(valid_w, _qk(qs_ref[n], kw) - slope_of(g, j) * distf, NEG)
        o_w[(g, j)], _ = _softmax_pv(s, vw)

    gl = jax.nn.sigmoid(pf_ref[0])
    for j in range(NSA_HPG):
        comb = []
        for g in range(NSA_GROUPS):
            n = g * NSA_HPG + j
            c0 = GATE_COL + n * 3
            o_s = acc_ref[n] / l_ref[n]
            comb.append(gl[:, c0:c0 + 1] * o_c[(g, j)] + gl[:, c0 + 1:c0 + 2] * o_s
                        + gl[:, c0 + 2:c0 + 3] * o_w[(g, j)])
        o_ref[0, :, j * LANES:(j + 1) * LANES] = _merge_halves(comb[0], comb[1]).astype(BF16)


def _nsa_attention(pb, pf, kc, vc, tq=256, tk=512):
    nb, seq, _ = pb.shape
    tk = min(tk, seq)
    nc = kc.shape[1]
    n_sel = seq // NSA_SEL_BLOCK
    n_top = min(NSA_TOP_N, n_sel)
    n_cmp = seq // NSA_CMP_STRIDE - NSA_CMP_LEN // NSA_CMP_STRIDE + 1
    expand = (np.arange(LANES)[:, None] == (np.arange(seq)[None, :] // NSA_SEL_BLOCK))
    e = jnp.asarray(expand, BF16)
    m2 = jnp.asarray(_cmp_to_sel_matrix(nc, n_cmp, n_sel), BF16)
    w = NSA_HPG * LANES
    kvb = P_NSAKV // LANES
    kv_spec = lambda i: pl.BlockSpec((1, seq, LANES), lambda b, q, i=i: (b, 0, kvb + i))
    return pl.pallas_call(
        functools.partial(_nsa_kernel, tq=tq, tk=tk, seq=seq, n_sel=n_sel, n_top=n_top),
        grid=(nb, seq // tq),
        in_specs=[
            pl.BlockSpec((1, tq, w), lambda b, i: (b, i, P_NSAQ // w)),
            pl.BlockSpec((1, nc, LANES), lambda b, i: (b, 0, 0)),
            pl.BlockSpec((1, nc, LANES), lambda b, i: (b, 0, 0)),
            kv_spec(2), kv_spec(3), kv_spec(4), kv_spec(5),
            pl.BlockSpec((1, tq, LANES), lambda b, i: (b, i, 0)),
            pl.BlockSpec((LANES, seq), lambda b, i: (0, 0)),
            pl.BlockSpec((nc, LANES), lambda b, i: (0, 0)),
        ],
        out_specs=pl.BlockSpec((1, tq, w), lambda b, i: (b, i, 0)),
        out_shape=jax.ShapeDtypeStruct((nb, seq, w), BF16),
        scratch_shapes=[pltpu.VMEM((NSA_HEADS, tq, LANES), BF16)] + _flash_scratch(NSA_HEADS, tq),
        compiler_params=_cparams(("parallel", "arbitrary")),
        name="nsa_attention",
    )(pb, kc, vc, pb, pb, pb, pb, pf, e, m2)


def _mixers(pb, pf, f_bias, cmp_w1, cmp_w2, cmp_pos, lam_vecs, subln_g, lam_init):
    nb, seq, _ = pb.shape
    cum = _forget_cumsum(pf, f_bias)
    cumrow = jnp.transpose(cum[:, :, :8], (0, 2, 1))
    o_fox = _fox_attention(pb, cum, cumrow)
    nc = seq // NSA_CMP_STRIDE
    rk = pb[:, :, P_NSAKV:P_NSAKV + LANES].reshape(nb, nc, NSA_CMP_STRIDE * LANES)
    rv = pb[:, :, P_NSAKV + LANES:P_NSAKV + 2 * LANES].reshape(nb, nc, NSA_CMP_STRIDE * LANES)
    kc, vc = _nsa_compress(rk, rv, *_compress_weights(cmp_w1, cmp_w2, cmp_pos))
    o_nsa = _nsa_attention(pb, pf, kc, vc)
    o_dil = _dil_attention(pb)
    o_dif = _diff_attention(pb, lam_vecs, subln_g, lam_init)
    return jnp.concatenate([o_dil, o_fox, o_nsa, o_dif], axis=-1)


def kernel(x, c, ada_w, ada_b, norm_mix_g, norm_ffn_g, w_in, fox_f_bias, nsa_cmp_w1, nsa_cmp_w2, nsa_cmp_pos,
           diff_lambda, diff_subln_g, w_out, ffn_w_gate, ffn_w_up, ffn_w_down, final_norm_g):
    nb, seq, d = x.shape
    depth = ada_w.shape[0]
    m = nb * seq
    o_map = _out_rowmap()
    mods = _modulation(c, ada_w, ada_b)
    w_p = _relayout_w_in(w_in, _proj_colmap())
    w_pf = w_p[:, :, P_FGATE:P_FGATE + LANES] + w_p[:, :, P_NGATE:P_NGATE + LANES]
    h = x.reshape(m, d)
    for layer in range(depth):
        sh1, sc1, g1, sh2, sc2, g2 = [mods[layer, :, i * d:(i + 1) * d].reshape(nb, 1, d) for i in range(6)]
        lam_init = 0.8 - 0.6 * math.exp(-0.3 * layer)
        w_o = _gather_axis(w_out[layer], o_map, 0).astype(BF16)
        pb, pf = _norm_proj(h, norm_mix_g[layer], sc1, sh1, w_p, layer, w_pf[layer], seq)
        o = _mixers(pb.reshape(nb, seq, N_PB), pf.reshape(nb, seq, LANES), fox_f_bias[layer],
                    nsa_cmp_w1[layer], nsa_cmp_w2[layer], nsa_cmp_pos[layer],
                    diff_lambda[layer], diff_subln_g[layer], lam_init)
        h = _matmul_gated_residual(o.reshape(m, N_O), w_o, h, g1, seq, name="out_proj_res")
        act = _norm_ffn_up(h, norm_ffn_g[layer], sc2, sh2,
                           ffn_w_gate[layer].astype(BF16), ffn_w_up[layer].astype(BF16), seq)
        h = _matmul_gated_residual(act, ffn_w_down[layer].astype(BF16), h, g2, seq, name="ffn_down_res")
    return _final_norm(h, final_norm_g).reshape(nb, seq, d)
```

```python
import functools
import math

import numpy as np
import jax
import jax.numpy as jnp
from jax import lax
from jax.experimental import pallas as pl
from jax.experimental.pallas import tpu as pltpu

F32 = jnp.float32
BF16 = jnp.bfloat16

LANES = 128
HEAD_DIM = 64
HALF = HEAD_DIM
RMS_EPS = 1e-6
NEG = -1e30
PICKED = -3e38
LOG2E = math.log2(math.e)
Q_SCALE = HEAD_DIM ** -0.5 * LOG2E

FOX_HEADS = 7
NSA_HEADS = 8
NSA_GROUPS = 2
NSA_HPG = NSA_HEADS // NSA_GROUPS
NSA_CMP_LEN = 32
NSA_CMP_STRIDE = 16
NSA_SEL_BLOCK = 64
NSA_TOP_N = 16
NSA_WINDOW = 512
NSA_FORCE_SCORE = 1e4
DIL_CONFIGS = ((128, 1), (512, 4), (2048, 16))
DIL_HPG = 3
DIL_HEADS = DIL_HPG * len(DIL_CONFIGS)
DIFF_HEADS = 4
ALIBI_MAX_EXP = 8.0

_SPLITS = (
    ("fox_q", 448), ("fox_k", 448), ("fox_v", 448), ("fox_f", 7),
    ("nsa_q", 512), ("nsa_cmp_k", 128), ("nsa_cmp_v", 128), ("nsa_slc_k", 128),
    ("nsa_slc_v", 128), ("nsa_win_k", 128), ("nsa_win_v", 128), ("nsa_gate", 24),
    ("dil_q", 576), ("dil_k", 576), ("dil_v", 576),
    ("diff_q", 512), ("diff_k", 512), ("diff_v", 512),
)
_SRC = {}
_o = 0
for _n, _w in _SPLITS:
    _SRC[_n] = _o
    _o += _w
N_IN = _o

DIL_GW = 256
DIL_W = 3 * DIL_GW
P_FOXQ, P_FOXK, P_FOXV = 0, 512, 1024
P_NSAQ = 1536
P_NSAKV = 2048
P_DILQ, P_DILK, P_DILV = 2816, 3584, 4352
P_DIFQ, P_DIFK, P_DIFV = 5120, 5632, 6144
N_PB = 6656
GATE_COL = 8
RELAYOUT_TILE = 512
P_FGATE, P_NGATE = N_PB, N_PB + RELAYOUT_TILE
N_WP = N_PB + 2 * RELAYOUT_TILE


def _proj_colmap():
    pb = -np.ones(N_WP, np.int64)
    ar = np.arange
    for name, dst in (("fox_q", P_FOXQ), ("fox_k", P_FOXK), ("fox_v", P_FOXV)):
        pb[dst:dst + 448] = _SRC[name] + ar(448)
    for j in range(NSA_HPG):
        for g in range(NSA_GROUPS):
            d = P_NSAQ + j * LANES + g * HALF
            pb[d:d + HALF] = _SRC["nsa_q"] + (g * NSA_HPG + j) * HEAD_DIM + ar(HALF)
    for i, name in enumerate(("nsa_cmp_k", "nsa_cmp_v", "nsa_slc_k", "nsa_slc_v", "nsa_win_k", "nsa_win_v")):
        pb[P_NSAKV + i * LANES:P_NSAKV + (i + 1) * LANES] = _SRC[name] + ar(LANES)
    for name, dst in (("dil_q", P_DILQ), ("dil_k", P_DILK), ("dil_v", P_DILV)):
        for g in range(len(DIL_CONFIGS)):
            for jj in range(DIL_HPG):
                d = dst + (2 * g + jj // 2) * LANES + (jj % 2) * HALF
                pb[d:d + HALF] = _SRC[name] + (g * DIL_HPG + jj) * HEAD_DIM + ar(HALF)
    for name, dst in (("diff_q", P_DIFQ), ("diff_k", P_DIFK), ("diff_v", P_DIFV)):
        pb[dst:dst + 512] = _SRC[name] + ar(512)
    pb[P_FGATE:P_FGATE + FOX_HEADS] = _SRC["fox_f"] + ar(FOX_HEADS)
    pb[P_NGATE + GATE_COL:P_NGATE + GATE_COL + 24] = _SRC["nsa_gate"] + ar(24)
    return pb


O_DIL, O_FOX, O_NSA, O_DIF = 0, 768, 1280, 1792
N_O = 2304


def _out_rowmap():
    m = -np.ones(N_O, np.int64)
    ar = np.arange
    m[O_FOX:O_FOX + 448] = ar(448)
    for j in range(NSA_HPG):
        for g in range(NSA_GROUPS):
            d = O_NSA + j * LANES + g * HALF
            m[d:d + HALF] = 448 + (g * NSA_HPG + j) * HEAD_DIM + ar(HALF)
    for g in range(len(DIL_CONFIGS)):
        for jj in range(DIL_HPG):
            d = O_DIL + (2 * g + jj // 2) * LANES + (jj % 2) * HALF
            m[d:d + HALF] = 960 + (g * DIL_HPG + jj) * HEAD_DIM + ar(HALF)
    m[O_DIF:O_DIF + 512] = 1536 + ar(512)
    return m


def _gather_axis(x, idx, axis):
    pieces = []
    i = 0
    n = len(idx)
    while i < n:
        j = i + 1
        if idx[i] < 0:
            while j < n and idx[j] < 0:
                j += 1
            shape = list(x.shape)
            shape[axis] = j - i
            pieces.append(jnp.zeros(shape, x.dtype))
        else:
            while j < n and idx[j] == idx[j - 1] + 1:
                j += 1
            pieces.append(lax.slice_in_dim(x, int(idx[i]), int(idx[j - 1]) + 1, axis=axis))
        i = j
    return jnp.concatenate(pieces, axis=axis)


def _cparams(sem, vmem_mb=48):
    return pltpu.CompilerParams(dimension_semantics=sem, vmem_limit_bytes=vmem_mb * 1024 * 1024)


def _alibi_slope(k, n_heads):
    return float(2.0 ** (-ALIBI_MAX_EXP * (k + 1) / n_heads)) * LOG2E


def _src_col_scale(n_pad):
    sc = np.ones(n_pad, np.float32)
    for name, width in (("fox_q", 448), ("nsa_q", 512), ("dil_q", 576), ("diff_q", 512)):
        sc[_SRC[name]:_SRC[name] + width] = Q_SCALE
    return sc


def _relayout_plan(colmap, n_src, tile):
    n_tiles = len(colmap) // tile
    last = (n_src - 1) // tile
    win = np.zeros(n_tiles, np.int32)
    local = -np.ones((n_tiles, 1, tile), np.int32)
    for t in range(n_tiles):
        src = colmap[t * tile:(t + 1) * tile]
        used = src[src >= 0]
        a = min(int(used.min()) // tile, max(last - 1, 0))
        assert int(used.max()) < (a + 2) * tile, "tile sources exceed the two-block window"
        win[t] = a
        local[t, 0] = np.where(src >= 0, src - a * tile, -1)
    return win, local


def _relayout_kernel(win_ref, idx_ref, sa_ref, sb_ref, wa_ref, wb_ref, o_ref, *, n_src, tile):
    a = win_ref[pl.program_id(1)]
    idx = idx_ref[0]
    row = lax.broadcasted_iota(jnp.int32, (tile, tile), 0)
    col = lax.broadcasted_iota(jnp.int32, wa_ref.shape[1:], 1)
    acc = None
    for part, (w_ref, s_ref) in enumerate(((wa_ref, sa_ref), (wb_ref, sb_ref))):
        sel = jnp.where(row + part * tile == idx, 1.0, 0.0).astype(BF16)
        w = jnp.where(col + (a + part) * tile < n_src, w_ref[0] * s_ref[0], 0.0).astype(BF16)
        d = jnp.dot(w, sel, preferred_element_type=F32)
        acc = d if acc is None else acc + d
    o_ref[0] = acc.astype(BF16)


def _relayout_w_in(w_in, colmap, tile=RELAYOUT_TILE):
    depth, d, n_src = w_in.shape
    win, local = _relayout_plan(colmap, n_src, tile)
    n_tiles = len(win)
    n_blk = -(-n_src // tile)
    scale = jnp.asarray(_src_col_scale(n_blk * tile).reshape(n_blk, 1, tile))
    grid_spec = pltpu.PrefetchScalarGridSpec(
        num_scalar_prefetch=1,
        grid=(depth, n_tiles),
        in_specs=[
            pl.BlockSpec((1, 1, tile), lambda l, t, win: (t, 0, 0)),
            pl.BlockSpec((1, 1, tile), lambda l, t, win: (win[t], 0, 0)),
            pl.BlockSpec((1, 1, tile), lambda l, t, win: (win[t] + 1, 0, 0)),
            pl.BlockSpec((1, d, tile), lambda l, t, win: (l, 0, win[t])),
            pl.BlockSpec((1, d, tile), lambda l, t, win: (l, 0, win[t] + 1)),
        ],
        out_specs=pl.BlockSpec((1, d, tile), lambda l, t, win: (l, 0, t)),
    )
    return pl.pallas_call(
        functools.partial(_relayout_kernel, n_src=n_src, tile=tile),
        grid_spec=grid_spec,
        out_shape=jax.ShapeDtypeStruct((depth, d, n_tiles * tile), BF16),
        compiler_params=_cparams(("parallel", "arbitrary")),
        name="w_in_relayout",
    )(jnp.asarray(win), jnp.asarray(local), scale, scale, w_in, w_in)


def _mod_kernel(c_ref, w_ref, b_ref, o_ref):
    nb = c_ref.shape[0]
    tn = w_ref.shape[2]
    for b in range(nb):
        cb = c_ref[b]
        ca = cb * jax.nn.sigmoid(cb)
        for j in range(tn // LANES):
            sl = slice(j * LANES, (j + 1) * LANES)
            w = w_ref[0, :, sl]
            o_ref[0, b:b + 1, sl] = jnp.sum(w * ca, axis=0, keepdims=True) + b_ref[0, :, sl]


def _modulation(c, ada_w, ada_b, tn=512):
    depth, d, n = ada_w.shape
    nb = c.shape[0]
    c_rep = jnp.broadcast_to(c[:, :, None], (nb, d, LANES))
    return pl.pallas_call(
        _mod_kernel,
        grid=(depth, n // tn),
        in_specs=[
            pl.BlockSpec((nb, d, LANES), lambda l, j: (0, 0, 0)),
            pl.BlockSpec((1, d, tn), lambda l, j: (l, 0, j)),
            pl.BlockSpec((1, 1, tn), lambda l, j: (l, 0, j)),
        ],
        out_specs=pl.BlockSpec((1, nb, tn), lambda l, j: (l, 0, j)),
        out_shape=jax.ShapeDtypeStruct((depth, nb, n), F32),
        compiler_params=_cparams(("parallel", "parallel")),
        name="adaln_mod",
    )(c_rep, ada_w, ada_b.reshape(depth, 1, n))


def _norm_mod(h_ref, g_ref, sc_ref, sh_ref):
    x = h_ref[...]
    ms = jnp.mean(x * x, axis=-1, keepdims=True)
    y = x * lax.rsqrt(ms + RMS_EPS) * g_ref[...]
    return (y * (1.0 + sc_ref[0]) + sh_ref[0]).astype(BF16)


def _proj_kernel(h_ref, g_ref, sc_ref, sh_ref, w_ref, wf_ref, pb_ref, pf_ref, u_ref):
    @pl.when(pl.program_id(1) == 0)
    def _():
        u = _norm_mod(h_ref, g_ref, sc_ref, sh_ref)
        u_ref[...] = u
        pf_ref[...] = jnp.dot(u, wf_ref[...], preferred_element_type=F32)

    pb_ref[...] = jnp.dot(u_ref[...], w_ref[0], preferred_element_type=F32).astype(BF16)


def _norm_proj(h, gain, sc, sh, w_all, layer, wf, seq, tm=1024, tn=512):
    m, d = h.shape
    per = seq // tm
    return pl.pallas_call(
        _proj_kernel,
        grid=(m // tm, N_PB // tn),
        in_specs=[
            pl.BlockSpec((tm, d), lambda i, j: (i, 0)),
            pl.BlockSpec((1, d), lambda i, j: (0, 0)),
            pl.BlockSpec((1, 1, d), lambda i, j: (i // per, 0, 0)),
            pl.BlockSpec((1, 1, d), lambda i, j: (i // per, 0, 0)),
            pl.BlockSpec((1, d, tn), lambda i, j: (layer, 0, j)),
            pl.BlockSpec((d, LANES), lambda i, j: (0, 0)),
        ],
        out_specs=[
            pl.BlockSpec((tm, tn), lambda i, j: (i, j)),
            pl.BlockSpec((tm, LANES), lambda i, j: (i, 0)),
        ],
        out_shape=[jax.ShapeDtypeStruct((m, N_PB), BF16), jax.ShapeDtypeStruct((m, LANES), F32)],
        scratch_shapes=[pltpu.VMEM((tm, d), BF16)],
        compiler_params=_cparams(("parallel", "arbitrary")),
        name="norm_in_proj",
    )(h, gain.reshape(1, d), sc, sh, w_all, wf)


def _ffn_up_kernel(h_ref, g_ref, sc_ref, sh_ref, wg_ref, wu_ref, a_ref, u_ref):
    @pl.when(pl.program_id(1) == 0)
    def _():
        u_ref[...] = _norm_mod(h_ref, g_ref, sc_ref, sh_ref)

    u = u_ref[...]
    gate = jnp.dot(u, wg_ref[0], preferred_element_type=F32)
    up = jnp.dot(u, wu_ref[0], preferred_element_type=F32)
    a_ref[...] = (gate * jax.nn.sigmoid(gate) * up).astype(BF16)


def _norm_ffn_up(h, gain, sc, sh, wg, wu, layer, seq, tm=1024, tn=512):
    m, d = h.shape
    n = wg.shape[2]
    per = seq // tm
    return pl.pallas_call(
        _ffn_up_kernel,
        grid=(m // tm, n // tn),
        in_specs=[
            pl.BlockSpec((tm, d), lambda i, j: (i, 0)),
            pl.BlockSpec((1, d), lambda i, j: (0, 0)),
            pl.BlockSpec((1, 1, d), lambda i, j: (i // per, 0, 0)),
            pl.BlockSpec((1, 1, d), lambda i, j: (i // per, 0, 0)),
            pl.BlockSpec((1, d, tn), lambda i, j: (layer, 0, j)),
            pl.BlockSpec((1, d, tn), lambda i, j: (layer, 0, j)),
        ],
        out_specs=pl.BlockSpec((tm, tn), lambda i, j: (i, j)),
        out_shape=jax.ShapeDtypeStruct((m, n), BF16),
        scratch_shapes=[pltpu.VMEM((tm, d), BF16)],
        compiler_params=_cparams(("parallel", "arbitrary")),
        name="norm_ffn_up",
    )(h, gain.reshape(1, d), sc, sh, wg, wu)


def _mm_res_kernel(*refs, widths):
    n = len(widths)
    a_refs, (w_ref, h_ref, g_ref, o_ref) = refs[:n], refs[n:]
    acc = None
    off = 0
    for a_ref, width in zip(a_refs, widths):
        d = jnp.dot(a_ref[...], w_ref[0, off:off + width, :], preferred_element_type=F32)
        acc = d if acc is None else acc + d
        off += width
    o_ref[...] = h_ref[...] + g_ref[0] * acc


def _matmul_gated_residual(a_parts, w, layer, h, gate, seq, tm=512, tn=512, name="mm_res"):
    m = h.shape[0]
    widths = tuple(a.shape[1] for a in a_parts)
    k, n = w.shape[1], w.shape[2]
    assert sum(widths) == k
    per = seq // tm
    return pl.pallas_call(
        functools.partial(_mm_res_kernel, widths=widths),
        grid=(m // tm, n // tn),
        in_specs=[pl.BlockSpec((tm, wd), lambda i, j: (i, 0)) for wd in widths] + [
            pl.BlockSpec((1, k, tn), lambda i, j: (layer, 0, j)),
            pl.BlockSpec((tm, tn), lambda i, j: (i, j)),
            pl.BlockSpec((1, 1, tn), lambda i, j: (i // per, 0, j)),
        ],
        out_specs=pl.BlockSpec((tm, tn), lambda i, j: (i, j)),
        out_shape=jax.ShapeDtypeStruct((m, n), F32),
        compiler_params=_cparams(("parallel", "arbitrary")),
        name=name,
    )(*a_parts, w, h, gate)


def _final_norm_kernel(h_ref, g_ref, o_ref):
    x = h_ref[...]
    ms = jnp.mean(x * x, axis=-1, keepdims=True)
    o_ref[...] = x * lax.rsqrt(ms + RMS_EPS) * g_ref[...]


def _final_norm(h, gain, tm=512):
    m, d = h.shape
    return pl.pallas_call(
        _final_norm_kernel,
        grid=(m // tm,),
        in_specs=[pl.BlockSpec((tm, d), lambda i: (i, 0)), pl.BlockSpec((1, d), lambda i: (0, 0))],
        out_specs=pl.BlockSpec((tm, d), lambda i: (i, 0)),
        out_shape=jax.ShapeDtypeStruct((m, d), F32),
        compiler_params=_cparams(("parallel",)),
        name="final_norm",
    )(h, gain.reshape(1, d))


def _half_mask(shape, half):
    lane = lax.broadcasted_iota(jnp.int32, shape, 1)
    return (lane < HALF) if half == 0 else (lane >= HALF)


def _pick_half(qp, half):
    return jnp.where(_half_mask(qp.shape, half), qp, jnp.zeros_like(qp))


def _lane_tile(x, n):
    return x if n == 1 else jnp.concatenate([x] * n, axis=1)


def _qk(qh, k):
    return lax.dot_general(qh, k, (((1,), (1,)), ((), ())), preferred_element_type=F32)


def _merge_halves(lo, hi):
    if hi is None:
        hi = jnp.zeros_like(lo)
    return jnp.where(_half_mask(lo.shape, 0), lo, hi)


def _flash_scratch(n_heads, tq):
    return [pltpu.VMEM((n_heads, tq, LANES), F32), pltpu.VMEM((n_heads, tq, LANES), F32),
            pltpu.VMEM((n_heads, tq, LANES), F32)]


def _flash_reset(m_ref, l_ref, acc_ref):
    m_ref[...] = jnp.full(m_ref.shape, NEG, F32)
    l_ref[...] = jnp.zeros(l_ref.shape, F32)
    acc_ref[...] = jnp.zeros(acc_ref.shape, F32)


def _flash_update(h, s, valid, v, m_ref, l_ref, acc_ref):
    m_old = m_ref[h]
    m_new = jnp.maximum(m_old, jnp.max(s, axis=1, keepdims=True))
    alpha = jnp.exp2(m_old - m_new)
    p = jnp.exp2(s - _lane_tile(m_new, s.shape[1] // LANES))
    if valid is not None:
        p = jnp.where(valid, p, 0.0)
    l_ref[h] = alpha * l_ref[h] + jnp.sum(p, axis=1, keepdims=True)
    acc_ref[h] = alpha * acc_ref[h] + jnp.dot(p.astype(BF16), v, preferred_element_type=F32)
    m_ref[h] = m_new


def _softmax_pv(s, v):
    m = jnp.max(s, axis=1, keepdims=True)
    p = jnp.exp2(s - m)
    l = jnp.sum(p, axis=1, keepdims=True)
    return jnp.dot(p.astype(BF16), v, preferred_element_type=F32) / l, m + jnp.log2(l)


def _window_start(qs, back, width, seq, align):
    return pl.multiple_of(jnp.clip(qs - back, 0, seq - width), align)


def _cumsum_kernel(pf_ref, fb_ref, o_ref, *, chunk):
    seq = pf_ref.shape[1]
    r = lax.broadcasted_iota(jnp.int32, (chunk, chunk), 0)
    c = lax.broadcasted_iota(jnp.int32, (chunk, chunk), 1)
    tri = jnp.where(r >= c, 1.0, 0.0).astype(BF16)

    def body(i, carry):
        st = pl.multiple_of(i * chunk, chunk)
        x = pf_ref[0, pl.ds(st, chunk), :] + fb_ref[...]
        ls = jnp.minimum(x, 0.0) - jnp.log1p(jnp.exp(-jnp.abs(x)))
        hi = ls.astype(BF16)
        r1 = ls - hi.astype(F32)
        mid = r1.astype(BF16)
        lo = (r1 - mid.astype(F32)).astype(BF16)
        cs = (jnp.dot(tri, hi, preferred_element_type=F32)
              + jnp.dot(tri, mid, preferred_element_type=F32)
              + jnp.dot(tri, lo, preferred_element_type=F32))
        out = cs + carry
        o_ref[0, pl.ds(st, chunk), :] = out * LOG2E
        return out[chunk - 1:chunk, :]

    lax.fori_loop(0, seq // chunk, body, jnp.zeros((1, LANES), F32))


def _forget_cumsum(pf, f_bias, chunk=256):
    nb, seq, _ = pf.shape
    fb = jnp.zeros((1, LANES), F32).at[0, :FOX_HEADS].set(f_bias.astype(F32))
    return pl.pallas_call(
        functools.partial(_cumsum_kernel, chunk=chunk),
        grid=(nb,),
        in_specs=[pl.BlockSpec((1, seq, LANES), lambda b: (b, 0, 0)), pl.BlockSpec((1, LANES), lambda b: (0, 0))],
        out_specs=pl.BlockSpec((1, seq, LANES), lambda b: (b, 0, 0)),
        out_shape=jax.ShapeDtypeStruct((nb, seq, LANES), F32),
        compiler_params=_cparams(("parallel",)),
        name="forget_cumsum",
    )(pf, fb)


def _fox_kernel(q_ref, k_ref, v_ref, cc_ref, cr_ref, o_ref, qs_ref, cq_ref, m_ref, l_ref, acc_ref, *, t):
    qi = pl.program_id(1)
    _flash_reset(m_ref, l_ref, acc_ref)
    for h in range(FOX_HEADS):
        qs_ref[h] = _pick_half(q_ref[0, :, (h // 2) * LANES:(h // 2 + 1) * LANES], h % 2)
        cq_ref[h] = jnp.broadcast_to(cc_ref[0, :, h:h + 1], (t, LANES))

    def step(ki, masked):
        ks = pl.multiple_of(ki * t, t)
        if masked:
            causal = (lax.broadcasted_iota(jnp.int32, (t, t), 0) >= lax.broadcasted_iota(jnp.int32, (t, t), 1))
        for h in range(FOX_HEADS):
            sl = slice((h // 2) * LANES, (h // 2 + 1) * LANES)
            k = k_ref[0, pl.ds(ks, t), sl]
            v = v_ref[0, pl.ds(ks, t), sl]
            s = ((_qk(qs_ref[h], k) + _lane_tile(cq_ref[h], t // LANES))
                 - cr_ref[0, h:h + 1, pl.ds(ks, t)])
            if masked:
                s = jnp.where(causal, s, NEG)
            _flash_update(h, s, None, v, m_ref, l_ref, acc_ref)

    def body(ki, c):
        step(ki, False)
        return c

    lax.fori_loop(0, qi, body, 0)
    step(qi, True)
    for p in range(4):
        lo = acc_ref[2 * p] / l_ref[2 * p]
        hi = acc_ref[2 * p + 1] / l_ref[2 * p + 1] if 2 * p + 1 < FOX_HEADS else None
        o_ref[0, :, p * LANES:(p + 1) * LANES] = _merge_halves(lo, hi).astype(BF16)


def _fox_attention(pb, cum, cumrow, t=512):
    nb, seq, _ = pb.shape
    t = min(t, seq)
    w = 4 * LANES
    return pl.pallas_call(
        functools.partial(_fox_kernel, t=t),
        grid=(nb, seq // t),
        in_specs=[
            pl.BlockSpec((1, t, w), lambda b, i: (b, i, P_FOXQ // w)),
            pl.BlockSpec((1, seq, w), lambda b, i: (b, 0, P_FOXK // w)),
            pl.BlockSpec((1, seq, w), lambda b, i: (b, 0, P_FOXV // w)),
            pl.BlockSpec((1, t, LANES), lambda b, i: (b, i, 0)),
            pl.BlockSpec((1, 8, seq), lambda b, i: (b, 0, 0)),
        ],
        out_specs=pl.BlockSpec((1, t, w), lambda b, i: (b, i, 0)),
        out_shape=jax.ShapeDtypeStruct((nb, seq, w), BF16),
        scratch_shapes=([pltpu.VMEM((FOX_HEADS, t, LANES), BF16), pltpu.VMEM((FOX_HEADS, t, LANES), F32)]
                        + _flash_scratch(FOX_HEADS, t)),
        compiler_params=_cparams(("parallel", "arbitrary")),
        name="fox_attention",
    )(pb, pb, pb, cum, cumrow)


def _diff_kernel(q_ref, k_ref, v_ref, lam_ref, g_ref, o_ref, qs_ref, m_ref, l_ref, acc_ref, *, t, lam_init):
    qi = pl.program_id(1)
    n_slots = 2 * DIFF_HEADS
    _flash_reset(m_ref, l_ref, acc_ref)
    for n in range(n_slots):
        qs_ref[n] = _pick_half(q_ref[0, :, (n // 2) * LANES:(n // 2 + 1) * LANES], n % 2)
    tpos = _lane_tile((qi * t + lax.broadcasted_iota(jnp.int32, (t, LANES), 0)).astype(F32), t // LANES)

    def step(ki, masked):
        ks = pl.multiple_of(ki * t, t)
        kpos = (ks + lax.broadcasted_iota(jnp.int32, (1, t), 1)).astype(F32)
        if masked:
            causal = (lax.broadcasted_iota(jnp.int32, (t, t), 0) >= lax.broadcasted_iota(jnp.int32, (t, t), 1))
        for n in range(n_slots):
            hd = n // 2
            sl = slice(hd * LANES, (hd + 1) * LANES)
            slope = _alibi_slope(hd, DIFF_HEADS)
            k = k_ref[0, pl.ds(ks, t), sl]
            v = v_ref[0, pl.ds(ks, t), sl]
            s = (_qk(qs_ref[n], k) + tpos * (-slope)) + kpos * slope
            if masked:
                s = jnp.where(causal, s, NEG)
            _flash_update(n, s, None, v, m_ref, l_ref, acc_ref)

    def body(ki, c):
        step(ki, False)
        return c

    lax.fori_loop(0, qi, body, 0)
    step(qi, True)
    lv = lam_ref[...]
    lam = (jnp.exp(jnp.sum(lv[0:1] * lv[1:2], axis=1, keepdims=True))
           - jnp.exp(jnp.sum(lv[2:3] * lv[3:4], axis=1, keepdims=True)) + lam_init)
    for hd in range(DIFF_HEADS):
        o = acc_ref[2 * hd] / l_ref[2 * hd] - lam * (acc_ref[2 * hd + 1] / l_ref[2 * hd + 1])
        ms = jnp.mean(o * o, axis=-1, keepdims=True)
        y = o * lax.rsqrt(ms + RMS_EPS) * g_ref[...]
        o_ref[0, :, hd * LANES:(hd + 1) * LANES] = (y * (1.0 - lam_init)).astype(BF16)


def _diff_attention(pb, lam_vecs, subln_g, lam_init, t=512):
    nb, seq, _ = pb.shape
    t = min(t, seq)
    w = DIFF_HEADS * LANES
    return pl.pallas_call(
        functools.partial(_diff_kernel, t=t, lam_init=lam_init),
        grid=(nb, seq // t),
        in_specs=[
            pl.BlockSpec((1, t, w), lambda b, i: (b, i, P_DIFQ // w)),
            pl.BlockSpec((1, seq, w), lambda b, i: (b, 0, P_DIFK // w)),
            pl.BlockSpec((1, seq, w), lambda b, i: (b, 0, P_DIFV // w)),
            pl.BlockSpec((4, HEAD_DIM), lambda b, i: (0, 0)),
            pl.BlockSpec((1, LANES), lambda b, i: (0, 0)),
        ],
        out_specs=pl.BlockSpec((1, t, w), lambda b, i: (b, i, 0)),
        out_shape=jax.ShapeDtypeStruct((nb, seq, w), BF16),
        scratch_shapes=[pltpu.VMEM((2 * DIFF_HEADS, t, LANES), BF16)] + _flash_scratch(2 * DIFF_HEADS, t),
        compiler_params=_cparams(("parallel", "arbitrary")),
        name="diff_attention",
    )(pb, pb, pb, lam_vecs.astype(F32), subln_g.astype(F32).reshape(1, LANES))


def _dil_kernel(*refs, tq, seq):
    ng = len(DIL_CONFIGS)
    q_refs, k_refs, v_refs, o_ref = refs[0:ng], refs[ng:2 * ng], refs[2 * ng:3 * ng], refs[3 * ng]
    qs = pl.program_id(1) * tq
    outs = {}
    lses = {}
    for g, (window, dilation) in enumerate(DIL_CONFIGS):
        back = -(-window // tq) * tq
        width = min(back + tq, seq)
        start = _window_start(qs, back, width, seq, tq)
        dist = ((qs - start) + lax.broadcasted_iota(jnp.int32, (tq, width), 0)
                - lax.broadcasted_iota(jnp.int32, (tq, width), 1))
        valid = ((dist & (dilation - 1)) | ((dist | (window - dist)) >> 31)) == 0
        distf = dist.astype(F32)
        for jj in range(DIL_HPG):
            sl = slice((jj // 2) * LANES, (jj // 2 + 1) * LANES)
            slope = _alibi_slope(g * DIL_HPG + jj, DIL_HEADS)
            qh = _pick_half(q_refs[g][0, :, sl], jj % 2)
            k = k_refs[g][0, pl.ds(start, width), sl]
            v = v_refs[g][0, pl.ds(start, width), sl]
            s = jnp.where(valid, _qk(qh, k) - slope * distf, NEG)
            outs[(g, jj)], lses[(g, jj)] = _softmax_pv(s, v)
    for jj in range(DIL_HPG):
        mx = lses[(0, jj)]
        for g in range(1, ng):
            mx = jnp.maximum(mx, lses[(g, jj)])
        es = [jnp.exp2(lses[(g, jj)] - mx) for g in range(ng)]
        tot = es[0]
        for g in range(1, ng):
            tot = tot + es[g]
        for g in range(ng):
            outs[(g, jj)] = outs[(g, jj)] * (es[g] / tot)
    for g in range(ng):
        o_ref[0, :, (2 * g) * LANES:(2 * g + 1) * LANES] = _merge_halves(outs[(g, 0)], outs[(g, 1)]).astype(BF16)
        o_ref[0, :, (2 * g + 1) * LANES:(2 * g + 2) * LANES] = _merge_halves(outs[(g, 2)], None).astype(BF16)


def _dil_attention(pb, tq=256):
    nb, seq, _ = pb.shape
    ng = len(DIL_CONFIGS)
    gw = DIL_GW
    q_specs = [pl.BlockSpec((1, tq, gw), lambda b, i, g=g: (b, i, P_DILQ // gw + g)) for g in range(ng)]
    k_specs = [pl.BlockSpec((1, seq, gw), lambda b, i, g=g: (b, 0, P_DILK // gw + g)) for g in range(ng)]
    v_specs = [pl.BlockSpec((1, seq, gw), lambda b, i, g=g: (b, 0, P_DILV // gw + g)) for g in range(ng)]
    return pl.pallas_call(
        functools.partial(_dil_kernel, tq=tq, seq=seq),
        grid=(nb, seq // tq),
        in_specs=q_specs + k_specs + v_specs,
        out_specs=pl.BlockSpec((1, tq, DIL_W), lambda b, i: (b, i, 0)),
        out_shape=jax.ShapeDtypeStruct((nb, seq, DIL_W), BF16),
        compiler_params=_cparams(("parallel", "arbitrary"), vmem_mb=56),
        name="dil_attention",
    )(*([pb] * (3 * ng)))


def _gelu_tanh(x):
    return 0.5 * x * (1.0 + jnp.tanh(math.sqrt(2.0 / math.pi) * (x + 0.044715 * (x * x * x))))


def _compress_kernel(rk_ref, rv_ref, w1_ref, w2_ref, pos_ref, kc_ref, vc_ref):
    nc = rk_ref.shape[1]
    rowi = lax.broadcasted_iota(jnp.int32, (nc, LANES), 0)
    for kv, (r_ref, o_ref) in enumerate(((rk_ref, kc_ref), (rv_ref, vc_ref))):
        r = r_ref[0]
        a = jnp.dot(r, w1_ref[kv, 0], preferred_element_type=F32)
        b = jnp.dot(r, w1_ref[kv, 1], preferred_element_type=F32)
        pt = (jnp.dot(pos_ref[kv, 0], w1_ref[kv, 0], preferred_element_type=F32)
              + jnp.dot(pos_ref[kv, 1], w1_ref[kv, 1], preferred_element_type=F32))[0:1]
        pre = a + pltpu.roll(b, nc - 1, 0) + pt
        out = jnp.dot(_gelu_tanh(pre).astype(BF16), w2_ref[kv], preferred_element_type=F32)
        o_ref[0] = jnp.where(rowi < nc - 1, out, 0.0).astype(BF16)


def _compress_weights(cmp_w1, cmp_w2, cmp_pos):
    half = NSA_CMP_LEN // 2
    w1 = cmp_w1.reshape(2, 2, half, HEAD_DIM, HEAD_DIM)
    z = jnp.zeros_like(w1)
    w1x = jnp.concatenate([jnp.concatenate([w1, z], -1), jnp.concatenate([z, w1], -1)], -2)
    w1x = w1x.reshape(2, 2, half * LANES, LANES).astype(BF16)
    z2 = jnp.zeros_like(cmp_w2)
    w2x = jnp.concatenate([jnp.concatenate([cmp_w2, z2], -1), jnp.concatenate([z2, cmp_w2], -1)], -2).astype(BF16)
    pos = cmp_pos.reshape(2, 2, half, 1, HEAD_DIM)
    posx = jnp.broadcast_to(pos, (2, 2, half, NSA_GROUPS, HEAD_DIM)).reshape(2, 2, 1, half * LANES)
    posx = jnp.broadcast_to(posx, (2, 2, 8, half * LANES)).astype(BF16)
    return w1x, w2x, posx


def _nsa_compress(rk, rv, w1x, w2x, posx):
    nb, nc, width = rk.shape
    full = lambda a: pl.BlockSpec(a.shape, lambda b: (0,) * a.ndim)
    return pl.pallas_call(
        _compress_kernel,
        grid=(nb,),
        in_specs=[pl.BlockSpec((1, nc, width), lambda b: (b, 0, 0)), pl.BlockSpec((1, nc, width), lambda b: (b, 0, 0)),
                  full(w1x), full(w2x), full(posx)],
        out_specs=[pl.BlockSpec((1, nc, LANES), lambda b: (b, 0, 0))] * 2,
        out_shape=[jax.ShapeDtypeStruct((nb, nc, LANES), BF16)] * 2,
        compiler_params=_cparams(("parallel",)),
        name="nsa_compress",
    )(rk, rv, w1x, w2x, posx)


def _cmp_to_sel_matrix(n_cmp_pad, n_cmp, n_sel):
    a = NSA_SEL_BLOCK // NSA_CMP_STRIDE
    b = NSA_CMP_LEN // NSA_CMP_STRIDE
    j = np.arange(n_sel)[:, None, None]
    idx = a * j + np.arange(a)[None, :, None] - np.arange(b)[None, None, :]
    jj = np.broadcast_to(j, idx.shape)
    ok = (idx >= 0) & (idx < n_cmp)
    m = np.zeros((n_cmp_pad, LANES), np.float32)
    np.add.at(m, (idx[ok], jj[ok]), 1.0)
    return m


def _nsa_kernel(q_ref, kc_ref, vc_ref, ks_ref, vs_ref, kw_ref, vw_ref, pf_ref, e_ref, m2_ref, t2_ref, o_ref,
                qs_ref, flag_ref, m_ref, l_ref, acc_ref, *, tq, tk, seq, n_sel, n_top):
    qi = pl.program_id(1)
    qs = qi * tq
    nc = kc_ref.shape[1]
    heads = [(g, j) for g in range(NSA_GROUPS) for j in range(NSA_HPG)]

    def slope_of(g, j):
        return _alibi_slope(g * NSA_HPG + j, NSA_HEADS)

    for n, (g, j) in enumerate(heads):
        qs_ref[n] = _pick_half(q_ref[0, :, j * LANES:(j + 1) * LANES], g)

    cidx = lax.broadcasted_iota(jnp.int32, (tq, nc), 1)
    dist_c = (qs + lax.broadcasted_iota(jnp.int32, (tq, nc), 0)) - (cidx * NSA_CMP_STRIDE + (NSA_CMP_LEN - 1))
    valid_c = (dist_c | ((nc - 2) - cidx)) >= 0
    dist_cf = dist_c.astype(F32)
    kc = kc_ref[0]
    vc = vc_ref[0]
    o_c = {}
    psum = [None] * NSA_GROUPS
    for n, (g, j) in enumerate(heads):
        s = jnp.where(valid_c, _qk(qs_ref[n], kc) - slope_of(g, j) * dist_cf, NEG)
        m = jnp.max(s, axis=1, keepdims=True)
        p = jnp.where(valid_c, jnp.exp2(s - m), 0.0)
        l = jnp.sum(p, axis=1, keepdims=True)
        pn = p / jnp.where(l > 0.0, l, 1.0)
        o_c[(g, j)] = jnp.dot(pn.astype(BF16), vc, preferred_element_type=F32)
        psum[g] = pn if psum[g] is None else psum[g] + pn

    lane = lax.broadcasted_iota(jnp.int32, (tq, LANES), 1)
    lanef = lane.astype(F32)
    tpos = qs + lax.broadcasted_iota(jnp.int32, (tq, LANES), 0)
    cur = tpos >> 6
    forced = (lane == 0) | (lane == cur) | (lane == cur - 1)
    causal_b = lane * NSA_SEL_BLOCK <= tpos
    sel_b = []
    for g in range(NSA_GROUPS):
        ph = psum[g].astype(BF16)
        plo = (psum[g] - ph.astype(F32)).astype(BF16)
        imp = (jnp.dot(ph, m2_ref[...], preferred_element_type=F32)
               + jnp.dot(plo, m2_ref[...], preferred_element_type=F32))
        score = jnp.where(causal_b, jnp.where(forced, NSA_FORCE_SCORE, imp), NEG)
        score = jnp.where(lane < n_sel, score, PICKED)
        sel = jnp.zeros((tq, LANES), F32)
        for _ in range(n_top):
            mx = jnp.max(score, axis=1, keepdims=True)
            idx = jnp.min(jnp.where(score == mx, lanef, float(LANES)), axis=1, keepdims=True)
            hit = lanef == idx
            sel = jnp.where(hit, 1.0, sel)
            score = jnp.where(hit, PICKED, score)
        sel_b.append(sel.astype(BF16))

    ones = jnp.ones((tq, LANES), BF16)
    for g in range(NSA_GROUPS):
        per_block = lax.dot_general(sel_b[g], ones, (((0,), (0,)), ((), ())), preferred_element_type=F32)
        flag_ref[g] = jnp.dot(t2_ref[...], per_block.astype(BF16), preferred_element_type=F32)
    _flash_reset(m_ref, l_ref, acc_ref)

    def slc_body(ki, c):
        ks = pl.multiple_of(ki * tk, tk)
        for g in range(NSA_GROUPS):
            @pl.when(flag_ref[g, pl.ds(ki, 1), :][0, 0] > 0.5)
            def _(g=g):
                dist = ((qs - ks) + lax.broadcasted_iota(jnp.int32, (tq, tk), 0)
                        - lax.broadcasted_iota(jnp.int32, (tq, tk), 1))
                distf = dist.astype(F32)
                k = ks_ref[0, pl.ds(ks, tk), :]
                v = vs_ref[0, pl.ds(ks, tk), :]
                picked = jnp.dot(sel_b[g], e_ref[:, pl.ds(ks, tk)], preferred_element_type=F32)
                valid = jnp.where(dist >= 0, picked, 0.0) > 0.5
                for j in range(NSA_HPG):
                    n = g * NSA_HPG + j
                    s = jnp.where(valid, _qk(qs_ref[n], k) - slope_of(g, j) * distf, NEG)
                    _flash_update(n, s, None, v, m_ref, l_ref, acc_ref)
        return c

    lax.fori_loop(0, (qs + tq + tk - 1) // tk, slc_body, 0)

    back = -(-NSA_WINDOW // tq) * tq
    width = min(back + tq, seq)
    start = _window_start(qs, back, width, seq, tq)
    dist = ((qs - start) + lax.broadcasted_iota(jnp.int32, (tq, width), 0)
            - lax.broadcasted_iota(jnp.int32, (tq, width), 1))
    valid_w = (dist | ((NSA_WINDOW - 1) - dist)) >= 0
    distf = dist.astype(F32)
    kw = kw_ref[0, pl.ds(start, width), :]
    vw = vw_ref[0, pl.ds(start, width), :]
    o_w = {}
    for n, (g, j) in enumerate(heads):
        s = jnp.where(valid_w, _qk(qs_ref[n], kw) - slope_of(g, j) * distf, NEG)
        o_w[(g, j)], _ = _softmax_pv(s, vw)

    gl = jax.nn.sigmoid(pf_ref[0])
    for j in range(NSA_HPG):
        comb = []
        for g in range(NSA_GROUPS):
            n = g * NSA_HPG + j
            c0 = GATE_COL + n * 3
            o_s = acc_ref[n] / l_ref[n]
            comb.append(gl[:, c0:c0 + 1] * o_c[(g, j)] + gl[:, c0 + 1:c0 + 2] * o_s
                        + gl[:, c0 + 2:c0 + 3] * o_w[(g, j)])
        o_ref[0, :, j * LANES:(j + 1) * LANES] = _merge_halves(comb[0], comb[1]).astype(BF16)


def _nsa_attention(pb, pf, kc, vc, tq=256, tk=512):
    nb, seq, _ = pb.shape
    tk = min(tk, seq)
    nc = kc.shape[1]
    n_sel = seq // NSA_SEL_BLOCK
    n_top = min(NSA_TOP_N, n_sel)
    n_cmp = seq // NSA_CMP_STRIDE - NSA_CMP_LEN // NSA_CMP_STRIDE + 1
    expand = (np.arange(LANES)[:, None] == (np.arange(seq)[None, :] // NSA_SEL_BLOCK))
    e = jnp.asarray(expand, BF16)
    m2 = jnp.asarray(_cmp_to_sel_matrix(nc, n_cmp, n_sel), BF16)
    n_tiles = seq // tk
    tile_rows = -(-n_tiles // 8) * 8
    blocks_per_tile = tk // NSA_SEL_BLOCK
    t2 = jnp.asarray(np.arange(tile_rows)[:, None] == (np.arange(LANES)[None, :] // blocks_per_tile), BF16)
    w = NSA_HPG * LANES
    kvb = P_NSAKV // LANES
    kv_spec = lambda i: pl.BlockSpec((1, seq, LANES), lambda b, q, i=i: (b, 0, kvb + i))
    return pl.pallas_call(
        functools.partial(_nsa_kernel, tq=tq, tk=tk, seq=seq, n_sel=n_sel, n_top=n_top),
        grid=(nb, seq // tq),
        in_specs=[
            pl.BlockSpec((1, tq, w), lambda b, i: (b, i, P_NSAQ // w)),
            pl.BlockSpec((1, nc, LANES), lambda b, i: (b, 0, 0)),
            pl.BlockSpec((1, nc, LANES), lambda b, i: (b, 0, 0)),
            kv_spec(2), kv_spec(3), kv_spec(4), kv_spec(5),
            pl.BlockSpec((1, tq, LANES), lambda b, i: (b, i, 0)),
            pl.BlockSpec((LANES, seq), lambda b, i: (0, 0)),
            pl.BlockSpec((nc, LANES), lambda b, i: (0, 0)),
            pl.BlockSpec((tile_rows, LANES), lambda b, i: (0, 0)),
        ],
        out_specs=pl.BlockSpec((1, tq, w), lambda b, i: (b, i, 0)),
        out_shape=jax.ShapeDtypeStruct((nb, seq, w), BF16),
        scratch_shapes=([pltpu.VMEM((NSA_HEADS, tq, LANES), BF16), pltpu.VMEM((NSA_GROUPS, tile_rows, LANES), F32)]
                        + _flash_scratch(NSA_HEADS, tq)),
        compiler_params=_cparams(("parallel", "arbitrary")),
        name="nsa_attention",
    )(pb, kc, vc, pb, pb, pb, pb, pf, e, m2, t2)


def _mixers(pb, pf, f_bias, cmp_w1, cmp_w2, cmp_pos, lam_vecs, subln_g, lam_init):
    nb, seq, _ = pb.shape
    cum = _forget_cumsum(pf, f_bias)
    cumrow = jnp.transpose(cum[:, :, :8], (0, 2, 1))
    o_fox = _fox_attention(pb, cum, cumrow)
    nc = seq // NSA_CMP_STRIDE
    rk = pb[:, :, P_NSAKV:P_NSAKV + LANES].reshape(nb, nc, NSA_CMP_STRIDE * LANES)
    rv = pb[:, :, P_NSAKV + LANES:P_NSAKV + 2 * LANES].reshape(nb, nc, NSA_CMP_STRIDE * LANES)
    kc, vc = _nsa_compress(rk, rv, *_compress_weights(cmp_w1, cmp_w2, cmp_pos))
    o_nsa = _nsa_attention(pb, pf, kc, vc)
    o_dil = _dil_attention(pb)
    o_dif = _diff_attention(pb, lam_vecs, subln_g, lam_init)
    return [o_dil, o_fox, o_nsa, o_dif]


def kernel(x, c, ada_w, ada_b, norm_mix_g, norm_ffn_g, w_in, fox_f_bias, nsa_cmp_w1, nsa_cmp_w2, nsa_cmp_pos,
           diff_lambda, diff_subln_g, w_out, ffn_w_gate, ffn_w_up, ffn_w_down, final_norm_g):
    nb, seq, d = x.shape
    depth = ada_w.shape[0]
    m = nb * seq
    mods = _modulation(c, ada_w, ada_b)
    w_p = _relayout_w_in(w_in, _proj_colmap())
    w_pf = w_p[:, :, P_FGATE:P_FGATE + LANES] + w_p[:, :, P_NGATE:P_NGATE + LANES]
    w_o = _gather_axis(w_out, _out_rowmap(), 1).astype(BF16)
    w_g, w_u, w_d = ffn_w_gate.astype(BF16), ffn_w_up.astype(BF16), ffn_w_down.astype(BF16)
    h = x.reshape(m, d)
    for layer in range(depth):
        sh1, sc1, g1, sh2, sc2, g2 = [mods[layer, :, i * d:(i + 1) * d].reshape(nb, 1, d) for i in range(6)]
        lam_init = 0.8 - 0.6 * math.exp(-0.3 * layer)
        pb, pf = _norm_proj(h, norm_mix_g[layer], sc1, sh1, w_p, layer, w_pf[layer], seq)
        parts = _mixers(pb.reshape(nb, seq, N_PB), pf.reshape(nb, seq, LANES), fox_f_bias[layer],
                        nsa_cmp_w1[layer], nsa_cmp_w2[layer], nsa_cmp_pos[layer],
                        diff_lambda[layer], diff_subln_g[layer], lam_init)
        h = _matmul_gated_residual([p.reshape(m, p.shape[-1]) for p in parts], w_o, layer, h, g1, seq,
                                   name="out_proj_res")
        act = _norm_ffn_up(h, norm_ffn_g[layer], sc2, sh2, w_g, w_u, layer, seq)
        h = _matmul_gated_residual([act], w_d, layer, h, g2, seq, name="ffn_down_res")
    return _final_norm(h, final_norm_g).reshape(nb, seq, d)
```

```python
import functools
import math

import numpy as np
import jax
import jax.numpy as jnp
from jax import lax
from jax.experimental import pallas as pl
from jax.experimental.pallas import tpu as pltpu

F32 = jnp.float32
BF16 = jnp.bfloat16

LANES = 128
HEAD_DIM = 64
HALF = HEAD_DIM
RMS_EPS = 1e-6
NEG = -1e30
PICKED = -3e38
LOG2E = math.log2(math.e)
Q_SCALE = HEAD_DIM ** -0.5 * LOG2E

FOX_HEADS = 7
NSA_HEADS = 8
NSA_GROUPS = 2
NSA_HPG = NSA_HEADS // NSA_GROUPS
NSA_CMP_LEN = 32
NSA_CMP_STRIDE = 16
NSA_SEL_BLOCK = 64
NSA_TOP_N = 16
NSA_WINDOW = 512
NSA_FORCE_SCORE = 1e4
DIL_CONFIGS = ((128, 1), (512, 4), (2048, 16))
DIL_HPG = 3
DIL_HEADS = DIL_HPG * len(DIL_CONFIGS)
DIFF_HEADS = 4
ALIBI_MAX_EXP = 8.0

_SPLITS = (
    ("fox_q", 448), ("fox_k", 448), ("fox_v", 448), ("fox_f", 7),
    ("nsa_q", 512), ("nsa_cmp_k", 128), ("nsa_cmp_v", 128), ("nsa_slc_k", 128),
    ("nsa_slc_v", 128), ("nsa_win_k", 128), ("nsa_win_v", 128), ("nsa_gate", 24),
    ("dil_q", 576), ("dil_k", 576), ("dil_v", 576),
    ("diff_q", 512), ("diff_k", 512), ("diff_v", 512),
)
_SRC = {}
_o = 0
for _n, _w in _SPLITS:
    _SRC[_n] = _o
    _o += _w
N_IN = _o

DIL_GW = 256
DIL_W = 3 * DIL_GW
P_FOXQ, P_FOXK, P_FOXV = 0, 512, 1024
P_NSAQ = 1536
P_NSAKV = 2048
P_DILQ, P_DILK, P_DILV = 2816, 3584, 4352
P_DIFQ, P_DIFK, P_DIFV = 5120, 5632, 6144
N_PB = 6656
GATE_COL = 8
RELAYOUT_TILE = 512
P_FGATE, P_NGATE = N_PB, N_PB + RELAYOUT_TILE
N_WP = N_PB + 2 * RELAYOUT_TILE


def _proj_colmap():
    pb = -np.ones(N_WP, np.int64)
    ar = np.arange
    for name, dst in (("fox_q", P_FOXQ), ("fox_k", P_FOXK), ("fox_v", P_FOXV)):
        pb[dst:dst + 448] = _SRC[name] + ar(448)
    for j in range(NSA_HPG):
        for g in range(NSA_GROUPS):
            d = P_NSAQ + j * LANES + g * HALF
            pb[d:d + HALF] = _SRC["nsa_q"] + (g * NSA_HPG + j) * HEAD_DIM + ar(HALF)
    for i, name in enumerate(("nsa_cmp_k", "nsa_cmp_v", "nsa_slc_k", "nsa_slc_v", "nsa_win_k", "nsa_win_v")):
        pb[P_NSAKV + i * LANES:P_NSAKV + (i + 1) * LANES] = _SRC[name] + ar(LANES)
    for name, dst in (("dil_q", P_DILQ), ("dil_k", P_DILK), ("dil_v", P_DILV)):
        for g in range(len(DIL_CONFIGS)):
            for jj in range(DIL_HPG):
                d = dst + (2 * g + jj // 2) * LANES + (jj % 2) * HALF
                pb[d:d + HALF] = _SRC[name] + (g * DIL_HPG + jj) * HEAD_DIM + ar(HALF)
    for name, dst in (("diff_q", P_DIFQ), ("diff_k", P_DIFK), ("diff_v", P_DIFV)):
        pb[dst:dst + 512] = _SRC[name] + ar(512)
    pb[P_FGATE:P_FGATE + FOX_HEADS] = _SRC["fox_f"] + ar(FOX_HEADS)
    pb[P_NGATE + GATE_COL:P_NGATE + GATE_COL + 24] = _SRC["nsa_gate"] + ar(24)
    return pb


O_DIL, O_FOX, O_NSA, O_DIF = 0, 768, 1280, 1792
N_O = 2304


def _out_rowmap():
    m = -np.ones(N_O, np.int64)
    ar = np.arange
    m[O_FOX:O_FOX + 448] = ar(448)
    for j in range(NSA_HPG):
        for g in range(NSA_GROUPS):
            d = O_NSA + j * LANES + g * HALF
            m[d:d + HALF] = 448 + (g * NSA_HPG + j) * HEAD_DIM + ar(HALF)
    for g in range(len(DIL_CONFIGS)):
        for jj in range(DIL_HPG):
            d = O_DIL + (2 * g + jj // 2) * LANES + (jj % 2) * HALF
            m[d:d + HALF] = 960 + (g * DIL_HPG + jj) * HEAD_DIM + ar(HALF)
    m[O_DIF:O_DIF + 512] = 1536 + ar(512)
    return m


def _gather_axis(x, idx, axis):
    pieces = []
    i = 0
    n = len(idx)
    while i < n:
        j = i + 1
        if idx[i] < 0:
            while j < n and idx[j] < 0:
                j += 1
            shape = list(x.shape)
            shape[axis] = j - i
            pieces.append(jnp.zeros(shape, x.dtype))
        else:
            while j < n and idx[j] == idx[j - 1] + 1:
                j += 1
            pieces.append(lax.slice_in_dim(x, int(idx[i]), int(idx[j - 1]) + 1, axis=axis))
        i = j
    return jnp.concatenate(pieces, axis=axis)


def _cparams(sem, vmem_mb=48):
    return pltpu.CompilerParams(dimension_semantics=sem, vmem_limit_bytes=vmem_mb * 1024 * 1024)


def _alibi_slope(k, n_heads):
    return float(2.0 ** (-ALIBI_MAX_EXP * (k + 1) / n_heads)) * LOG2E


def _src_col_scale(n_pad):
    sc = np.ones(n_pad, np.float32)
    for name, width in (("fox_q", 448), ("nsa_q", 512), ("dil_q", 576), ("diff_q", 512)):
        sc[_SRC[name]:_SRC[name] + width] = Q_SCALE
    return sc


def _relayout_plan(colmap, n_src, tile):
    n_tiles = len(colmap) // tile
    last = (n_src - 1) // tile
    win = np.zeros(n_tiles, np.int32)
    local = -np.ones((n_tiles, 1, tile), np.int32)
    for t in range(n_tiles):
        src = colmap[t * tile:(t + 1) * tile]
        used = src[src >= 0]
        a = min(int(used.min()) // tile, max(last - 1, 0))
        assert int(used.max()) < (a + 2) * tile, "tile sources exceed the two-block window"
        win[t] = a
        local[t, 0] = np.where(src >= 0, src - a * tile, -1)
    return win, local


def _relayout_kernel(win_ref, idx_ref, sa_ref, sb_ref, wa_ref, wb_ref, o_ref, *, n_src, tile):
    a = win_ref[pl.program_id(1)]
    idx = idx_ref[0]
    row = lax.broadcasted_iota(jnp.int32, (tile, tile), 0)
    col = lax.broadcasted_iota(jnp.int32, wa_ref.shape[1:], 1)
    acc = None
    for part, (w_ref, s_ref) in enumerate(((wa_ref, sa_ref), (wb_ref, sb_ref))):
        sel = jnp.where(row + part * tile == idx, 1.0, 0.0).astype(BF16)
        w = jnp.where(col + (a + part) * tile < n_src, w_ref[0] * s_ref[0], 0.0).astype(BF16)
        d = jnp.dot(w, sel, preferred_element_type=F32)
        acc = d if acc is None else acc + d
    o_ref[0] = acc.astype(BF16)


def _relayout_w_in(w_in, colmap, tile=RELAYOUT_TILE):
    depth, d, n_src = w_in.shape
    win, local = _relayout_plan(colmap, n_src, tile)
    n_tiles = len(win)
    n_blk = -(-n_src // tile)
    scale = jnp.asarray(_src_col_scale(n_blk * tile).reshape(n_blk, 1, tile))
    grid_spec = pltpu.PrefetchScalarGridSpec(
        num_scalar_prefetch=1,
        grid=(depth, n_tiles),
        in_specs=[
            pl.BlockSpec((1, 1, tile), lambda l, t, win: (t, 0, 0)),
            pl.BlockSpec((1, 1, tile), lambda l, t, win: (win[t], 0, 0)),
            pl.BlockSpec((1, 1, tile), lambda l, t, win: (win[t] + 1, 0, 0)),
            pl.BlockSpec((1, d, tile), lambda l, t, win: (l, 0, win[t])),
            pl.BlockSpec((1, d, tile), lambda l, t, win: (l, 0, win[t] + 1)),
        ],
        out_specs=pl.BlockSpec((1, d, tile), lambda l, t, win: (l, 0, t)),
    )
    return pl.pallas_call(
        functools.partial(_relayout_kernel, n_src=n_src, tile=tile),
        grid_spec=grid_spec,
        out_shape=jax.ShapeDtypeStruct((depth, d, n_tiles * tile), BF16),
        compiler_params=_cparams(("parallel", "arbitrary")),
        name="w_in_relayout",
    )(jnp.asarray(win), jnp.asarray(local), scale, scale, w_in, w_in)


def _mod_kernel(c_ref, w_ref, b_ref, o_ref):
    nb = c_ref.shape[0]
    tn = w_ref.shape[2]
    for b in range(nb):
        cb = c_ref[b]
        ca = cb * jax.nn.sigmoid(cb)
        for j in range(tn // LANES):
            sl = slice(j * LANES, (j + 1) * LANES)
            w = w_ref[0, :, sl]
            o_ref[0, b:b + 1, sl] = jnp.sum(w * ca, axis=0, keepdims=True) + b_ref[0, :, sl]


def _modulation(c, ada_w, ada_b, tn=512):
    depth, d, n = ada_w.shape
    nb = c.shape[0]
    c_rep = jnp.broadcast_to(c[:, :, None], (nb, d, LANES))
    return pl.pallas_call(
        _mod_kernel,
        grid=(depth, n // tn),
        in_specs=[
            pl.BlockSpec((nb, d, LANES), lambda l, j: (0, 0, 0)),
            pl.BlockSpec((1, d, tn), lambda l, j: (l, 0, j)),
            pl.BlockSpec((1, 1, tn), lambda l, j: (l, 0, j)),
        ],
        out_specs=pl.BlockSpec((1, nb, tn), lambda l, j: (l, 0, j)),
        out_shape=jax.ShapeDtypeStruct((depth, nb, n), F32),
        compiler_params=_cparams(("parallel", "parallel")),
        name="adaln_mod",
    )(c_rep, ada_w, ada_b.reshape(depth, 1, n))


def _norm_mod(h_ref, g_ref, sc_ref, sh_ref):
    x = h_ref[...]
    ms = jnp.mean(x * x, axis=-1, keepdims=True)
    y = x * lax.rsqrt(ms + RMS_EPS) * g_ref[...]
    return (y * (1.0 + sc_ref[0]) + sh_ref[0]).astype(BF16)


def _proj_kernel(h_ref, g_ref, sc_ref, sh_ref, w_ref, wf_ref, pb_ref, pf_ref, u_ref):
    @pl.when(pl.program_id(1) == 0)
    def _():
        u = _norm_mod(h_ref, g_ref, sc_ref, sh_ref)
        u_ref[...] = u
        pf_ref[...] = jnp.dot(u, wf_ref[...], preferred_element_type=F32)

    pb_ref[...] = jnp.dot(u_ref[...], w_ref[0], preferred_element_type=F32).astype(BF16)


def _norm_proj(h, gain, sc, sh, w_all, layer, wf, seq, tm=1024, tn=512):
    m, d = h.shape
    per = seq // tm
    return pl.pallas_call(
        _proj_kernel,
        grid=(m // tm, N_PB // tn),
        in_specs=[
            pl.BlockSpec((tm, d), lambda i, j: (i, 0)),
            pl.BlockSpec((1, d), lambda i, j: (0, 0)),
            pl.BlockSpec((1, 1, d), lambda i, j: (i // per, 0, 0)),
            pl.BlockSpec((1, 1, d), lambda i, j: (i // per, 0, 0)),
            pl.BlockSpec((1, d, tn), lambda i, j: (layer, 0, j)),
            pl.BlockSpec((d, LANES), lambda i, j: (0, 0)),
        ],
        out_specs=[
            pl.BlockSpec((tm, tn), lambda i, j: (i, j)),
            pl.BlockSpec((tm, LANES), lambda i, j: (i, 0)),
        ],
        out_shape=[jax.ShapeDtypeStruct((m, N_PB), BF16), jax.ShapeDtypeStruct((m, LANES), F32)],
        scratch_shapes=[pltpu.VMEM((tm, d), BF16)],
        compiler_params=_cparams(("parallel", "arbitrary")),
        name="norm_in_proj",
    )(h, gain.reshape(1, d), sc, sh, w_all, wf)


def _ffn_up_kernel(h_ref, g_ref, sc_ref, sh_ref, wg_ref, wu_ref, a_ref, u_ref):
    @pl.when(pl.program_id(1) == 0)
    def _():
        u_ref[...] = _norm_mod(h_ref, g_ref, sc_ref, sh_ref)

    u = u_ref[...]
    gate = jnp.dot(u, wg_ref[0], preferred_element_type=F32)
    up = jnp.dot(u, wu_ref[0], preferred_element_type=F32)
    a_ref[...] = (gate * jax.nn.sigmoid(gate) * up).astype(BF16)


def _norm_ffn_up(h, gain, sc, sh, wg, wu, layer, seq, tm=1024, tn=512):
    m, d = h.shape
    n = wg.shape[2]
    per = seq // tm
    return pl.pallas_call(
        _ffn_up_kernel,
        grid=(m // tm, n // tn),
        in_specs=[
            pl.BlockSpec((tm, d), lambda i, j: (i, 0)),
            pl.BlockSpec((1, d), lambda i, j: (0, 0)),
            pl.BlockSpec((1, 1, d), lambda i, j: (i // per, 0, 0)),
            pl.BlockSpec((1, 1, d), lambda i, j: (i // per, 0, 0)),
            pl.BlockSpec((1, d, tn), lambda i, j: (layer, 0, j)),
            pl.BlockSpec((1, d, tn), lambda i, j: (layer, 0, j)),
        ],
        out_specs=pl.BlockSpec((tm, tn), lambda i, j: (i, j)),
        out_shape=jax.ShapeDtypeStruct((m, n), BF16),
        scratch_shapes=[pltpu.VMEM((tm, d), BF16)],
        compiler_params=_cparams(("parallel", "arbitrary")),
        name="norm_ffn_up",
    )(h, gain.reshape(1, d), sc, sh, wg, wu)


def _mm_res_kernel(*refs, widths):
    n = len(widths)
    a_refs, (w_ref, h_ref, g_ref, o_ref) = refs[:n], refs[n:]
    acc = None
    off = 0
    for a_ref, width in zip(a_refs, widths):
        d = jnp.dot(a_ref[...], w_ref[0, off:off + width, :], preferred_element_type=F32)
        acc = d if acc is None else acc + d
        off += width
    o_ref[...] = h_ref[...] + g_ref[0] * acc


def _matmul_gated_residual(a_parts, w, layer, h, gate, seq, tm=1024, tn=512, name="mm_res"):
    m = h.shape[0]
    widths = tuple(a.shape[1] for a in a_parts)
    k, n = w.shape[1], w.shape[2]
    assert sum(widths) == k
    per = seq // tm
    block_bytes = tm * k * 2 + k * tn * 2 + 2 * tm * tn * 4
    vmem_mb = (2 * block_bytes + 4 * tm * tn * 4) // (1024 * 1024) + 1
    return pl.pallas_call(
        functools.partial(_mm_res_kernel, widths=widths),
        grid=(m // tm, n // tn),
        in_specs=[pl.BlockSpec((tm, wd), lambda i, j: (i, 0)) for wd in widths] + [
            pl.BlockSpec((1, k, tn), lambda i, j: (layer, 0, j)),
            pl.BlockSpec((tm, tn), lambda i, j: (i, j)),
            pl.BlockSpec((1, 1, tn), lambda i, j: (i // per, 0, j)),
        ],
        out_specs=pl.BlockSpec((tm, tn), lambda i, j: (i, j)),
        out_shape=jax.ShapeDtypeStruct((m, n), F32),
        compiler_params=_cparams(("parallel", "arbitrary"), vmem_mb=vmem_mb),
        name=name,
    )(*a_parts, w, h, gate)


def _final_norm_kernel(h_ref, g_ref, o_ref):
    x = h_ref[...]
    ms = jnp.mean(x * x, axis=-1, keepdims=True)
    o_ref[...] = x * lax.rsqrt(ms + RMS_EPS) * g_ref[...]


def _final_norm(h, gain, tm=512):
    m, d = h.shape
    return pl.pallas_call(
        _final_norm_kernel,
        grid=(m // tm,),
        in_specs=[pl.BlockSpec((tm, d), lambda i: (i, 0)), pl.BlockSpec((1, d), lambda i: (0, 0))],
        out_specs=pl.BlockSpec((tm, d), lambda i: (i, 0)),
        out_shape=jax.ShapeDtypeStruct((m, d), F32),
        compiler_params=_cparams(("parallel",)),
        name="final_norm",
    )(h, gain.reshape(1, d))


def _half_mask(shape, half):
    lane = lax.broadcasted_iota(jnp.int32, shape, 1)
    return (lane < HALF) if half == 0 else (lane >= HALF)


def _pick_half(qp, half):
    return jnp.where(_half_mask(qp.shape, half), qp, jnp.zeros_like(qp))


def _lane_tile(x, n):
    return x if n == 1 else jnp.concatenate([x] * n, axis=1)


def _qk(qh, k):
    return lax.dot_general(qh, k, (((1,), (1,)), ((), ())), preferred_element_type=F32)


def _merge_halves(lo, hi):
    if hi is None:
        hi = jnp.zeros_like(lo)
    return jnp.where(_half_mask(lo.shape, 0), lo, hi)


def _flash_scratch(n_heads, tq):
    return [pltpu.VMEM((n_heads, tq, LANES), F32), pltpu.VMEM((n_heads, tq, LANES), F32),
            pltpu.VMEM((n_heads, tq, LANES), F32)]


def _flash_reset(m_ref, l_ref, acc_ref):
    m_ref[...] = jnp.full(m_ref.shape, NEG, F32)
    l_ref[...] = jnp.zeros(l_ref.shape, F32)
    acc_ref[...] = jnp.zeros(acc_ref.shape, F32)


def _flash_update(h, s, valid, v, m_ref, l_ref, acc_ref):
    m_old = m_ref[h]
    m_new = jnp.maximum(m_old, jnp.max(s, axis=1, keepdims=True))
    alpha = jnp.exp2(m_old - m_new)
    p = jnp.exp2(s - _lane_tile(m_new, s.shape[1] // LANES))
    if valid is not None:
        p = jnp.where(valid, p, 0.0)
    l_ref[h] = alpha * l_ref[h] + jnp.sum(p, axis=1, keepdims=True)
    acc_ref[h] = alpha * acc_ref[h] + jnp.dot(p.astype(BF16), v, preferred_element_type=F32)
    m_ref[h] = m_new


def _softmax_pv(s, v):
    m = jnp.max(s, axis=1, keepdims=True)
    p = jnp.exp2(s - m)
    l = jnp.sum(p, axis=1, keepdims=True)
    return jnp.dot(p.astype(BF16), v, preferred_element_type=F32) / l, m + jnp.log2(l)


def _window_start(qs, back, width, seq, align):
    return pl.multiple_of(jnp.clip(qs - back, 0, seq - width), align)


def _cumsum_kernel(pf_ref, fb_ref, o_ref, *, chunk):
    seq = pf_ref.shape[1]
    r = lax.broadcasted_iota(jnp.int32, (chunk, chunk), 0)
    c = lax.broadcasted_iota(jnp.int32, (chunk, chunk), 1)
    tri = jnp.where(r >= c, 1.0, 0.0).astype(BF16)

    def body(i, carry):
        st = pl.multiple_of(i * chunk, chunk)
        x = pf_ref[0, pl.ds(st, chunk), :] + fb_ref[...]
        ls = jnp.minimum(x, 0.0) - jnp.log1p(jnp.exp(-jnp.abs(x)))
        hi = ls.astype(BF16)
        r1 = ls - hi.astype(F32)
        mid = r1.astype(BF16)
        lo = (r1 - mid.astype(F32)).astype(BF16)
        cs = (jnp.dot(tri, hi, preferred_element_type=F32)
              + jnp.dot(tri, mid, preferred_element_type=F32)
              + jnp.dot(tri, lo, preferred_element_type=F32))
        out = cs + carry
        o_ref[0, pl.ds(st, chunk), :] = out * LOG2E
        return out[chunk - 1:chunk, :]

    lax.fori_loop(0, seq // chunk, body, jnp.zeros((1, LANES), F32))


def _forget_cumsum(pf, f_bias, chunk=256):
    nb, seq, _ = pf.shape
    fb = jnp.zeros((1, LANES), F32).at[0, :FOX_HEADS].set(f_bias.astype(F32))
    return pl.pallas_call(
        functools.partial(_cumsum_kernel, chunk=chunk),
        grid=(nb,),
        in_specs=[pl.BlockSpec((1, seq, LANES), lambda b: (b, 0, 0)), pl.BlockSpec((1, LANES), lambda b: (0, 0))],
        out_specs=pl.BlockSpec((1, seq, LANES), lambda b: (b, 0, 0)),
        out_shape=jax.ShapeDtypeStruct((nb, seq, LANES), F32),
        compiler_params=_cparams(("parallel",)),
        name="forget_cumsum",
    )(pf, fb)


def _fox_kernel(q_ref, k_ref, v_ref, cc_ref, cr_ref, o_ref, qs_ref, cq_ref, m_ref, l_ref, acc_ref, *, t):
    qi = pl.program_id(1)
    _flash_reset(m_ref, l_ref, acc_ref)
    for h in range(FOX_HEADS):
        qs_ref[h] = _pick_half(q_ref[0, :, (h // 2) * LANES:(h // 2 + 1) * LANES], h % 2)
        cq_ref[h] = jnp.broadcast_to(cc_ref[0, :, h:h + 1], (t, LANES))

    def step(ki, masked):
        ks = pl.multiple_of(ki * t, t)
        if masked:
            causal = (lax.broadcasted_iota(jnp.int32, (t, t), 0) >= lax.broadcasted_iota(jnp.int32, (t, t), 1))
        for h in range(FOX_HEADS):
            sl = slice((h // 2) * LANES, (h // 2 + 1) * LANES)
            k = k_ref[0, pl.ds(ks, t), sl]
            v = v_ref[0, pl.ds(ks, t), sl]
            s = ((_qk(qs_ref[h], k) + _lane_tile(cq_ref[h], t // LANES))
                 - cr_ref[0, h:h + 1, pl.ds(ks, t)])
            if masked:
                s = jnp.where(causal, s, NEG)
            _flash_update(h, s, None, v, m_ref, l_ref, acc_ref)

    def body(ki, c):
        step(ki, False)
        return c

    lax.fori_loop(0, qi, body, 0)
    step(qi, True)
    for p in range(4):
        lo = acc_ref[2 * p] / l_ref[2 * p]
        hi = acc_ref[2 * p + 1] / l_ref[2 * p + 1] if 2 * p + 1 < FOX_HEADS else None
        o_ref[0, :, p * LANES:(p + 1) * LANES] = _merge_halves(lo, hi).astype(BF16)


def _fox_attention(pb, cum, cumrow, t=512):
    nb, seq, _ = pb.shape
    t = min(t, seq)
    w = 4 * LANES
    return pl.pallas_call(
        functools.partial(_fox_kernel, t=t),
        grid=(nb, seq // t),
        in_specs=[
            pl.BlockSpec((1, t, w), lambda b, i: (b, i, P_FOXQ // w)),
            pl.BlockSpec((1, seq, w), lambda b, i: (b, 0, P_FOXK // w)),
            pl.BlockSpec((1, seq, w), lambda b, i: (b, 0, P_FOXV // w)),
            pl.BlockSpec((1, t, LANES), lambda b, i: (b, i, 0)),
            pl.BlockSpec((1, 8, seq), lambda b, i: (b, 0, 0)),
        ],
        out_specs=pl.BlockSpec((1, t, w), lambda b, i: (b, i, 0)),
        out_shape=jax.ShapeDtypeStruct((nb, seq, w), BF16),
        scratch_shapes=([pltpu.VMEM((FOX_HEADS, t, LANES), BF16), pltpu.VMEM((FOX_HEADS, t, LANES), F32)]
                        + _flash_scratch(FOX_HEADS, t)),
        compiler_params=_cparams(("parallel", "arbitrary")),
        name="fox_attention",
    )(pb, pb, pb, cum, cumrow)


def _diff_kernel(q_ref, k_ref, v_ref, lam_ref, g_ref, o_ref, qs_ref, m_ref, l_ref, acc_ref, *, t, lam_init):
    qi = pl.program_id(1)
    n_slots = 2 * DIFF_HEADS
    _flash_reset(m_ref, l_ref, acc_ref)
    for n in range(n_slots):
        qs_ref[n] = _pick_half(q_ref[0, :, (n // 2) * LANES:(n // 2 + 1) * LANES], n % 2)
    tpos = _lane_tile((qi * t + lax.broadcasted_iota(jnp.int32, (t, LANES), 0)).astype(F32), t // LANES)

    def step(ki, masked):
        ks = pl.multiple_of(ki * t, t)
        kpos = (ks + lax.broadcasted_iota(jnp.int32, (1, t), 1)).astype(F32)
        if masked:
            causal = (lax.broadcasted_iota(jnp.int32, (t, t), 0) >= lax.broadcasted_iota(jnp.int32, (t, t), 1))
        for n in range(n_slots):
            hd = n // 2
            sl = slice(hd * LANES, (hd + 1) * LANES)
            slope = _alibi_slope(hd, DIFF_HEADS)
            k = k_ref[0, pl.ds(ks, t), sl]
            v = v_ref[0, pl.ds(ks, t), sl]
            s = (_qk(qs_ref[n], k) + tpos * (-slope)) + kpos * slope
            if masked:
                s = jnp.where(causal, s, NEG)
            _flash_update(n, s, None, v, m_ref, l_ref, acc_ref)

    def body(ki, c):
        step(ki, False)
        return c

    lax.fori_loop(0, qi, body, 0)
    step(qi, True)
    lv = lam_ref[...]
    lam = (jnp.exp(jnp.sum(lv[0:1] * lv[1:2], axis=1, keepdims=True))
           - jnp.exp(jnp.sum(lv[2:3] * lv[3:4], axis=1, keepdims=True)) + lam_init)
    for hd in range(DIFF_HEADS):
        o = acc_ref[2 * hd] / l_ref[2 * hd] - lam * (acc_ref[2 * hd + 1] / l_ref[2 * hd + 1])
        ms = jnp.mean(o * o, axis=-1, keepdims=True)
        y = o * lax.rsqrt(ms + RMS_EPS) * g_ref[...]
        o_ref[0, :, hd * LANES:(hd + 1) * LANES] = (y * (1.0 - lam_init)).astype(BF16)


def _diff_attention(pb, lam_vecs, subln_g, lam_init, t=512):
    nb, seq, _ = pb.shape
    t = min(t, seq)
    w = DIFF_HEADS * LANES
    return pl.pallas_call(
        functools.partial(_diff_kernel, t=t, lam_init=lam_init),
        grid=(nb, seq // t),
        in_specs=[
            pl.BlockSpec((1, t, w), lambda b, i: (b, i, P_DIFQ // w)),
            pl.BlockSpec((1, seq, w), lambda b, i: (b, 0, P_DIFK // w)),
            pl.BlockSpec((1, seq, w), lambda b, i: (b, 0, P_DIFV // w)),
            pl.BlockSpec((4, HEAD_DIM), lambda b, i: (0, 0)),
            pl.BlockSpec((1, LANES), lambda b, i: (0, 0)),
        ],
        out_specs=pl.BlockSpec((1, t, w), lambda b, i: (b, i, 0)),
        out_shape=jax.ShapeDtypeStruct((nb, seq, w), BF16),
        scratch_shapes=[pltpu.VMEM((2 * DIFF_HEADS, t, LANES), BF16)] + _flash_scratch(2 * DIFF_HEADS, t),
        compiler_params=_cparams(("parallel", "arbitrary")),
        name="diff_attention",
    )(pb, pb, pb, lam_vecs.astype(F32), subln_g.astype(F32).reshape(1, LANES))


DIL_SUBWINDOW = 128


def _dil_group_kernel(q_ref, k_ref, v_ref, o_ref, lse_ref, *, tq, sub_len, dilation, head0):
    qs = pl.program_id(2) * tq
    width = min(2 * tq, sub_len)
    start = _window_start(qs, tq, width, sub_len, tq)
    dist = ((qs - start) + lax.broadcasted_iota(jnp.int32, (tq, width), 0)
            - lax.broadcasted_iota(jnp.int32, (tq, width), 1))
    valid = (dist | (DIL_SUBWINDOW - dist)) >= 0
    distf = dist.astype(F32)
    lane = lax.broadcasted_iota(jnp.int32, (tq, LANES), 1)
    lse_blk = jnp.zeros((tq, LANES), F32)
    outs = []
    for jj in range(DIL_HPG):
        sl = slice((jj // 2) * LANES, (jj // 2 + 1) * LANES)
        slope = _alibi_slope(head0 + jj, DIL_HEADS) * dilation
        k = k_ref[0, pl.ds(start, width), sl]
        v = v_ref[0, pl.ds(start, width), sl]
        s = jnp.where(valid, _qk(_pick_half(q_ref[0, :, sl], jj % 2), k) - slope * distf, NEG)
        o, lse = _softmax_pv(s, v)
        outs.append(o)
        lse_blk = jnp.where(lane == jj, lse, lse_blk)
    o_ref[0, :, 0:LANES] = _merge_halves(outs[0], outs[1])
    o_ref[0, :, LANES:2 * LANES] = _merge_halves(outs[2], None)
    lse_ref[0] = lse_blk


def _dil_group(pb, g, tq=256):
    nb, seq, n_pb = pb.shape
    window, dilation = DIL_CONFIGS[g]
    assert window == DIL_SUBWINDOW * dilation
    sub_len = seq // dilation
    tq = min(tq, sub_len)
    gw = DIL_GW
    per_tok = n_pb // gw
    view = pb.reshape(nb, sub_len, dilation * n_pb)
    col = lambda base: (lambda b, r, i: (b, 0, r * per_tok + base // gw + g))
    o, lse = pl.pallas_call(
        functools.partial(_dil_group_kernel, tq=tq, sub_len=sub_len, dilation=dilation, head0=g * DIL_HPG),
        grid=(nb, dilation, sub_len // tq),
        in_specs=[
            pl.BlockSpec((1, tq, gw), lambda b, r, i: (b, i, r * per_tok + P_DILQ // gw + g)),
            pl.BlockSpec((1, sub_len, gw), col(P_DILK)),
            pl.BlockSpec((1, sub_len, gw), col(P_DILV)),
        ],
        out_specs=[pl.BlockSpec((1, tq, gw), lambda b, r, i: (b, i, r)),
                   pl.BlockSpec((1, tq, LANES), lambda b, r, i: (b, i, r))],
        out_shape=[jax.ShapeDtypeStruct((nb, sub_len, dilation * gw), F32),
                   jax.ShapeDtypeStruct((nb, sub_len, dilation * LANES), F32)],
        compiler_params=_cparams(("parallel", "parallel", "arbitrary")),
        name=f"dil_group{g}",
    )(view, view, view)
    return o.reshape(nb, seq, gw), lse.reshape(nb, seq, LANES)


def _dil_mix_kernel(*refs):
    ng = len(DIL_CONFIGS)
    o_refs, l_refs, (x_ref, out_ref) = refs[:ng], refs[ng:2 * ng], refs[2 * ng:]
    lses = [r[0] for r in l_refs]
    mx = lses[0]
    for g in range(1, ng):
        mx = jnp.maximum(mx, lses[g])
    es = [jnp.exp2(l - mx) for l in lses]
    tot = es[0]
    for g in range(1, ng):
        tot = tot + es[g]
    for g in range(ng):
        w = es[g] / tot
        hi = w.astype(BF16)
        lo = (w - hi.astype(F32)).astype(BF16)
        wx = (jnp.dot(hi, x_ref[...], preferred_element_type=F32)
              + jnp.dot(lo, x_ref[...], preferred_element_type=F32))
        out_ref[0, :, g * DIL_GW:(g + 1) * DIL_GW] = (o_refs[g][0] * wx).astype(BF16)


def _dil_attention(pb, tm=512):
    nb, seq, _ = pb.shape
    ng = len(DIL_CONFIGS)
    outs, lses = zip(*[_dil_group(pb, g) for g in range(ng)])
    spread = np.zeros((LANES, DIL_GW), np.float32)
    for jj in range(DIL_HPG):
        spread[jj, jj * HALF:(jj + 1) * HALF] = 1.0
    tm = min(tm, seq)
    return pl.pallas_call(
        _dil_mix_kernel,
        grid=(nb, seq // tm),
        in_specs=([pl.BlockSpec((1, tm, DIL_GW), lambda b, i: (b, i, 0))] * ng
                  + [pl.BlockSpec((1, tm, LANES), lambda b, i: (b, i, 0))] * ng
                  + [pl.BlockSpec((LANES, DIL_GW), lambda b, i: (0, 0))]),
        out_specs=pl.BlockSpec((1, tm, DIL_W), lambda b, i: (b, i, 0)),
        out_shape=jax.ShapeDtypeStruct((nb, seq, DIL_W), BF16),
        compiler_params=_cparams(("parallel", "parallel")),
        name="dil_mix",
    )(*outs, *lses, jnp.asarray(spread, BF16))


def _gelu_tanh(x):
    return 0.5 * x * (1.0 + jnp.tanh(math.sqrt(2.0 / math.pi) * (x + 0.044715 * (x * x * x))))


def _compress_kernel(rk_ref, rv_ref, w1_ref, w2_ref, pos_ref, kc_ref, vc_ref):
    nc = rk_ref.shape[1]
    rowi = lax.broadcasted_iota(jnp.int32, (nc, LANES), 0)
    for kv, (r_ref, o_ref) in enumerate(((rk_ref, kc_ref), (rv_ref, vc_ref))):
        r = r_ref[0]
        a = jnp.dot(r, w1_ref[kv, 0], preferred_element_type=F32)
        b = jnp.dot(r, w1_ref[kv, 1], preferred_element_type=F32)
        pt = (jnp.dot(pos_ref[kv, 0], w1_ref[kv, 0], preferred_element_type=F32)
              + jnp.dot(pos_ref[kv, 1], w1_ref[kv, 1], preferred_element_type=F32))[0:1]
        pre = a + pltpu.roll(b, nc - 1, 0) + pt
        out = jnp.dot(_gelu_tanh(pre).astype(BF16), w2_ref[kv], preferred_element_type=F32)
        o_ref[0] = jnp.where(rowi < nc - 1, out, 0.0).astype(BF16)


def _compress_weights(cmp_w1, cmp_w2, cmp_pos):
    half = NSA_CMP_LEN // 2
    w1 = cmp_w1.reshape(2, 2, half, HEAD_DIM, HEAD_DIM)
    z = jnp.zeros_like(w1)
    w1x = jnp.concatenate([jnp.concatenate([w1, z], -1), jnp.concatenate([z, w1], -1)], -2)
    w1x = w1x.reshape(2, 2, half * LANES, LANES).astype(BF16)
    z2 = jnp.zeros_like(cmp_w2)
    w2x = jnp.concatenate([jnp.concatenate([cmp_w2, z2], -1), jnp.concatenate([z2, cmp_w2], -1)], -2).astype(BF16)
    pos = cmp_pos.reshape(2, 2, half, 1, HEAD_DIM)
    posx = jnp.broadcast_to(pos, (2, 2, half, NSA_GROUPS, HEAD_DIM)).reshape(2, 2, 1, half * LANES)
    posx = jnp.broadcast_to(posx, (2, 2, 8, half * LANES)).astype(BF16)
    return w1x, w2x, posx


def _nsa_compress(rk, rv, w1x, w2x, posx):
    nb, nc, width = rk.shape
    full = lambda a: pl.BlockSpec(a.shape, lambda b: (0,) * a.ndim)
    return pl.pallas_call(
        _compress_kernel,
        grid=(nb,),
        in_specs=[pl.BlockSpec((1, nc, width), lambda b: (b, 0, 0)), pl.BlockSpec((1, nc, width), lambda b: (b, 0, 0)),
                  full(w1x), full(w2x), full(posx)],
        out_specs=[pl.BlockSpec((1, nc, LANES), lambda b: (b, 0, 0))] * 2,
        out_shape=[jax.ShapeDtypeStruct((nb, nc, LANES), BF16)] * 2,
        compiler_params=_cparams(("parallel",)),
        name="nsa_compress",
    )(rk, rv, w1x, w2x, posx)


def _cmp_to_sel_matrix(n_cmp_pad, n_cmp, n_sel):
    a = NSA_SEL_BLOCK // NSA_CMP_STRIDE
    b = NSA_CMP_LEN // NSA_CMP_STRIDE
    j = np.arange(n_sel)[:, None, None]
    idx = a * j + np.arange(a)[None, :, None] - np.arange(b)[None, None, :]
    jj = np.broadcast_to(j, idx.shape)
    ok = (idx >= 0) & (idx < n_cmp)
    m = np.zeros((n_cmp_pad, LANES), np.float32)
    np.add.at(m, (idx[ok], jj[ok]), 1.0)
    return m


def _nsa_kernel(q_ref, kc_ref, vc_ref, ks_ref, vs_ref, kw_ref, vw_ref, pf_ref, e_ref, m2t_ref, t2_ref, gx_ref,
                o_ref, qs_ref, flag_ref, ow_ref, m_ref, l_ref, acc_ref, *, tq, tk, seq, n_sel, n_top):
    qi = pl.program_id(1)
    qs = qi * tq
    nc = kc_ref.shape[1]
    heads = [(g, j) for g in range(NSA_GROUPS) for j in range(NSA_HPG)]

    def slope_of(g, j):
        return _alibi_slope(g * NSA_HPG + j, NSA_HEADS)

    for n, (g, j) in enumerate(heads):
        qs_ref[n] = _pick_half(q_ref[0, :, j * LANES:(j + 1) * LANES], g)

    cidx = lax.broadcasted_iota(jnp.int32, (tq, nc), 1)
    dist_c = (qs + lax.broadcasted_iota(jnp.int32, (tq, nc), 0)) - (cidx * NSA_CMP_STRIDE + (NSA_CMP_LEN - 1))
    valid_c = (dist_c | ((nc - 2) - cidx)) >= 0
    dist_cf = dist_c.astype(F32)
    kc = kc_ref[0]
    vc = vc_ref[0]
    o_c = {}
    psum = [None] * NSA_GROUPS
    for n, (g, j) in enumerate(heads):
        s = jnp.where(valid_c, _qk(qs_ref[n], kc) - slope_of(g, j) * dist_cf, NEG)
        m = jnp.max(s, axis=1, keepdims=True)
        p = jnp.where(valid_c, jnp.exp2(s - m), 0.0)
        l = jnp.sum(p, axis=1, keepdims=True)
        pn = p / jnp.where(l > 0.0, l, 1.0)
        o_c[(g, j)] = jnp.dot(pn.astype(BF16), vc, preferred_element_type=F32)
        psum[g] = pn if psum[g] is None else psum[g] + pn

    back = -(-NSA_WINDOW // tq) * tq
    width = min(back + tq, seq)
    start = _window_start(qs, back, width, seq, tq)
    dist = ((qs - start) + lax.broadcasted_iota(jnp.int32, (tq, width), 0)
            - lax.broadcasted_iota(jnp.int32, (tq, width), 1))
    valid_w = (dist | ((NSA_WINDOW - 1) - dist)) >= 0
    distf = dist.astype(F32)
    kw = kw_ref[0, pl.ds(start, width), :]
    vw = vw_ref[0, pl.ds(start, width), :]
    for n, (g, j) in enumerate(heads):
        s = jnp.where(valid_w, _qk(qs_ref[n], kw) - slope_of(g, j) * distf, NEG)
        ow_ref[n], _ = _softmax_pv(s, vw)

    rsel = -(-n_sel // 8) * 8
    blk = lax.broadcasted_iota(jnp.int32, (rsel, tq), 0)
    blkf = blk.astype(F32)
    tpos = qs + lax.broadcasted_iota(jnp.int32, (rsel, tq), 1)
    cur = tpos >> 6
    forced = (blk == 0) | (blk == cur) | (blk == cur - 1)
    causal_b = blk * NSA_SEL_BLOCK <= tpos
    sel_b = []
    for g in range(NSA_GROUPS):
        ph = psum[g].astype(BF16)
        plo = (psum[g] - ph.astype(F32)).astype(BF16)
        imp = (_qk(m2t_ref[...], ph) + _qk(m2t_ref[...], plo))[0:rsel]
        score = jnp.where(causal_b, jnp.where(forced, NSA_FORCE_SCORE, imp), NEG)
        score = jnp.where(blk < n_sel, score, PICKED)
        sel = jnp.zeros((rsel, tq), F32)
        for _ in range(n_top):
            mx = jnp.max(score, axis=0, keepdims=True)
            idx = jnp.min(jnp.where(score == mx, blkf, float(LANES)), axis=0, keepdims=True)
            hit = blkf == idx
            sel = jnp.where(hit, 1.0, sel)
            score = jnp.where(hit, PICKED, score)
        sel_b.append(sel.astype(BF16))

    ones = jnp.ones((tq, LANES), BF16)
    for g in range(NSA_GROUPS):
        per_block = jnp.dot(sel_b[g], ones, preferred_element_type=F32)
        flag_ref[g] = jnp.dot(t2_ref[:, 0:rsel], per_block.astype(BF16), preferred_element_type=F32)
    _flash_reset(m_ref, l_ref, acc_ref)

    def slc_body(ki, c):
        ks = pl.multiple_of(ki * tk, tk)
        for g in range(NSA_GROUPS):
            @pl.when(flag_ref[g, pl.ds(ki, 1), :][0, 0] > 0.5)
            def _(g=g):
                dist = ((qs - ks) + lax.broadcasted_iota(jnp.int32, (tq, tk), 0)
                        - lax.broadcasted_iota(jnp.int32, (tq, tk), 1))
                distf = dist.astype(F32)
                k = ks_ref[0, pl.ds(ks, tk), :]
                v = vs_ref[0, pl.ds(ks, tk), :]
                picked = lax.dot_general(sel_b[g], e_ref[0:rsel, pl.ds(ks, tk)], (((0,), (0,)), ((), ())),
                                         preferred_element_type=F32)
                valid = jnp.where(dist >= 0, picked, 0.0) > 0.5
                for j in range(NSA_HPG):
                    n = g * NSA_HPG + j
                    s = jnp.where(valid, _qk(qs_ref[n], k) - slope_of(g, j) * distf, NEG)
                    _flash_update(n, s, None, v, m_ref, l_ref, acc_ref)
        return c

    lax.fori_loop(0, (qs + tq + tk - 1) // tk, slc_body, 0)

    gl = jax.nn.sigmoid(pf_ref[0])
    g_hi = gl.astype(BF16)
    g_lo = (gl - g_hi.astype(F32)).astype(BF16)

    def gate(j, r):
        x = gx_ref[j * 3 + r]
        return jnp.dot(g_hi, x, preferred_element_type=F32) + jnp.dot(g_lo, x, preferred_element_type=F32)

    for j in range(NSA_HPG):
        n0, n1 = j, NSA_HPG + j
        comb = (gate(j, 0) * _merge_halves(o_c[(0, j)], o_c[(1, j)])
                + gate(j, 1) * _merge_halves(acc_ref[n0] / l_ref[n0], acc_ref[n1] / l_ref[n1])
                + gate(j, 2) * _merge_halves(ow_ref[n0], ow_ref[n1]))
        o_ref[0, :, j * LANES:(j + 1) * LANES] = comb.astype(BF16)


def _nsa_attention(pb, pf, kc, vc, tq=256, tk=512):
    nb, seq, _ = pb.shape
    tk = min(tk, seq)
    nc = kc.shape[1]
    n_sel = seq // NSA_SEL_BLOCK
    n_top = min(NSA_TOP_N, n_sel)
    n_cmp = seq // NSA_CMP_STRIDE - NSA_CMP_LEN // NSA_CMP_STRIDE + 1
    expand = (np.arange(LANES)[:, None] == (np.arange(seq)[None, :] // NSA_SEL_BLOCK))
    e = jnp.asarray(expand, BF16)
    m2t = jnp.asarray(_cmp_to_sel_matrix(nc, n_cmp, n_sel).T, BF16)
    n_tiles = seq // tk
    tile_rows = -(-n_tiles // 8) * 8
    blocks_per_tile = tk // NSA_SEL_BLOCK
    t2 = jnp.asarray(np.arange(tile_rows)[:, None] == (np.arange(LANES)[None, :] // blocks_per_tile), BF16)
    gx = np.zeros((NSA_HPG * 3, LANES, LANES), np.float32)
    for j in range(NSA_HPG):
        for r in range(3):
            for g in range(NSA_GROUPS):
                gx[j * 3 + r, GATE_COL + (g * NSA_HPG + j) * 3 + r, g * HALF:(g + 1) * HALF] = 1.0
    gx = jnp.asarray(gx, BF16)
    w = NSA_HPG * LANES
    kvb = P_NSAKV // LANES
    kv_spec = lambda i: pl.BlockSpec((1, seq, LANES), lambda b, q, i=i: (b, 0, kvb + i))
    return pl.pallas_call(
        functools.partial(_nsa_kernel, tq=tq, tk=tk, seq=seq, n_sel=n_sel, n_top=n_top),
        grid=(nb, seq // tq),
        in_specs=[
            pl.BlockSpec((1, tq, w), lambda b, i: (b, i, P_NSAQ // w)),
            pl.BlockSpec((1, nc, LANES), lambda b, i: (b, 0, 0)),
            pl.BlockSpec((1, nc, LANES), lambda b, i: (b, 0, 0)),
            kv_spec(2), kv_spec(3), kv_spec(4), kv_spec(5),
            pl.BlockSpec((1, tq, LANES), lambda b, i: (b, i, 0)),
            pl.BlockSpec((LANES, seq), lambda b, i: (0, 0)),
            pl.BlockSpec((LANES, nc), lambda b, i: (0, 0)),
            pl.BlockSpec((tile_rows, LANES), lambda b, i: (0, 0)),
            pl.BlockSpec((NSA_HPG * 3, LANES, LANES), lambda b, i: (0, 0, 0)),
        ],
        out_specs=pl.BlockSpec((1, tq, w), lambda b, i: (b, i, 0)),
        out_shape=jax.ShapeDtypeStruct((nb, seq, w), BF16),
        scratch_shapes=([pltpu.VMEM((NSA_HEADS, tq, LANES), BF16), pltpu.VMEM((NSA_GROUPS, tile_rows, LANES), F32),
                         pltpu.VMEM((NSA_HEADS, tq, LANES), F32)] + _flash_scratch(NSA_HEADS, tq)),
        compiler_params=_cparams(("parallel", "arbitrary")),
        name="nsa_attention",
    )(pb, kc, vc, pb, pb, pb, pb, pf, e, m2t, t2, gx)


def _mixers(pb, pf, f_bias, cmp_w1, cmp_w2, cmp_pos, lam_vecs, subln_g, lam_init):
    nb, seq, _ = pb.shape
    cum = _forget_cumsum(pf, f_bias)
    cumrow = jnp.transpose(cum[:, :, :8], (0, 2, 1))
    o_fox = _fox_attention(pb, cum, cumrow)
    nc = seq // NSA_CMP_STRIDE
    rk = pb[:, :, P_NSAKV:P_NSAKV + LANES].reshape(nb, nc, NSA_CMP_STRIDE * LANES)
    rv = pb[:, :, P_NSAKV + LANES:P_NSAKV + 2 * LANES].reshape(nb, nc, NSA_CMP_STRIDE * LANES)
    kc, vc = _nsa_compress(rk, rv, *_compress_weights(cmp_w1, cmp_w2, cmp_pos))
    o_nsa = _nsa_attention(pb, pf, kc, vc)
    o_dil = _dil_attention(pb)
    o_dif = _diff_attention(pb, lam_vecs, subln_g, lam_init)
    return [o_dil, o_fox, o_nsa, o_dif]


def kernel(x, c, ada_w, ada_b, norm_mix_g, norm_ffn_g, w_in, fox_f_bias, nsa_cmp_w1, nsa_cmp_w2, nsa_cmp_pos,
           diff_lambda, diff_subln_g, w_out, ffn_w_gate, ffn_w_up, ffn_w_down, final_norm_g):
    nb, seq, d = x.shape
    depth = ada_w.shape[0]
    m = nb * seq
    mods = _modulation(c, ada_w, ada_b)
    w_p = _relayout_w_in(w_in, _proj_colmap())
    w_pf = w_p[:, :, P_FGATE:P_FGATE + LANES] + w_p[:, :, P_NGATE:P_NGATE + LANES]
    w_o = _gather_axis(w_out, _out_rowmap(), 1).astype(BF16)
    w_g, w_u, w_d = ffn_w_gate.astype(BF16), ffn_w_up.astype(BF16), ffn_w_down.astype(BF16)
    h = x.reshape(m, d)
    for layer in range(depth):
        sh1, sc1, g1, sh2, sc2, g2 = [mods[layer, :, i * d:(i + 1) * d].reshape(nb, 1, d) for i in range(6)]
        lam_init = 0.8 - 0.6 * math.exp(-0.3 * layer)
        pb, pf = _norm_proj(h, norm_mix_g[layer], sc1, sh1, w_p, layer, w_pf[layer], seq)
        parts = _mixers(pb.reshape(nb, seq, N_PB), pf.reshape(nb, seq, LANES), fox_f_bias[layer],
                        nsa_cmp_w1[layer], nsa_cmp_w2[layer], nsa_cmp_pos[layer],
                        diff_lambda[layer], diff_subln_g[layer], lam_init)
        h = _matmul_gated_residual([p.reshape(m, p.shape[-1]) for p in parts], w_o, layer, h, g1, seq,
                                   name="out_proj_res")
        act = _norm_ffn_up(h, norm_ffn_g[layer], sc2, sh2, w_g, w_u, layer, seq)
        h = _matmul_gated_residual([act], w_d, layer, h, g2, seq, name="ffn_down_res")
    return _final_norm(h, final_norm_g).reshape(nb, seq, d)
```

```python
import functools
import math

import numpy as np
import jax
import jax.numpy as jnp
from jax import lax
from jax.experimental import pallas as pl
from jax.experimental.pallas import tpu as pltpu

F32 = jnp.float32
BF16 = jnp.bfloat16

LANES = 128
HEAD_DIM = 64
HALF = HEAD_DIM
RMS_EPS = 1e-6
NEG = -1e30
PICKED = -3e38
LOG2E = math.log2(math.e)
Q_SCALE = HEAD_DIM ** -0.5 * LOG2E

FOX_HEADS = 7
NSA_HEADS = 8
NSA_GROUPS = 2
NSA_HPG = NSA_HEADS // NSA_GROUPS
NSA_CMP_LEN = 32
NSA_CMP_STRIDE = 16
NSA_SEL_BLOCK = 64
NSA_TOP_N = 16
NSA_WINDOW = 512
NSA_FORCE_SCORE = 1e4
DIL_CONFIGS = ((128, 1), (512, 4), (2048, 16))
DIL_HPG = 3
DIL_HEADS = DIL_HPG * len(DIL_CONFIGS)
DIFF_HEADS = 4
ALIBI_MAX_EXP = 8.0

_SPLITS = (
    ("fox_q", 448), ("fox_k", 448), ("fox_v", 448), ("fox_f", 7),
    ("nsa_q", 512), ("nsa_cmp_k", 128), ("nsa_cmp_v", 128), ("nsa_slc_k", 128),
    ("nsa_slc_v", 128), ("nsa_win_k", 128), ("nsa_win_v", 128), ("nsa_gate", 24),
    ("dil_q", 576), ("dil_k", 576), ("dil_v", 576),
    ("diff_q", 512), ("diff_k", 512), ("diff_v", 512),
)
_SRC = {}
_o = 0
for _n, _w in _SPLITS:
    _SRC[_n] = _o
    _o += _w
N_IN = _o

DIL_GW = 256
DIL_W = 3 * DIL_GW
P_FOXQ, P_FOXK, P_FOXV = 0, 512, 1024
P_NSAQ = 1536
P_NSAKV = 2048
P_DILQ, P_DILK, P_DILV = 2816, 3584, 4352
P_DIFQ, P_DIFK, P_DIFV = 5120, 5632, 6144
N_PB = 6656
GATE_COL = 8
RELAYOUT_TILE = 512
P_FGATE, P_NGATE = N_PB, N_PB + RELAYOUT_TILE
N_WP = N_PB + 2 * RELAYOUT_TILE


def _proj_colmap():
    pb = -np.ones(N_WP, np.int64)
    ar = np.arange
    for name, dst in (("fox_q", P_FOXQ), ("fox_k", P_FOXK), ("fox_v", P_FOXV)):
        pb[dst:dst + 448] = _SRC[name] + ar(448)
    for j in range(NSA_HPG):
        for g in range(NSA_GROUPS):
            d = P_NSAQ + j * LANES + g * HALF
            pb[d:d + HALF] = _SRC["nsa_q"] + (g * NSA_HPG + j) * HEAD_DIM + ar(HALF)
    for i, name in enumerate(("nsa_cmp_k", "nsa_cmp_v", "nsa_slc_k", "nsa_slc_v", "nsa_win_k", "nsa_win_v")):
        pb[P_NSAKV + i * LANES:P_NSAKV + (i + 1) * LANES] = _SRC[name] + ar(LANES)
    for name, dst in (("dil_q", P_DILQ), ("dil_k", P_DILK), ("dil_v", P_DILV)):
        for g in range(len(DIL_CONFIGS)):
            for jj in range(DIL_HPG):
                d = dst + (2 * g + jj // 2) * LANES + (jj % 2) * HALF
                pb[d:d + HALF] = _SRC[name] + (g * DIL_HPG + jj) * HEAD_DIM + ar(HALF)
    for name, dst in (("diff_q", P_DIFQ), ("diff_k", P_DIFK), ("diff_v", P_DIFV)):
        pb[dst:dst + 512] = _SRC[name] + ar(512)
    pb[P_FGATE:P_FGATE + FOX_HEADS] = _SRC["fox_f"] + ar(FOX_HEADS)
    pb[P_NGATE + GATE_COL:P_NGATE + GATE_COL + 24] = _SRC["nsa_gate"] + ar(24)
    return pb


O_DIL, O_FOX, O_NSA, O_DIF = 0, 768, 1280, 1792
N_O = 2304


def _out_rowmap():
    m = -np.ones(N_O, np.int64)
    ar = np.arange
    m[O_FOX:O_FOX + 448] = ar(448)
    for j in range(NSA_HPG):
        for g in range(NSA_GROUPS):
            d = O_NSA + j * LANES + g * HALF
            m[d:d + HALF] = 448 + (g * NSA_HPG + j) * HEAD_DIM + ar(HALF)
    for g in range(len(DIL_CONFIGS)):
        for jj in range(DIL_HPG):
            d = O_DIL + (2 * g + jj // 2) * LANES + (jj % 2) * HALF
            m[d:d + HALF] = 960 + (g * DIL_HPG + jj) * HEAD_DIM + ar(HALF)
    m[O_DIF:O_DIF + 512] = 1536 + ar(512)
    return m


def _gather_axis(x, idx, axis):
    pieces = []
    i = 0
    n = len(idx)
    while i < n:
        j = i + 1
        if idx[i] < 0:
            while j < n and idx[j] < 0:
                j += 1
            shape = list(x.shape)
            shape[axis] = j - i
            pieces.append(jnp.zeros(shape, x.dtype))
        else:
            while j < n and idx[j] == idx[j - 1] + 1:
                j += 1
            pieces.append(lax.slice_in_dim(x, int(idx[i]), int(idx[j - 1]) + 1, axis=axis))
        i = j
    return jnp.concatenate(pieces, axis=axis)


def _cparams(sem, vmem_mb=48):
    return pltpu.CompilerParams(dimension_semantics=sem, vmem_limit_bytes=vmem_mb * 1024 * 1024)


def _alibi_slope(k, n_heads):
    return float(2.0 ** (-ALIBI_MAX_EXP * (k + 1) / n_heads)) * LOG2E


def _src_col_scale(n_pad):
    sc = np.ones(n_pad, np.float32)
    for name, width in (("fox_q", 448), ("nsa_q", 512), ("dil_q", 576), ("diff_q", 512)):
        sc[_SRC[name]:_SRC[name] + width] = Q_SCALE
    return sc


def _relayout_plan(colmap, n_src, tile):
    n_tiles = len(colmap) // tile
    last = (n_src - 1) // tile
    win = np.zeros(n_tiles, np.int32)
    local = -np.ones((n_tiles, 1, tile), np.int32)
    for t in range(n_tiles):
        src = colmap[t * tile:(t + 1) * tile]
        used = src[src >= 0]
        a = min(int(used.min()) // tile, max(last - 1, 0))
        assert int(used.max()) < (a + 2) * tile, "tile sources exceed the two-block window"
        win[t] = a
        local[t, 0] = np.where(src >= 0, src - a * tile, -1)
    return win, local


def _relayout_kernel(win_ref, idx_ref, sa_ref, sb_ref, wa_ref, wb_ref, o_ref, *, n_src, tile):
    a = win_ref[pl.program_id(1)]
    idx = idx_ref[0]
    row = lax.broadcasted_iota(jnp.int32, (tile, tile), 0)
    col = lax.broadcasted_iota(jnp.int32, wa_ref.shape[1:], 1)
    acc = None
    for part, (w_ref, s_ref) in enumerate(((wa_ref, sa_ref), (wb_ref, sb_ref))):
        sel = jnp.where(row + part * tile == idx, 1.0, 0.0).astype(BF16)
        w = (jnp.where(col + (a + part) * tile < n_src, w_ref[0], 0.0) * s_ref[0]).astype(BF16)
        d = jnp.dot(w, sel, preferred_element_type=F32)
        acc = d if acc is None else acc + d
    o_ref[0] = acc.astype(BF16)


def _relayout_w_in(w_in, colmap, tile=RELAYOUT_TILE):
    depth, d, n_src = w_in.shape
    win, local = _relayout_plan(colmap, n_src, tile)
    n_tiles = len(win)
    n_blk = -(-n_src // tile)
    scale = jnp.asarray(_src_col_scale(n_blk * tile).reshape(n_blk, 1, tile))
    grid_spec = pltpu.PrefetchScalarGridSpec(
        num_scalar_prefetch=1,
        grid=(depth, n_tiles),
        in_specs=[
            pl.BlockSpec((1, 1, tile), lambda l, t, win: (t, 0, 0)),
            pl.BlockSpec((1, 1, tile), lambda l, t, win: (win[t], 0, 0)),
            pl.BlockSpec((1, 1, tile), lambda l, t, win: (win[t] + 1, 0, 0)),
            pl.BlockSpec((1, d, tile), lambda l, t, win: (l, 0, win[t])),
            pl.BlockSpec((1, d, tile), lambda l, t, win: (l, 0, win[t] + 1)),
        ],
        out_specs=pl.BlockSpec((1, d, tile), lambda l, t, win: (l, 0, t)),
    )
    return pl.pallas_call(
        functools.partial(_relayout_kernel, n_src=n_src, tile=tile),
        grid_spec=grid_spec,
        out_shape=jax.ShapeDtypeStruct((depth, d, n_tiles * tile), BF16),
        compiler_params=_cparams(("parallel", "arbitrary")),
        name="w_in_relayout",
    )(jnp.asarray(win), jnp.asarray(local), scale, scale, w_in, w_in)


def _mod_kernel(c_ref, w_ref, b_ref, o_ref):
    nb = c_ref.shape[0]
    tn = w_ref.shape[2]
    for b in range(nb):
        cb = c_ref[b]
        ca = cb * jax.nn.sigmoid(cb)
        for j in range(tn // LANES):
            sl = slice(j * LANES, (j + 1) * LANES)
            w = w_ref[0, :, sl]
            o_ref[0, b:b + 1, sl] = jnp.sum(w * ca, axis=0, keepdims=True) + b_ref[0, :, sl]


def _modulation(c, ada_w, ada_b, tn=512):
    depth, d, n = ada_w.shape
    nb = c.shape[0]
    c_rep = jnp.broadcast_to(c[:, :, None], (nb, d, LANES))
    return pl.pallas_call(
        _mod_kernel,
        grid=(depth, n // tn),
        in_specs=[
            pl.BlockSpec((nb, d, LANES), lambda l, j: (0, 0, 0)),
            pl.BlockSpec((1, d, tn), lambda l, j: (l, 0, j)),
            pl.BlockSpec((1, 1, tn), lambda l, j: (l, 0, j)),
        ],
        out_specs=pl.BlockSpec((1, nb, tn), lambda l, j: (l, 0, j)),
        out_shape=jax.ShapeDtypeStruct((depth, nb, n), F32),
        compiler_params=_cparams(("parallel", "parallel")),
        name="adaln_mod",
    )(c_rep, ada_w, ada_b.reshape(depth, 1, n))


def _norm_mod(h_ref, g_ref, sc_ref, sh_ref):
    x = h_ref[...]
    ms = jnp.mean(x * x, axis=-1, keepdims=True)
    y = x * lax.rsqrt(ms + RMS_EPS) * g_ref[...]
    return (y * (1.0 + sc_ref[0]) + sh_ref[0]).astype(BF16)


def _proj_kernel(h_ref, g_ref, sc_ref, sh_ref, w_ref, wf_ref, pb_ref, pf_ref, u_ref):
    @pl.when(pl.program_id(1) == 0)
    def _():
        u = _norm_mod(h_ref, g_ref, sc_ref, sh_ref)
        u_ref[...] = u
        pf_ref[...] = jnp.dot(u, wf_ref[...], preferred_element_type=F32)

    pb_ref[...] = jnp.dot(u_ref[...], w_ref[0], preferred_element_type=F32).astype(BF16)


def _norm_proj(h, gain, sc, sh, w_all, layer, wf, seq, tm=1024, tn=512):
    m, d = h.shape
    per = seq // tm
    return pl.pallas_call(
        _proj_kernel,
        grid=(m // tm, N_PB // tn),
        in_specs=[
            pl.BlockSpec((tm, d), lambda i, j: (i, 0)),
            pl.BlockSpec((1, d), lambda i, j: (0, 0)),
            pl.BlockSpec((1, 1, d), lambda i, j: (i // per, 0, 0)),
            pl.BlockSpec((1, 1, d), lambda i, j: (i // per, 0, 0)),
            pl.BlockSpec((1, d, tn), lambda i, j: (layer, 0, j)),
            pl.BlockSpec((d, LANES), lambda i, j: (0, 0)),
        ],
        out_specs=[
            pl.BlockSpec((tm, tn), lambda i, j: (i, j)),
            pl.BlockSpec((tm, LANES), lambda i, j: (i, 0)),
        ],
        out_shape=[jax.ShapeDtypeStruct((m, N_PB), BF16), jax.ShapeDtypeStruct((m, LANES), F32)],
        scratch_shapes=[pltpu.VMEM((tm, d), BF16)],
        compiler_params=_cparams(("parallel", "arbitrary")),
        name="norm_in_proj",
    )(h, gain.reshape(1, d), sc, sh, w_all, wf)


def _ffn_up_kernel(h_ref, g_ref, sc_ref, sh_ref, wg_ref, wu_ref, a_ref, u_ref):
    @pl.when(pl.program_id(1) == 0)
    def _():
        u_ref[...] = _norm_mod(h_ref, g_ref, sc_ref, sh_ref)

    u = u_ref[...]
    gate = jnp.dot(u, wg_ref[0], preferred_element_type=F32)
    up = jnp.dot(u, wu_ref[0], preferred_element_type=F32)
    a_ref[...] = (gate * jax.nn.sigmoid(gate) * up).astype(BF16)


def _norm_ffn_up(h, gain, sc, sh, wg, wu, layer, seq, tm=1024, tn=512):
    m, d = h.shape
    n = wg.shape[2]
    per = seq // tm
    return pl.pallas_call(
        _ffn_up_kernel,
        grid=(m // tm, n // tn),
        in_specs=[
            pl.BlockSpec((tm, d), lambda i, j: (i, 0)),
            pl.BlockSpec((1, d), lambda i, j: (0, 0)),
            pl.BlockSpec((1, 1, d), lambda i, j: (i // per, 0, 0)),
            pl.BlockSpec((1, 1, d), lambda i, j: (i // per, 0, 0)),
            pl.BlockSpec((1, d, tn), lambda i, j: (layer, 0, j)),
            pl.BlockSpec((1, d, tn), lambda i, j: (layer, 0, j)),
        ],
        out_specs=pl.BlockSpec((tm, tn), lambda i, j: (i, j)),
        out_shape=jax.ShapeDtypeStruct((m, n), BF16),
        scratch_shapes=[pltpu.VMEM((tm, d), BF16)],
        compiler_params=_cparams(("parallel", "arbitrary")),
        name="norm_ffn_up",
    )(h, gain.reshape(1, d), sc, sh, wg, wu)


def _mm_res_kernel(*refs, widths):
    n = len(widths)
    a_refs, (w_ref, h_ref, g_ref, o_ref) = refs[:n], refs[n:]
    acc = None
    off = 0
    for a_ref, width in zip(a_refs, widths):
        d = jnp.dot(a_ref[...], w_ref[0, off:off + width, :], preferred_element_type=F32)
        acc = d if acc is None else acc + d
        off += width
    o_ref[...] = h_ref[...] + g_ref[0] * acc


def _matmul_gated_residual(a_parts, w, layer, h, gate, seq, tm=1024, tn=512, name="mm_res"):
    m = h.shape[0]
    widths = tuple(a.shape[1] for a in a_parts)
    k, n = w.shape[1], w.shape[2]
    assert sum(widths) == k
    per = seq // tm
    block_bytes = tm * k * 2 + k * tn * 2 + 2 * tm * tn * 4
    vmem_mb = (2 * block_bytes + 4 * tm * tn * 4) // (1024 * 1024) + 1
    return pl.pallas_call(
        functools.partial(_mm_res_kernel, widths=widths),
        grid=(m // tm, n // tn),
        in_specs=[pl.BlockSpec((tm, wd), lambda i, j: (i, 0)) for wd in widths] + [
            pl.BlockSpec((1, k, tn), lambda i, j: (layer, 0, j)),
            pl.BlockSpec((tm, tn), lambda i, j: (i, j)),
            pl.BlockSpec((1, 1, tn), lambda i, j: (i // per, 0, j)),
        ],
        out_specs=pl.BlockSpec((tm, tn), lambda i, j: (i, j)),
        out_shape=jax.ShapeDtypeStruct((m, n), F32),
        compiler_params=_cparams(("parallel", "arbitrary"), vmem_mb=vmem_mb),
        name=name,
    )(*a_parts, w, h, gate)


def _final_norm_kernel(h_ref, g_ref, o_ref):
    x = h_ref[...]
    ms = jnp.mean(x * x, axis=-1, keepdims=True)
    o_ref[...] = x * lax.rsqrt(ms + RMS_EPS) * g_ref[...]


def _final_norm(h, gain, tm=512):
    m, d = h.shape
    return pl.pallas_call(
        _final_norm_kernel,
        grid=(m // tm,),
        in_specs=[pl.BlockSpec((tm, d), lambda i: (i, 0)), pl.BlockSpec((1, d), lambda i: (0, 0))],
        out_specs=pl.BlockSpec((tm, d), lambda i: (i, 0)),
        out_shape=jax.ShapeDtypeStruct((m, d), F32),
        compiler_params=_cparams(("parallel",)),
        name="final_norm",
    )(h, gain.reshape(1, d))


def _half_mask(shape, half):
    lane = lax.broadcasted_iota(jnp.int32, shape, 1)
    return (lane < HALF) if half == 0 else (lane >= HALF)


def _pick_half(qp, half):
    return jnp.where(_half_mask(qp.shape, half), qp, jnp.zeros_like(qp))


def _lane_tile(x, n):
    return x if n == 1 else jnp.concatenate([x] * n, axis=1)


def _qk(qh, k):
    return lax.dot_general(qh, k, (((1,), (1,)), ((), ())), preferred_element_type=F32)


def _merge_halves(lo, hi):
    if hi is None:
        hi = jnp.zeros_like(lo)
    return jnp.where(_half_mask(lo.shape, 0), lo, hi)


def _flash_scratch(n_heads, tq):
    return [pltpu.VMEM((n_heads, tq, LANES), F32), pltpu.VMEM((n_heads, tq, LANES), F32),
            pltpu.VMEM((n_heads, tq, LANES), F32)]


def _flash_reset(m_ref, l_ref, acc_ref):
    m_ref[...] = jnp.full(m_ref.shape, NEG, F32)
    l_ref[...] = jnp.zeros(l_ref.shape, F32)
    acc_ref[...] = jnp.zeros(acc_ref.shape, F32)


def _flash_update(h, s, valid, v, m_ref, l_ref, acc_ref):
    m_old = m_ref[h]
    m_new = jnp.maximum(m_old, jnp.max(s, axis=1, keepdims=True))
    alpha = jnp.exp2(m_old - m_new)
    p = jnp.exp2(s - _lane_tile(m_new, s.shape[1] // LANES))
    if valid is not None:
        p = jnp.where(valid, p, 0.0)
    l_ref[h] = alpha * l_ref[h] + jnp.sum(p, axis=1, keepdims=True)
    acc_ref[h] = alpha * acc_ref[h] + jnp.dot(p.astype(BF16), v, preferred_element_type=F32)
    m_ref[h] = m_new


def _softmax_pv(s, v):
    m = jnp.max(s, axis=1, keepdims=True)
    p = jnp.exp2(s - m)
    l = jnp.sum(p, axis=1, keepdims=True)
    return jnp.dot(p.astype(BF16), v, preferred_element_type=F32) / l, m + jnp.log2(l)


def _causal_steps(qi, t, step):
    def wide(k2, c):
        step(pl.multiple_of(k2 * (2 * t), 2 * t), 2 * t, False)
        return c

    lax.fori_loop(0, lax.shift_right_logical(qi, 1), wide, 0)

    @pl.when((qi & 1) == 1)
    def _():
        step(pl.multiple_of((qi - 1) * t, t), t, False)

    step(pl.multiple_of(qi * t, t), t, True)


def _window_start(qs, back, width, seq, align):
    return pl.multiple_of(jnp.clip(qs - back, 0, seq - width), align)


def _cumsum_kernel(pf_ref, fb_ref, o_ref, *, chunk):
    seq = pf_ref.shape[1]
    r = lax.broadcasted_iota(jnp.int32, (chunk, chunk), 0)
    c = lax.broadcasted_iota(jnp.int32, (chunk, chunk), 1)
    tri = jnp.where(r >= c, 1.0, 0.0).astype(BF16)

    def body(i, carry):
        st = pl.multiple_of(i * chunk, chunk)
        x = pf_ref[0, pl.ds(st, chunk), :] + fb_ref[...]
        ls = jnp.minimum(x, 0.0) - jnp.log1p(jnp.exp(-jnp.abs(x)))
        hi = ls.astype(BF16)
        r1 = ls - hi.astype(F32)
        mid = r1.astype(BF16)
        lo = (r1 - mid.astype(F32)).astype(BF16)
        cs = (jnp.dot(tri, hi, preferred_element_type=F32)
              + jnp.dot(tri, mid, preferred_element_type=F32)
              + jnp.dot(tri, lo, preferred_element_type=F32))
        out = cs + carry
        o_ref[0, pl.ds(st, chunk), :] = out * LOG2E
        return out[chunk - 1:chunk, :]

    lax.fori_loop(0, seq // chunk, body, jnp.zeros((1, LANES), F32))


def _forget_cumsum(pf, f_bias, chunk=256):
    nb, seq, _ = pf.shape
    fb = jnp.zeros((1, LANES), F32).at[0, :FOX_HEADS].set(f_bias.astype(F32))
    return pl.pallas_call(
        functools.partial(_cumsum_kernel, chunk=chunk),
        grid=(nb,),
        in_specs=[pl.BlockSpec((1, seq, LANES), lambda b: (b, 0, 0)), pl.BlockSpec((1, LANES), lambda b: (0, 0))],
        out_specs=pl.BlockSpec((1, seq, LANES), lambda b: (b, 0, 0)),
        out_shape=jax.ShapeDtypeStruct((nb, seq, LANES), F32),
        compiler_params=_cparams(("parallel",)),
        name="forget_cumsum",
    )(pf, fb)


def _fox_kernel(q_ref, k_ref, v_ref, cc_ref, cr_ref, o_ref, qs_ref, cq_ref, m_ref, l_ref, acc_ref, *, t):
    qi = pl.program_id(1)
    _flash_reset(m_ref, l_ref, acc_ref)
    for h in range(FOX_HEADS):
        qs_ref[h] = _pick_half(q_ref[0, :, (h // 2) * LANES:(h // 2 + 1) * LANES], h % 2)
        cq_ref[h] = jnp.broadcast_to(cc_ref[0, :, h:h + 1], (t, LANES))

    def step(ks, width, masked):
        if masked:
            causal = (lax.broadcasted_iota(jnp.int32, (t, t), 0) >= lax.broadcasted_iota(jnp.int32, (t, t), 1))
        for h in range(FOX_HEADS):
            sl = slice((h // 2) * LANES, (h // 2 + 1) * LANES)
            k = k_ref[0, pl.ds(ks, width), sl]
            v = v_ref[0, pl.ds(ks, width), sl]
            s = ((_qk(qs_ref[h], k) + _lane_tile(cq_ref[h], width // LANES))
                 - cr_ref[0, h:h + 1, pl.ds(ks, width)])
            if masked:
                s = jnp.where(causal, s, NEG)
            _flash_update(h, s, None, v, m_ref, l_ref, acc_ref)

    _causal_steps(qi, t, step)
    for p in range(4):
        lo = acc_ref[2 * p] / l_ref[2 * p]
        hi = acc_ref[2 * p + 1] / l_ref[2 * p + 1] if 2 * p + 1 < FOX_HEADS else None
        o_ref[0, :, p * LANES:(p + 1) * LANES] = _merge_halves(lo, hi).astype(BF16)


def _fox_attention(pb, cum, cumrow, t=512):
    nb, seq, _ = pb.shape
    t = min(t, seq)
    w = 4 * LANES
    return pl.pallas_call(
        functools.partial(_fox_kernel, t=t),
        grid=(nb, seq // t),
        in_specs=[
            pl.BlockSpec((1, t, w), lambda b, i: (b, i, P_FOXQ // w)),
            pl.BlockSpec((1, seq, w), lambda b, i: (b, 0, P_FOXK // w)),
            pl.BlockSpec((1, seq, w), lambda b, i: (b, 0, P_FOXV // w)),
            pl.BlockSpec((1, t, LANES), lambda b, i: (b, i, 0)),
            pl.BlockSpec((1, 8, seq), lambda b, i: (b, 0, 0)),
        ],
        out_specs=pl.BlockSpec((1, t, w), lambda b, i: (b, i, 0)),
        out_shape=jax.ShapeDtypeStruct((nb, seq, w), BF16),
        scratch_shapes=([pltpu.VMEM((FOX_HEADS, t, LANES), BF16), pltpu.VMEM((FOX_HEADS, t, LANES), F32)]
                        + _flash_scratch(FOX_HEADS, t)),
        compiler_params=_cparams(("parallel", "arbitrary")),
        name="fox_attention",
    )(pb, pb, pb, cum, cumrow)


def _diff_kernel(q_ref, k_ref, v_ref, lam_ref, g_ref, o_ref, qs_ref, m_ref, l_ref, acc_ref, *, t, lam_init):
    qi = pl.program_id(1)
    n_slots = 2 * DIFF_HEADS
    _flash_reset(m_ref, l_ref, acc_ref)
    for n in range(n_slots):
        qs_ref[n] = _pick_half(q_ref[0, :, (n // 2) * LANES:(n // 2 + 1) * LANES], n % 2)
    tpos = (qi * t + lax.broadcasted_iota(jnp.int32, (t, LANES), 0)).astype(F32)

    def step(ks, width, masked):
        kpos = (ks + lax.broadcasted_iota(jnp.int32, (1, width), 1)).astype(F32)
        if masked:
            causal = (lax.broadcasted_iota(jnp.int32, (t, t), 0) >= lax.broadcasted_iota(jnp.int32, (t, t), 1))
        for n in range(n_slots):
            hd = n // 2
            sl = slice(hd * LANES, (hd + 1) * LANES)
            slope = _alibi_slope(hd, DIFF_HEADS)
            k = k_ref[0, pl.ds(ks, width), sl]
            v = v_ref[0, pl.ds(ks, width), sl]
            s = (_qk(qs_ref[n], k) + _lane_tile(tpos * (-slope), width // LANES)) + kpos * slope
            if masked:
                s = jnp.where(causal, s, NEG)
            _flash_update(n, s, None, v, m_ref, l_ref, acc_ref)

    _causal_steps(qi, t, step)
    lv = lam_ref[...]
    lam = (jnp.exp(jnp.sum(lv[0:1] * lv[1:2], axis=1, keepdims=True))
           - jnp.exp(jnp.sum(lv[2:3] * lv[3:4], axis=1, keepdims=True)) + lam_init)
    for hd in range(DIFF_HEADS):
        o = acc_ref[2 * hd] / l_ref[2 * hd] - lam * (acc_ref[2 * hd + 1] / l_ref[2 * hd + 1])
        ms = jnp.mean(o * o, axis=-1, keepdims=True)
        y = o * lax.rsqrt(ms + RMS_EPS) * g_ref[...]
        o_ref[0, :, hd * LANES:(hd + 1) * LANES] = (y * (1.0 - lam_init)).astype(BF16)


def _diff_attention(pb, lam_vecs, subln_g, lam_init, t=512):
    nb, seq, _ = pb.shape
    t = min(t, seq)
    w = DIFF_HEADS * LANES
    return pl.pallas_call(
        functools.partial(_diff_kernel, t=t, lam_init=lam_init),
        grid=(nb, seq // t),
        in_specs=[
            pl.BlockSpec((1, t, w), lambda b, i: (b, i, P_DIFQ // w)),
            pl.BlockSpec((1, seq, w), lambda b, i: (b, 0, P_DIFK // w)),
            pl.BlockSpec((1, seq, w), lambda b, i: (b, 0, P_DIFV // w)),
            pl.BlockSpec((4, HEAD_DIM), lambda b, i: (0, 0)),
            pl.BlockSpec((1, LANES), lambda b, i: (0, 0)),
        ],
        out_specs=pl.BlockSpec((1, t, w), lambda b, i: (b, i, 0)),
        out_shape=jax.ShapeDtypeStruct((nb, seq, w), BF16),
        scratch_shapes=[pltpu.VMEM((2 * DIFF_HEADS, t, LANES), BF16)] + _flash_scratch(2 * DIFF_HEADS, t),
        compiler_params=_cparams(("parallel", "arbitrary")),
        name="diff_attention",
    )(pb, pb, pb, lam_vecs.astype(F32), subln_g.astype(F32).reshape(1, LANES))


DIL_SUBWINDOW = 128


def _dil_group_kernel(q_ref, k_ref, v_ref, o_ref, lse_ref, *, tq, sub_len, dilation, head0):
    qs = pl.program_id(2) * tq
    width = min(2 * tq, sub_len)
    start = _window_start(qs, tq, width, sub_len, tq)
    dist = ((qs - start) + lax.broadcasted_iota(jnp.int32, (tq, width), 0)
            - lax.broadcasted_iota(jnp.int32, (tq, width), 1))
    valid = (dist | (DIL_SUBWINDOW - dist)) >= 0
    distf = dist.astype(F32)
    lane = lax.broadcasted_iota(jnp.int32, (tq, LANES), 1)
    lse_blk = jnp.zeros((tq, LANES), F32)
    outs = []
    for jj in range(DIL_HPG):
        sl = slice((jj // 2) * LANES, (jj // 2 + 1) * LANES)
        slope = _alibi_slope(head0 + jj, DIL_HEADS) * dilation
        k = k_ref[0, pl.ds(start, width), sl]
        v = v_ref[0, pl.ds(start, width), sl]
        s = jnp.where(valid, _qk(_pick_half(q_ref[0, :, sl], jj % 2), k) - slope * distf, NEG)
        o, lse = _softmax_pv(s, v)
        outs.append(o)
        lse_blk = jnp.where(lane == jj, lse, lse_blk)
    o_ref[0, :, 0:LANES] = _merge_halves(outs[0], outs[1])
    o_ref[0, :, LANES:2 * LANES] = _merge_halves(outs[2], None)
    lse_ref[0] = lse_blk


def _dil_group(pb, g, tq=256):
    nb, seq, n_pb = pb.shape
    window, dilation = DIL_CONFIGS[g]
    assert window == DIL_SUBWINDOW * dilation
    sub_len = seq // dilation
    tq = min(tq, sub_len)
    gw = DIL_GW
    if dilation == 1:
        views = (pb, pb, pb)
        cols = [base // gw + g for base in (P_DILQ, P_DILK, P_DILV)]
        per_tok = 0
    else:
        views = tuple(lax.slice_in_dim(pb, base + g * gw, base + (g + 1) * gw, axis=2)
                      .reshape(nb, sub_len, dilation * gw) for base in (P_DILQ, P_DILK, P_DILV))
        cols = [0, 0, 0]
        per_tok = 1
    o, lse = pl.pallas_call(
        functools.partial(_dil_group_kernel, tq=tq, sub_len=sub_len, dilation=dilation, head0=g * DIL_HPG),
        grid=(nb, dilation, sub_len // tq),
        in_specs=[
            pl.BlockSpec((1, tq, gw), lambda b, r, i: (b, i, r * per_tok + cols[0])),
            pl.BlockSpec((1, sub_len, gw), lambda b, r, i: (b, 0, r * per_tok + cols[1])),
            pl.BlockSpec((1, sub_len, gw), lambda b, r, i: (b, 0, r * per_tok + cols[2])),
        ],
        out_specs=[pl.BlockSpec((1, tq, gw), lambda b, r, i: (b, i, r)),
                   pl.BlockSpec((1, tq, LANES), lambda b, r, i: (b, i, r))],
        out_shape=[jax.ShapeDtypeStruct((nb, sub_len, dilation * gw), F32),
                   jax.ShapeDtypeStruct((nb, sub_len, dilation * LANES), F32)],
        compiler_params=_cparams(("parallel", "parallel", "arbitrary")),
        name=f"dil_group{g}",
    )(*views)
    return o.reshape(nb, seq, gw), lse.reshape(nb, seq, LANES)


def _dil_mix_kernel(*refs):
    ng = len(DIL_CONFIGS)
    o_refs, l_refs, (x_ref, out_ref) = refs[:ng], refs[ng:2 * ng], refs[2 * ng:]
    lses = [r[0] for r in l_refs]
    mx = lses[0]
    for g in range(1, ng):
        mx = jnp.maximum(mx, lses[g])
    es = [jnp.exp2(l - mx) for l in lses]
    tot = es[0]
    for g in range(1, ng):
        tot = tot + es[g]
    for g in range(ng):
        w = es[g] / tot
        hi = w.astype(BF16)
        lo = (w - hi.astype(F32)).astype(BF16)
        wx = (jnp.dot(hi, x_ref[...], preferred_element_type=F32)
              + jnp.dot(lo, x_ref[...], preferred_element_type=F32))
        out_ref[0, :, g * DIL_GW:(g + 1) * DIL_GW] = (o_refs[g][0] * wx).astype(BF16)


def _dil_attention(pb, tm=512):
    nb, seq, _ = pb.shape
    ng = len(DIL_CONFIGS)
    outs, lses = zip(*[_dil_group(pb, g) for g in range(ng)])
    spread = np.zeros((LANES, DIL_GW), np.float32)
    for jj in range(DIL_HPG):
        spread[jj, jj * HALF:(jj + 1) * HALF] = 1.0
    tm = min(tm, seq)
    return pl.pallas_call(
        _dil_mix_kernel,
        grid=(nb, seq // tm),
        in_specs=([pl.BlockSpec((1, tm, DIL_GW), lambda b, i: (b, i, 0))] * ng
                  + [pl.BlockSpec((1, tm, LANES), lambda b, i: (b, i, 0))] * ng
                  + [pl.BlockSpec((LANES, DIL_GW), lambda b, i: (0, 0))]),
        out_specs=pl.BlockSpec((1, tm, DIL_W), lambda b, i: (b, i, 0)),
        out_shape=jax.ShapeDtypeStruct((nb, seq, DIL_W), BF16),
        compiler_params=_cparams(("parallel", "parallel")),
        name="dil_mix",
    )(*outs, *lses, jnp.asarray(spread, BF16))


def _gelu_tanh(x):
    return 0.5 * x * (1.0 + jnp.tanh(math.sqrt(2.0 / math.pi) * (x + 0.044715 * (x * x * x))))


def _compress_kernel(rk_ref, rv_ref, w1_ref, w2_ref, pos_ref, kc_ref, vc_ref):
    nc = rk_ref.shape[1]
    rowi = lax.broadcasted_iota(jnp.int32, (nc, LANES), 0)
    for kv, (r_ref, o_ref) in enumerate(((rk_ref, kc_ref), (rv_ref, vc_ref))):
        r = r_ref[0]
        a = jnp.dot(r, w1_ref[kv, 0], preferred_element_type=F32)
        b = jnp.dot(r, w1_ref[kv, 1], preferred_element_type=F32)
        pt = (jnp.dot(pos_ref[kv, 0], w1_ref[kv, 0], preferred_element_type=F32)
              + jnp.dot(pos_ref[kv, 1], w1_ref[kv, 1], preferred_element_type=F32))[0:1]
        pre = a + pltpu.roll(b, nc - 1, 0) + pt
        out = jnp.dot(_gelu_tanh(pre).astype(BF16), w2_ref[kv], preferred_element_type=F32)
        o_ref[0] = jnp.where(rowi < nc - 1, out, 0.0).astype(BF16)


def _compress_weights(cmp_w1, cmp_w2, cmp_pos):
    half = NSA_CMP_LEN // 2
    w1 = cmp_w1.reshape(2, 2, half, HEAD_DIM, HEAD_DIM)
    z = jnp.zeros_like(w1)
    w1x = jnp.concatenate([jnp.concatenate([w1, z], -1), jnp.concatenate([z, w1], -1)], -2)
    w1x = w1x.reshape(2, 2, half * LANES, LANES).astype(BF16)
    z2 = jnp.zeros_like(cmp_w2)
    w2x = jnp.concatenate([jnp.concatenate([cmp_w2, z2], -1), jnp.concatenate([z2, cmp_w2], -1)], -2).astype(BF16)
    pos = cmp_pos.reshape(2, 2, half, 1, HEAD_DIM)
    posx = jnp.broadcast_to(pos, (2, 2, half, NSA_GROUPS, HEAD_DIM)).reshape(2, 2, 1, half * LANES)
    posx = jnp.broadcast_to(posx, (2, 2, 8, half * LANES)).astype(BF16)
    return w1x, w2x, posx


def _nsa_compress(rk, rv, w1x, w2x, posx):
    nb, nc, width = rk.shape
    full = lambda a: pl.BlockSpec(a.shape, lambda b: (0,) * a.ndim)
    return pl.pallas_call(
        _compress_kernel,
        grid=(nb,),
        in_specs=[pl.BlockSpec((1, nc, width), lambda b: (b, 0, 0)), pl.BlockSpec((1, nc, width), lambda b: (b, 0, 0)),
                  full(w1x), full(w2x), full(posx)],
        out_specs=[pl.BlockSpec((1, nc, LANES), lambda b: (b, 0, 0))] * 2,
        out_shape=[jax.ShapeDtypeStruct((nb, nc, LANES), BF16)] * 2,
        compiler_params=_cparams(("parallel",)),
        name="nsa_compress",
    )(rk, rv, w1x, w2x, posx)


def _cmp_to_sel_matrix(n_cmp_pad, n_cmp, n_sel):
    a = NSA_SEL_BLOCK // NSA_CMP_STRIDE
    b = NSA_CMP_LEN // NSA_CMP_STRIDE
    j = np.arange(n_sel)[:, None, None]
    idx = a * j + np.arange(a)[None, :, None] - np.arange(b)[None, None, :]
    jj = np.broadcast_to(j, idx.shape)
    ok = (idx >= 0) & (idx < n_cmp)
    m = np.zeros((n_cmp_pad, LANES), np.float32)
    np.add.at(m, (idx[ok], jj[ok]), 1.0)
    return m


def _nsa_kernel(q_ref, kc_ref, vc_ref, ks_ref, vs_ref, kw_ref, vw_ref, pf_ref, e_ref, m2t_ref, t2_ref, gx_ref,
                o_ref, qs_ref, flag_ref, ow_ref, m_ref, l_ref, acc_ref, *, tq, tk, seq, n_sel, n_top):
    qi = pl.program_id(1)
    qs = qi * tq
    nc = kc_ref.shape[1]
    heads = [(g, j) for g in range(NSA_GROUPS) for j in range(NSA_HPG)]

    def slope_of(g, j):
        return _alibi_slope(g * NSA_HPG + j, NSA_HEADS)

    for n, (g, j) in enumerate(heads):
        qs_ref[n] = _pick_half(q_ref[0, :, j * LANES:(j + 1) * LANES], g)

    cidx = lax.broadcasted_iota(jnp.int32, (tq, nc), 1)
    dist_c = (qs + lax.broadcasted_iota(jnp.int32, (tq, nc), 0)) - (cidx * NSA_CMP_STRIDE + (NSA_CMP_LEN - 1))
    valid_c = (dist_c | ((nc - 2) - cidx)) >= 0
    dist_cf = dist_c.astype(F32)
    kc = kc_ref[0]
    vc = vc_ref[0]
    o_c = {}
    psum = [None] * NSA_GROUPS
    for n, (g, j) in enumerate(heads):
        s = jnp.where(valid_c, _qk(qs_ref[n], kc) - slope_of(g, j) * dist_cf, NEG)
        m = jnp.max(s, axis=1, keepdims=True)
        p = jnp.where(valid_c, jnp.exp2(s - m), 0.0)
        l = jnp.sum(p, axis=1, keepdims=True)
        pn = p / jnp.where(l > 0.0, l, 1.0)
        o_c[(g, j)] = jnp.dot(pn.astype(BF16), vc, preferred_element_type=F32)
        psum[g] = pn if psum[g] is None else psum[g] + pn

    back = -(-NSA_WINDOW // tq) * tq
    width = min(back + tq, seq)
    start = _window_start(qs, back, width, seq, tq)
    dist = ((qs - start) + lax.broadcasted_iota(jnp.int32, (tq, width), 0)
            - lax.broadcasted_iota(jnp.int32, (tq, width), 1))
    valid_w = (dist | ((NSA_WINDOW - 1) - dist)) >= 0
    distf = dist.astype(F32)
    kw = kw_ref[0, pl.ds(start, width), :]
    vw = vw_ref[0, pl.ds(start, width), :]
    for n, (g, j) in enumerate(heads):
        s = jnp.where(valid_w, _qk(qs_ref[n], kw) - slope_of(g, j) * distf, NEG)
        ow_ref[n], _ = _softmax_pv(s, vw)

    rsel = -(-n_sel // 8) * 8
    blk = lax.broadcasted_iota(jnp.int32, (rsel, tq), 0)
    blkf = blk.astype(F32)
    tpos = qs + lax.broadcasted_iota(jnp.int32, (rsel, tq), 1)
    cur = tpos >> 6
    forced = (blk == 0) | (blk == cur) | (blk == cur - 1)
    causal_b = blk * NSA_SEL_BLOCK <= tpos
    sel_b = []
    for g in range(NSA_GROUPS):
        ph = psum[g].astype(BF16)
        plo = (psum[g] - ph.astype(F32)).astype(BF16)
        imp = (_qk(m2t_ref[...], ph) + _qk(m2t_ref[...], plo))[0:rsel]
        score = jnp.where(causal_b, jnp.where(forced, NSA_FORCE_SCORE, imp), NEG)
        score = jnp.where(blk < n_sel, score, PICKED)
        sel = jnp.zeros((rsel, tq), F32)
        for _ in range(n_top):
            mx = jnp.max(score, axis=0, keepdims=True)
            idx = jnp.min(jnp.where(score == mx, blkf, float(LANES)), axis=0, keepdims=True)
            hit = blkf == idx
            sel = jnp.where(hit, 1.0, sel)
            score = jnp.where(hit, PICKED, score)
        sel_b.append(sel.astype(BF16))

    ones = jnp.ones((tq, LANES), BF16)
    for g in range(NSA_GROUPS):
        per_block = jnp.dot(sel_b[g], ones, preferred_element_type=F32)
        flag_ref[g] = jnp.dot(t2_ref[:, 0:rsel], per_block.astype(BF16), preferred_element_type=F32)
    _flash_reset(m_ref, l_ref, acc_ref)

    def slc_body(ki, c):
        ks = pl.multiple_of(ki * tk, tk)
        for g in range(NSA_GROUPS):
            @pl.when(flag_ref[g, pl.ds(ki, 1), :][0, 0] > 0.5)
            def _(g=g):
                dist = ((qs - ks) + lax.broadcasted_iota(jnp.int32, (tq, tk), 0)
                        - lax.broadcasted_iota(jnp.int32, (tq, tk), 1))
                distf = dist.astype(F32)
                k = ks_ref[0, pl.ds(ks, tk), :]
                v = vs_ref[0, pl.ds(ks, tk), :]
                picked = lax.dot_general(sel_b[g], e_ref[0:rsel, pl.ds(ks, tk)], (((0,), (0,)), ((), ())),
                                         preferred_element_type=F32)
                valid = jnp.where(dist >= 0, picked, 0.0) > 0.5
                for j in range(NSA_HPG):
                    n = g * NSA_HPG + j
                    s = jnp.where(valid, _qk(qs_ref[n], k) - slope_of(g, j) * distf, NEG)
                    _flash_update(n, s, None, v, m_ref, l_ref, acc_ref)
        return c

    lax.fori_loop(0, (qs + tq + tk - 1) // tk, slc_body, 0)

    gl = jax.nn.sigmoid(pf_ref[0])
    g_hi = gl.astype(BF16)
    g_lo = (gl - g_hi.astype(F32)).astype(BF16)

    def gate(j, r):
        x = gx_ref[j * 3 + r]
        return jnp.dot(g_hi, x, preferred_element_type=F32) + jnp.dot(g_lo, x, preferred_element_type=F32)

    for j in range(NSA_HPG):
        n0, n1 = j, NSA_HPG + j
        comb = (gate(j, 0) * _merge_halves(o_c[(0, j)], o_c[(1, j)])
                + gate(j, 1) * _merge_halves(acc_ref[n0] / l_ref[n0], acc_ref[n1] / l_ref[n1])
                + gate(j, 2) * _merge_halves(ow_ref[n0], ow_ref[n1]))
        o_ref[0, :, j * LANES:(j + 1) * LANES] = comb.astype(BF16)


def _nsa_attention(pb, pf, kc, vc, tq=512, tk=512):
    nb, seq, _ = pb.shape
    tk = min(tk, seq)
    nc = kc.shape[1]
    n_sel = seq // NSA_SEL_BLOCK
    n_top = min(NSA_TOP_N, n_sel)
    n_cmp = seq // NSA_CMP_STRIDE - NSA_CMP_LEN // NSA_CMP_STRIDE + 1
    expand = (np.arange(LANES)[:, None] == (np.arange(seq)[None, :] // NSA_SEL_BLOCK))
    e = jnp.asarray(expand, BF16)
    m2t = jnp.asarray(_cmp_to_sel_matrix(nc, n_cmp, n_sel).T, BF16)
    n_tiles = seq // tk
    tile_rows = -(-n_tiles // 8) * 8
    blocks_per_tile = tk // NSA_SEL_BLOCK
    t2 = jnp.asarray(np.arange(tile_rows)[:, None] == (np.arange(LANES)[None, :] // blocks_per_tile), BF16)
    gx = np.zeros((NSA_HPG * 3, LANES, LANES), np.float32)
    for j in range(NSA_HPG):
        for r in range(3):
            for g in range(NSA_GROUPS):
                gx[j * 3 + r, GATE_COL + (g * NSA_HPG + j) * 3 + r, g * HALF:(g + 1) * HALF] = 1.0
    gx = jnp.asarray(gx, BF16)
    w = NSA_HPG * LANES
    kvb = P_NSAKV // LANES
    kv_spec = lambda i: pl.BlockSpec((1, seq, LANES), lambda b, q, i=i: (b, 0, kvb + i))
    return pl.pallas_call(
        functools.partial(_nsa_kernel, tq=tq, tk=tk, seq=seq, n_sel=n_sel, n_top=n_top),
        grid=(nb, seq // tq),
        in_specs=[
            pl.BlockSpec((1, tq, w), lambda b, i: (b, i, P_NSAQ // w)),
            pl.BlockSpec((1, nc, LANES), lambda b, i: (b, 0, 0)),
            pl.BlockSpec((1, nc, LANES), lambda b, i: (b, 0, 0)),
            kv_spec(2), kv_spec(3), kv_spec(4), kv_spec(5),
            pl.BlockSpec((1, tq, LANES), lambda b, i: (b, i, 0)),
            pl.BlockSpec((LANES, seq), lambda b, i: (0, 0)),
            pl.BlockSpec((LANES, nc), lambda b, i: (0, 0)),
            pl.BlockSpec((tile_rows, LANES), lambda b, i: (0, 0)),
            pl.BlockSpec((NSA_HPG * 3, LANES, LANES), lambda b, i: (0, 0, 0)),
        ],
        out_specs=pl.BlockSpec((1, tq, w), lambda b, i: (b, i, 0)),
        out_shape=jax.ShapeDtypeStruct((nb, seq, w), BF16),
        scratch_shapes=([pltpu.VMEM((NSA_HEADS, tq, LANES), BF16), pltpu.VMEM((NSA_GROUPS, tile_rows, LANES), F32),
                         pltpu.VMEM((NSA_HEADS, tq, LANES), F32)] + _flash_scratch(NSA_HEADS, tq)),
        compiler_params=_cparams(("parallel", "arbitrary")),
        name="nsa_attention",
    )(pb, kc, vc, pb, pb, pb, pb, pf, e, m2t, t2, gx)


def _mixers(pb, pf, f_bias, cmp_w1, cmp_w2, cmp_pos, lam_vecs, subln_g, lam_init):
    nb, seq, _ = pb.shape
    cum = _forget_cumsum(pf, f_bias)
    cumrow = jnp.transpose(cum[:, :, :8], (0, 2, 1))
    o_fox = _fox_attention(pb, cum, cumrow)
    nc = seq // NSA_CMP_STRIDE
    rk = pb[:, :, P_NSAKV:P_NSAKV + LANES].reshape(nb, nc, NSA_CMP_STRIDE * LANES)
    rv = pb[:, :, P_NSAKV + LANES:P_NSAKV + 2 * LANES].reshape(nb, nc, NSA_CMP_STRIDE * LANES)
    kc, vc = _nsa_compress(rk, rv, *_compress_weights(cmp_w1, cmp_w2, cmp_pos))
    o_nsa = _nsa_attention(pb, pf, kc, vc)
    o_dil = _dil_attention(pb)
    o_dif = _diff_attention(pb, lam_vecs, subln_g, lam_init)
    return [o_dil, o_fox, o_nsa, o_dif]


def kernel(x, c, ada_w, ada_b, norm_mix_g, norm_ffn_g, w_in, fox_f_bias, nsa_cmp_w1, nsa_cmp_w2, nsa_cmp_pos,
           diff_lambda, diff_subln_g, w_out, ffn_w_gate, ffn_w_up, ffn_w_down, final_norm_g):
    nb, seq, d = x.shape
    depth = ada_w.shape[0]
    m = nb * seq
    mods = _modulation(c, ada_w, ada_b)
    w_p = _relayout_w_in(w_in, _proj_colmap())
    w_pf = w_p[:, :, P_FGATE:P_FGATE + LANES] + w_p[:, :, P_NGATE:P_NGATE + LANES]
    w_o = _gather_axis(w_out, _out_rowmap(), 1).astype(BF16)
    w_g, w_u, w_d = ffn_w_gate.astype(BF16), ffn_w_up.astype(BF16), ffn_w_down.astype(BF16)
    h = x.reshape(m, d)
    for layer in range(depth):
        sh1, sc1, g1, sh2, sc2, g2 = [mods[layer, :, i * d:(i + 1) * d].reshape(nb, 1, d) for i in range(6)]
        lam_init = 0.8 - 0.6 * math.exp(-0.3 * layer)
        pb, pf = _norm_proj(h, norm_mix_g[layer], sc1, sh1, w_p, layer, w_pf[layer], seq)
        parts = _mixers(pb.reshape(nb, seq, N_PB), pf.reshape(nb, seq, LANES), fox_f_bias[layer],
                        nsa_cmp_w1[layer], nsa_cmp_w2[layer], nsa_cmp_pos[layer],
                        diff_lambda[layer], diff_subln_g[layer], lam_init)
        h = _matmul_gated_residual([p.reshape(m, p.shape[-1]) for p in parts], w_o, layer, h, g1, seq,
                                   name="out_proj_res")
        act = _norm_ffn_up(h, norm_ffn_g[layer], sc2, sh2, w_g, w_u, layer, seq)
        h = _matmul_gated_residual([act], w_d, layer, h, g2, seq, name="ffn_down_res")
    return _final_norm(h, final_norm_g).reshape(nb, seq, d)
```

```python
import functools
import math

import numpy as np
import jax
import jax.numpy as jnp
from jax import lax
from jax.experimental import pallas as pl
from jax.experimental.pallas import tpu as pltpu

F32 = jnp.float32
BF16 = jnp.bfloat16

LANES = 128
HEAD_DIM = 64
HALF = HEAD_DIM
RMS_EPS = 1e-6
NEG = -1e30
PICKED = -3e38
LOG2E = math.log2(math.e)
Q_SCALE = HEAD_DIM ** -0.5 * LOG2E

FOX_HEADS = 7
NSA_HEADS = 8
NSA_GROUPS = 2
NSA_HPG = NSA_HEADS // NSA_GROUPS
NSA_CMP_LEN = 32
NSA_CMP_STRIDE = 16
NSA_SEL_BLOCK = 64
NSA_TOP_N = 16
NSA_WINDOW = 512
NSA_WIN_ROWS = 256
NSA_FORCE_SCORE = 1e4
DIL_CONFIGS = ((128, 1), (512, 4), (2048, 16))
DIL_HPG = 3
DIL_HEADS = DIL_HPG * len(DIL_CONFIGS)
DIFF_HEADS = 4
ALIBI_MAX_EXP = 8.0

_SPLITS = (
    ("fox_q", 448), ("fox_k", 448), ("fox_v", 448), ("fox_f", 7),
    ("nsa_q", 512), ("nsa_cmp_k", 128), ("nsa_cmp_v", 128), ("nsa_slc_k", 128),
    ("nsa_slc_v", 128), ("nsa_win_k", 128), ("nsa_win_v", 128), ("nsa_gate", 24),
    ("dil_q", 576), ("dil_k", 576), ("dil_v", 576),
    ("diff_q", 512), ("diff_k", 512), ("diff_v", 512),
)
_SRC = {}
_o = 0
for _n, _w in _SPLITS:
    _SRC[_n] = _o
    _o += _w
N_IN = _o

DIL_GW = 256
DIL_W = 3 * DIL_GW
P_FOXQ, P_FOXK, P_FOXV = 0, 512, 1024
P_NSAQ = 1536
P_NSAKV = 2048
P_DILQ, P_DILK, P_DILV = 2816, 3584, 4352
P_DIFQ, P_DIFK, P_DIFV = 5120, 5632, 6144
N_PB = 6656
GATE_COL = 8
RELAYOUT_TILE = 512
P_FGATE, P_NGATE = N_PB, N_PB + RELAYOUT_TILE
N_WP = N_PB + 2 * RELAYOUT_TILE


def _proj_colmap():
    pb = -np.ones(N_WP, np.int64)
    ar = np.arange
    for name, dst in (("fox_q", P_FOXQ), ("fox_k", P_FOXK), ("fox_v", P_FOXV)):
        pb[dst:dst + 448] = _SRC[name] + ar(448)
    for j in range(NSA_HPG):
        for g in range(NSA_GROUPS):
            d = P_NSAQ + j * LANES + g * HALF
            pb[d:d + HALF] = _SRC["nsa_q"] + (g * NSA_HPG + j) * HEAD_DIM + ar(HALF)
    for i, name in enumerate(("nsa_cmp_k", "nsa_cmp_v", "nsa_slc_k", "nsa_slc_v", "nsa_win_k", "nsa_win_v")):
        pb[P_NSAKV + i * LANES:P_NSAKV + (i + 1) * LANES] = _SRC[name] + ar(LANES)
    for name, dst in (("dil_q", P_DILQ), ("dil_k", P_DILK), ("dil_v", P_DILV)):
        for g in range(len(DIL_CONFIGS)):
            for jj in range(DIL_HPG):
                d = dst + (2 * g + jj // 2) * LANES + (jj % 2) * HALF
                pb[d:d + HALF] = _SRC[name] + (g * DIL_HPG + jj) * HEAD_DIM + ar(HALF)
    for name, dst in (("diff_q", P_DIFQ), ("diff_k", P_DIFK), ("diff_v", P_DIFV)):
        pb[dst:dst + 512] = _SRC[name] + ar(512)
    pb[P_FGATE:P_FGATE + FOX_HEADS] = _SRC["fox_f"] + ar(FOX_HEADS)
    pb[P_NGATE + GATE_COL:P_NGATE + GATE_COL + 24] = _SRC["nsa_gate"] + ar(24)
    return pb


O_DIL, O_FOX, O_NSA, O_DIF = 0, 768, 1280, 1792
N_O = 2304


def _out_rowmap():
    m = -np.ones(N_O, np.int64)
    ar = np.arange
    m[O_FOX:O_FOX + 448] = ar(448)
    for j in range(NSA_HPG):
        for g in range(NSA_GROUPS):
            d = O_NSA + j * LANES + g * HALF
            m[d:d + HALF] = 448 + (g * NSA_HPG + j) * HEAD_DIM + ar(HALF)
    for g in range(len(DIL_CONFIGS)):
        for jj in range(DIL_HPG):
            d = O_DIL + (2 * g + jj // 2) * LANES + (jj % 2) * HALF
            m[d:d + HALF] = 960 + (g * DIL_HPG + jj) * HEAD_DIM + ar(HALF)
    m[O_DIF:O_DIF + 512] = 1536 + ar(512)
    return m


def _gather_axis(x, idx, axis):
    pieces = []
    i = 0
    n = len(idx)
    while i < n:
        j = i + 1
        if idx[i] < 0:
            while j < n and idx[j] < 0:
                j += 1
            shape = list(x.shape)
            shape[axis] = j - i
            pieces.append(jnp.zeros(shape, x.dtype))
        else:
            while j < n and idx[j] == idx[j - 1] + 1:
                j += 1
            pieces.append(lax.slice_in_dim(x, int(idx[i]), int(idx[j - 1]) + 1, axis=axis))
        i = j
    return jnp.concatenate(pieces, axis=axis)


def _cparams(sem, vmem_mb=48):
    return pltpu.CompilerParams(dimension_semantics=sem, vmem_limit_bytes=vmem_mb * 1024 * 1024)


def _alibi_slope(k, n_heads):
    return float(2.0 ** (-ALIBI_MAX_EXP * (k + 1) / n_heads)) * LOG2E


def _src_col_scale(n_pad):
    sc = np.ones(n_pad, np.float32)
    for name, width in (("fox_q", 448), ("nsa_q", 512), ("dil_q", 576), ("diff_q", 512)):
        sc[_SRC[name]:_SRC[name] + width] = Q_SCALE
    return sc


def _relayout_plan(colmap, n_src, tile):
    n_tiles = len(colmap) // tile
    last = (n_src - 1) // tile
    win = np.zeros(n_tiles, np.int32)
    local = -np.ones((n_tiles, 1, tile), np.int32)
    for t in range(n_tiles):
        src = colmap[t * tile:(t + 1) * tile]
        used = src[src >= 0]
        a = min(int(used.min()) // tile, max(last - 1, 0))
        assert int(used.max()) < (a + 2) * tile, "tile sources exceed the two-block window"
        win[t] = a
        local[t, 0] = np.where(src >= 0, src - a * tile, -1)
    return win, local


def _relayout_kernel(win_ref, idx_ref, sa_ref, sb_ref, wa_ref, wb_ref, o_ref, *, n_src, tile):
    a = win_ref[pl.program_id(1)]
    idx = idx_ref[0]
    row = lax.broadcasted_iota(jnp.int32, (tile, tile), 0)
    col = lax.broadcasted_iota(jnp.int32, wa_ref.shape[1:], 1)
    acc = None
    for part, (w_ref, s_ref) in enumerate(((wa_ref, sa_ref), (wb_ref, sb_ref))):
        sel = jnp.where(row + part * tile == idx, 1.0, 0.0).astype(BF16)
        w = (jnp.where(col + (a + part) * tile < n_src, w_ref[0], 0.0) * s_ref[0]).astype(BF16)
        d = jnp.dot(w, sel, preferred_element_type=F32)
        acc = d if acc is None else acc + d
    o_ref[0] = acc.astype(BF16)


def _relayout_w_in(w_in, colmap, tile=RELAYOUT_TILE):
    depth, d, n_src = w_in.shape
    win, local = _relayout_plan(colmap, n_src, tile)
    n_tiles = len(win)
    n_blk = -(-n_src // tile)
    scale = jnp.asarray(_src_col_scale(n_blk * tile).reshape(n_blk, 1, tile))
    grid_spec = pltpu.PrefetchScalarGridSpec(
        num_scalar_prefetch=1,
        grid=(depth, n_tiles),
        in_specs=[
            pl.BlockSpec((1, 1, tile), lambda l, t, win: (t, 0, 0)),
            pl.BlockSpec((1, 1, tile), lambda l, t, win: (win[t], 0, 0)),
            pl.BlockSpec((1, 1, tile), lambda l, t, win: (win[t] + 1, 0, 0)),
            pl.BlockSpec((1, d, tile), lambda l, t, win: (l, 0, win[t])),
            pl.BlockSpec((1, d, tile), lambda l, t, win: (l, 0, win[t] + 1)),
        ],
        out_specs=pl.BlockSpec((1, d, tile), lambda l, t, win: (l, 0, t)),
    )
    return pl.pallas_call(
        functools.partial(_relayout_kernel, n_src=n_src, tile=tile),
        grid_spec=grid_spec,
        out_shape=jax.ShapeDtypeStruct((depth, d, n_tiles * tile), BF16),
        compiler_params=_cparams(("parallel", "arbitrary")),
        name="w_in_relayout",
    )(jnp.asarray(win), jnp.asarray(local), scale, scale, w_in, w_in)


def _mod_kernel(c_ref, w_ref, b_ref, o_ref):
    nb = c_ref.shape[0]
    tn = w_ref.shape[2]
    for b in range(nb):
        cb = c_ref[b]
        ca = cb * jax.nn.sigmoid(cb)
        for j in range(tn // LANES):
            sl = slice(j * LANES, (j + 1) * LANES)
            w = w_ref[0, :, sl]
            o_ref[0, b:b + 1, sl] = jnp.sum(w * ca, axis=0, keepdims=True) + b_ref[0, :, sl]


def _modulation(c, ada_w, ada_b, tn=512):
    depth, d, n = ada_w.shape
    nb = c.shape[0]
    c_rep = jnp.broadcast_to(c[:, :, None], (nb, d, LANES))
    return pl.pallas_call(
        _mod_kernel,
        grid=(depth, n // tn),
        in_specs=[
            pl.BlockSpec((nb, d, LANES), lambda l, j: (0, 0, 0)),
            pl.BlockSpec((1, d, tn), lambda l, j: (l, 0, j)),
            pl.BlockSpec((1, 1, tn), lambda l, j: (l, 0, j)),
        ],
        out_specs=pl.BlockSpec((1, nb, tn), lambda l, j: (l, 0, j)),
        out_shape=jax.ShapeDtypeStruct((depth, nb, n), F32),
        compiler_params=_cparams(("parallel", "parallel")),
        name="adaln_mod",
    )(c_rep, ada_w, ada_b.reshape(depth, 1, n))


def _norm_mod(h_ref, g_ref, sc_ref, sh_ref):
    x = h_ref[...]
    ms = jnp.mean(x * x, axis=-1, keepdims=True)
    y = x * lax.rsqrt(ms + RMS_EPS) * g_ref[...]
    return (y * (1.0 + sc_ref[0]) + sh_ref[0]).astype(BF16)


def _proj_kernel(h_ref, g_ref, sc_ref, sh_ref, w_ref, wf_ref, pb_ref, pf_ref, u_ref):
    @pl.when(pl.program_id(1) == 0)
    def _():
        u = _norm_mod(h_ref, g_ref, sc_ref, sh_ref)
        u_ref[...] = u
        pf_ref[...] = jnp.dot(u, wf_ref[...], preferred_element_type=F32)

    pb_ref[...] = jnp.dot(u_ref[...], w_ref[0], preferred_element_type=F32).astype(BF16)


def _norm_proj(h, gain, sc, sh, w_all, layer, wf, seq, tm=1024, tn=512):
    m, d = h.shape
    per = seq // tm
    return pl.pallas_call(
        _proj_kernel,
        grid=(m // tm, N_PB // tn),
        in_specs=[
            pl.BlockSpec((tm, d), lambda i, j: (i, 0)),
            pl.BlockSpec((1, d), lambda i, j: (0, 0)),
            pl.BlockSpec((1, 1, d), lambda i, j: (i // per, 0, 0)),
            pl.BlockSpec((1, 1, d), lambda i, j: (i // per, 0, 0)),
            pl.BlockSpec((1, d, tn), lambda i, j: (layer, 0, j)),
            pl.BlockSpec((d, LANES), lambda i, j: (0, 0)),
        ],
        out_specs=[
            pl.BlockSpec((tm, tn), lambda i, j: (i, j)),
            pl.BlockSpec((tm, LANES), lambda i, j: (i, 0)),
        ],
        out_shape=[jax.ShapeDtypeStruct((m, N_PB), BF16), jax.ShapeDtypeStruct((m, LANES), F32)],
        scratch_shapes=[pltpu.VMEM((tm, d), BF16)],
        compiler_params=_cparams(("parallel", "arbitrary")),
        name="norm_in_proj",
    )(h, gain.reshape(1, d), sc, sh, w_all, wf)


def _ffn_up_kernel(h_ref, g_ref, sc_ref, sh_ref, wg_ref, wu_ref, a_ref, u_ref):
    @pl.when(pl.program_id(1) == 0)
    def _():
        u_ref[...] = _norm_mod(h_ref, g_ref, sc_ref, sh_ref)

    u = u_ref[...]
    gate = jnp.dot(u, wg_ref[0], preferred_element_type=F32)
    up = jnp.dot(u, wu_ref[0], preferred_element_type=F32)
    a_ref[...] = (gate * jax.nn.sigmoid(gate) * up).astype(BF16)


def _norm_ffn_up(h, gain, sc, sh, wg, wu, layer, seq, tm=1024, tn=512):
    m, d = h.shape
    n = wg.shape[2]
    per = seq // tm
    return pl.pallas_call(
        _ffn_up_kernel,
        grid=(m // tm, n // tn),
        in_specs=[
            pl.BlockSpec((tm, d), lambda i, j: (i, 0)),
            pl.BlockSpec((1, d), lambda i, j: (0, 0)),
            pl.BlockSpec((1, 1, d), lambda i, j: (i // per, 0, 0)),
            pl.BlockSpec((1, 1, d), lambda i, j: (i // per, 0, 0)),
            pl.BlockSpec((1, d, tn), lambda i, j: (layer, 0, j)),
            pl.BlockSpec((1, d, tn), lambda i, j: (layer, 0, j)),
        ],
        out_specs=pl.BlockSpec((tm, tn), lambda i, j: (i, j)),
        out_shape=jax.ShapeDtypeStruct((m, n), BF16),
        scratch_shapes=[pltpu.VMEM((tm, d), BF16)],
        compiler_params=_cparams(("parallel", "arbitrary")),
        name="norm_ffn_up",
    )(h, gain.reshape(1, d), sc, sh, wg, wu)


def _mm_res_kernel(*refs, widths):
    n = len(widths)
    a_refs, (w_ref, h_ref, g_ref, o_ref) = refs[:n], refs[n:]
    acc = None
    off = 0
    for a_ref, width in zip(a_refs, widths):
        d = jnp.dot(a_ref[...], w_ref[0, off:off + width, :], preferred_element_type=F32)
        acc = d if acc is None else acc + d
        off += width
    o_ref[...] = h_ref[...] + g_ref[0] * acc


def _matmul_gated_residual(a_parts, w, layer, h, gate, seq, tm=1024, tn=512, name="mm_res"):
    m = h.shape[0]
    widths = tuple(a.shape[1] for a in a_parts)
    k, n = w.shape[1], w.shape[2]
    assert sum(widths) == k
    per = seq // tm
    block_bytes = tm * k * 2 + k * tn * 2 + 2 * tm * tn * 4
    vmem_mb = (2 * block_bytes + 4 * tm * tn * 4) // (1024 * 1024) + 1
    return pl.pallas_call(
        functools.partial(_mm_res_kernel, widths=widths),
        grid=(m // tm, n // tn),
        in_specs=[pl.BlockSpec((tm, wd), lambda i, j: (i, 0)) for wd in widths] + [
            pl.BlockSpec((1, k, tn), lambda i, j: (layer, 0, j)),
            pl.BlockSpec((tm, tn), lambda i, j: (i, j)),
            pl.BlockSpec((1, 1, tn), lambda i, j: (i // per, 0, j)),
        ],
        out_specs=pl.BlockSpec((tm, tn), lambda i, j: (i, j)),
        out_shape=jax.ShapeDtypeStruct((m, n), F32),
        compiler_params=_cparams(("parallel", "arbitrary"), vmem_mb=vmem_mb),
        name=name,
    )(*a_parts, w, h, gate)


def _final_norm_kernel(h_ref, g_ref, o_ref):
    x = h_ref[...]
    ms = jnp.mean(x * x, axis=-1, keepdims=True)
    o_ref[...] = x * lax.rsqrt(ms + RMS_EPS) * g_ref[...]


def _final_norm(h, gain, tm=512):
    m, d = h.shape
    return pl.pallas_call(
        _final_norm_kernel,
        grid=(m // tm,),
        in_specs=[pl.BlockSpec((tm, d), lambda i: (i, 0)), pl.BlockSpec((1, d), lambda i: (0, 0))],
        out_specs=pl.BlockSpec((tm, d), lambda i: (i, 0)),
        out_shape=jax.ShapeDtypeStruct((m, d), F32),
        compiler_params=_cparams(("parallel",)),
        name="final_norm",
    )(h, gain.reshape(1, d))


def _half_mask(shape, half):
    lane = lax.broadcasted_iota(jnp.int32, shape, 1)
    return (lane < HALF) if half == 0 else (lane >= HALF)


def _pick_half(qp, half):
    return jnp.where(_half_mask(qp.shape, half), qp, jnp.zeros_like(qp))


def _lane_tile(x, n):
    return x if n == 1 else jnp.concatenate([x] * n, axis=1)


def _qk(qh, k):
    return lax.dot_general(qh, k, (((1,), (1,)), ((), ())), preferred_element_type=F32)


def _merge_halves(lo, hi):
    if hi is None:
        hi = jnp.zeros_like(lo)
    return jnp.where(_half_mask(lo.shape, 0), lo, hi)


def _flash_scratch(n_heads, tq):
    return [pltpu.VMEM((n_heads, tq, LANES), F32), pltpu.VMEM((n_heads, tq, LANES), F32),
            pltpu.VMEM((n_heads, tq, LANES), F32)]


def _flash_reset(m_ref, l_ref, acc_ref):
    m_ref[...] = jnp.full(m_ref.shape, NEG, F32)
    l_ref[...] = jnp.zeros(l_ref.shape, F32)
    acc_ref[...] = jnp.zeros(acc_ref.shape, F32)


def _flash_update(h, s, valid, v, m_ref, l_ref, acc_ref):
    m_old = m_ref[h]
    m_new = jnp.maximum(m_old, jnp.max(s, axis=1, keepdims=True))
    alpha = jnp.exp2(m_old - m_new)
    p = jnp.exp2(s - _lane_tile(m_new, s.shape[1] // LANES))
    if valid is not None:
        p = jnp.where(valid, p, 0.0)
    l_ref[h] = alpha * l_ref[h] + jnp.sum(p, axis=1, keepdims=True)
    acc_ref[h] = alpha * acc_ref[h] + jnp.dot(p.astype(BF16), v, preferred_element_type=F32)
    m_ref[h] = m_new


def _softmax_pv(s, v):
    m = jnp.max(s, axis=1, keepdims=True)
    p = jnp.exp2(s - m)
    l = jnp.sum(p, axis=1, keepdims=True)
    return jnp.dot(p.astype(BF16), v, preferred_element_type=F32) / l, m + jnp.log2(l)


def _causal_steps(qi, t, step):
    def wide(k2, c):
        step(pl.multiple_of(k2 * (2 * t), 2 * t), 2 * t, False)
        return c

    lax.fori_loop(0, lax.shift_right_logical(qi, 1), wide, 0)

    @pl.when((qi & 1) == 1)
    def _():
        step(pl.multiple_of((qi - 1) * t, t), t, False)

    step(pl.multiple_of(qi * t, t), t, True)


def _window_start(qs, back, width, seq, align):
    return pl.multiple_of(jnp.clip(qs - back, 0, seq - width), align)


def _cumsum_kernel(pf_ref, fb_ref, o_ref, *, chunk):
    seq = pf_ref.shape[1]
    r = lax.broadcasted_iota(jnp.int32, (chunk, chunk), 0)
    c = lax.broadcasted_iota(jnp.int32, (chunk, chunk), 1)
    tri = jnp.where(r >= c, 1.0, 0.0).astype(BF16)

    def body(i, carry):
        st = pl.multiple_of(i * chunk, chunk)
        x = pf_ref[0, pl.ds(st, chunk), :] + fb_ref[...]
        ls = jnp.minimum(x, 0.0) - jnp.log1p(jnp.exp(-jnp.abs(x)))
        hi = ls.astype(BF16)
        r1 = ls - hi.astype(F32)
        mid = r1.astype(BF16)
        lo = (r1 - mid.astype(F32)).astype(BF16)
        cs = (jnp.dot(tri, hi, preferred_element_type=F32)
              + jnp.dot(tri, mid, preferred_element_type=F32)
              + jnp.dot(tri, lo, preferred_element_type=F32))
        out = cs + carry
        o_ref[0, pl.ds(st, chunk), :] = out * LOG2E
        return out[chunk - 1:chunk, :]

    lax.fori_loop(0, seq // chunk, body, jnp.zeros((1, LANES), F32))


def _forget_cumsum(pf, f_bias, chunk=256):
    nb, seq, _ = pf.shape
    fb = jnp.zeros((1, LANES), F32).at[0, :FOX_HEADS].set(f_bias.astype(F32))
    return pl.pallas_call(
        functools.partial(_cumsum_kernel, chunk=chunk),
        grid=(nb,),
        in_specs=[pl.BlockSpec((1, seq, LANES), lambda b: (b, 0, 0)), pl.BlockSpec((1, LANES), lambda b: (0, 0))],
        out_specs=pl.BlockSpec((1, seq, LANES), lambda b: (b, 0, 0)),
        out_shape=jax.ShapeDtypeStruct((nb, seq, LANES), F32),
        compiler_params=_cparams(("parallel",)),
        name="forget_cumsum",
    )(pf, fb)


def _fox_kernel(q_ref, k_ref, v_ref, cc_ref, cr_ref, o_ref, qs_ref, cq_ref, m_ref, l_ref, acc_ref, *, t):
    qi = pl.program_id(1)
    _flash_reset(m_ref, l_ref, acc_ref)
    for h in range(FOX_HEADS):
        qs_ref[h] = _pick_half(q_ref[0, :, (h // 2) * LANES:(h // 2 + 1) * LANES], h % 2)
        cq_ref[h] = jnp.broadcast_to(cc_ref[0, :, h:h + 1], (t, LANES))

    def step(ks, width, masked):
        if masked:
            causal = (lax.broadcasted_iota(jnp.int32, (t, t), 0) >= lax.broadcasted_iota(jnp.int32, (t, t), 1))
        for h in range(FOX_HEADS):
            sl = slice((h // 2) * LANES, (h // 2 + 1) * LANES)
            k = k_ref[0, pl.ds(ks, width), sl]
            v = v_ref[0, pl.ds(ks, width), sl]
            s = ((_qk(qs_ref[h], k) + _lane_tile(cq_ref[h], width // LANES))
                 - cr_ref[0, h:h + 1, pl.ds(ks, width)])
            if masked:
                s = jnp.where(causal, s, NEG)
            _flash_update(h, s, None, v, m_ref, l_ref, acc_ref)

    _causal_steps(qi, t, step)
    for p in range(4):
        lo = acc_ref[2 * p] / l_ref[2 * p]
        hi = acc_ref[2 * p + 1] / l_ref[2 * p + 1] if 2 * p + 1 < FOX_HEADS else None
        o_ref[0, :, p * LANES:(p + 1) * LANES] = _merge_halves(lo, hi).astype(BF16)


def _fox_attention(pb, cum, cumrow, t=512):
    nb, seq, _ = pb.shape
    t = min(t, seq)
    w = 4 * LANES
    return pl.pallas_call(
        functools.partial(_fox_kernel, t=t),
        grid=(nb, seq // t),
        in_specs=[
            pl.BlockSpec((1, t, w), lambda b, i: (b, i, P_FOXQ // w)),
            pl.BlockSpec((1, seq, w), lambda b, i: (b, 0, P_FOXK // w)),
            pl.BlockSpec((1, seq, w), lambda b, i: (b, 0, P_FOXV // w)),
            pl.BlockSpec((1, t, LANES), lambda b, i: (b, i, 0)),
            pl.BlockSpec((1, 8, seq), lambda b, i: (b, 0, 0)),
        ],
        out_specs=pl.BlockSpec((1, t, w), lambda b, i: (b, i, 0)),
        out_shape=jax.ShapeDtypeStruct((nb, seq, w), BF16),
        scratch_shapes=([pltpu.VMEM((FOX_HEADS, t, LANES), BF16), pltpu.VMEM((FOX_HEADS, t, LANES), F32)]
                        + _flash_scratch(FOX_HEADS, t)),
        compiler_params=_cparams(("parallel", "arbitrary")),
        name="fox_attention",
    )(pb, pb, pb, cum, cumrow)


def _diff_kernel(q_ref, k_ref, v_ref, lam_ref, g_ref, o_ref, qs_ref, m_ref, l_ref, acc_ref, *, t, lam_init):
    qi = pl.program_id(1)
    n_slots = 2 * DIFF_HEADS
    _flash_reset(m_ref, l_ref, acc_ref)
    for n in range(n_slots):
        qs_ref[n] = _pick_half(q_ref[0, :, (n // 2) * LANES:(n // 2 + 1) * LANES], n % 2)
    tpos = (qi * t + lax.broadcasted_iota(jnp.int32, (t, LANES), 0)).astype(F32)

    def step(ks, width, masked):
        kpos = (ks + lax.broadcasted_iota(jnp.int32, (1, width), 1)).astype(F32)
        if masked:
            causal = (lax.broadcasted_iota(jnp.int32, (t, t), 0) >= lax.broadcasted_iota(jnp.int32, (t, t), 1))
        for n in range(n_slots):
            hd = n // 2
            sl = slice(hd * LANES, (hd + 1) * LANES)
            slope = _alibi_slope(hd, DIFF_HEADS)
            k = k_ref[0, pl.ds(ks, width), sl]
            v = v_ref[0, pl.ds(ks, width), sl]
            s = (_qk(qs_ref[n], k) + _lane_tile(tpos * (-slope), width // LANES)) + kpos * slope
            if masked:
                s = jnp.where(causal, s, NEG)
            _flash_update(n, s, None, v, m_ref, l_ref, acc_ref)

    _causal_steps(qi, t, step)
    lv = lam_ref[...]
    lam = (jnp.exp(jnp.sum(lv[0:1] * lv[1:2], axis=1, keepdims=True))
           - jnp.exp(jnp.sum(lv[2:3] * lv[3:4], axis=1, keepdims=True)) + lam_init)
    for hd in range(DIFF_HEADS):
        o = acc_ref[2 * hd] / l_ref[2 * hd] - lam * (acc_ref[2 * hd + 1] / l_ref[2 * hd + 1])
        ms = jnp.mean(o * o, axis=-1, keepdims=True)
        y = o * lax.rsqrt(ms + RMS_EPS) * g_ref[...]
        o_ref[0, :, hd * LANES:(hd + 1) * LANES] = (y * (1.0 - lam_init)).astype(BF16)


def _diff_attention(pb, lam_vecs, subln_g, lam_init, t=512):
    nb, seq, _ = pb.shape
    t = min(t, seq)
    w = DIFF_HEADS * LANES
    return pl.pallas_call(
        functools.partial(_diff_kernel, t=t, lam_init=lam_init),
        grid=(nb, seq // t),
        in_specs=[
            pl.BlockSpec((1, t, w), lambda b, i: (b, i, P_DIFQ // w)),
            pl.BlockSpec((1, seq, w), lambda b, i: (b, 0, P_DIFK // w)),
            pl.BlockSpec((1, seq, w), lambda b, i: (b, 0, P_DIFV // w)),
            pl.BlockSpec((4, HEAD_DIM), lambda b, i: (0, 0)),
            pl.BlockSpec((1, LANES), lambda b, i: (0, 0)),
        ],
        out_specs=pl.BlockSpec((1, t, w), lambda b, i: (b, i, 0)),
        out_shape=jax.ShapeDtypeStruct((nb, seq, w), BF16),
        scratch_shapes=[pltpu.VMEM((2 * DIFF_HEADS, t, LANES), BF16)] + _flash_scratch(2 * DIFF_HEADS, t),
        compiler_params=_cparams(("parallel", "arbitrary")),
        name="diff_attention",
    )(pb, pb, pb, lam_vecs.astype(F32), subln_g.astype(F32).reshape(1, LANES))


DIL_SUBWINDOW = 128


def _dil_group_kernel(q_ref, k_ref, v_ref, o_ref, lse_ref, *, tq, sub_len, dilation, head0):
    qs = pl.program_id(2) * tq
    width = min(2 * tq, sub_len)
    start = _window_start(qs, tq, width, sub_len, tq)
    dist = ((qs - start) + lax.broadcasted_iota(jnp.int32, (tq, width), 0)
            - lax.broadcasted_iota(jnp.int32, (tq, width), 1))
    valid = (dist | (DIL_SUBWINDOW - dist)) >= 0
    distf = dist.astype(F32)
    lane = lax.broadcasted_iota(jnp.int32, (tq, LANES), 1)
    lse_blk = jnp.zeros((tq, LANES), F32)
    outs = []
    for jj in range(DIL_HPG):
        sl = slice((jj // 2) * LANES, (jj // 2 + 1) * LANES)
        slope = _alibi_slope(head0 + jj, DIL_HEADS) * dilation
        k = k_ref[0, pl.ds(start, width), sl]
        v = v_ref[0, pl.ds(start, width), sl]
        s = jnp.where(valid, _qk(_pick_half(q_ref[0, :, sl], jj % 2), k) - slope * distf, NEG)
        o, lse = _softmax_pv(s, v)
        outs.append(o)
        lse_blk = jnp.where(lane == jj, lse, lse_blk)
    o_ref[0, :, 0:LANES] = _merge_halves(outs[0], outs[1])
    o_ref[0, :, LANES:2 * LANES] = _merge_halves(outs[2], None)
    lse_ref[0] = lse_blk


def _dil_group(pb, g, tq=256):
    nb, seq, n_pb = pb.shape
    window, dilation = DIL_CONFIGS[g]
    assert window == DIL_SUBWINDOW * dilation
    sub_len = seq // dilation
    tq = min(tq, sub_len)
    gw = DIL_GW
    if dilation == 1:
        views = (pb, pb, pb)
        cols = [base // gw + g for base in (P_DILQ, P_DILK, P_DILV)]
        per_tok = 0
    else:
        views = tuple(lax.slice_in_dim(pb, base + g * gw, base + (g + 1) * gw, axis=2)
                      .reshape(nb, sub_len, dilation * gw) for base in (P_DILQ, P_DILK, P_DILV))
        cols = [0, 0, 0]
        per_tok = 1
    o, lse = pl.pallas_call(
        functools.partial(_dil_group_kernel, tq=tq, sub_len=sub_len, dilation=dilation, head0=g * DIL_HPG),
        grid=(nb, dilation, sub_len // tq),
        in_specs=[
            pl.BlockSpec((1, tq, gw), lambda b, r, i: (b, i, r * per_tok + cols[0])),
            pl.BlockSpec((1, sub_len, gw), lambda b, r, i: (b, 0, r * per_tok + cols[1])),
            pl.BlockSpec((1, sub_len, gw), lambda b, r, i: (b, 0, r * per_tok + cols[2])),
        ],
        out_specs=[pl.BlockSpec((1, tq, gw), lambda b, r, i: (b, i, r)),
                   pl.BlockSpec((1, tq, LANES), lambda b, r, i: (b, i, r))],
        out_shape=[jax.ShapeDtypeStruct((nb, sub_len, dilation * gw), F32),
                   jax.ShapeDtypeStruct((nb, sub_len, dilation * LANES), F32)],
        compiler_params=_cparams(("parallel", "parallel", "arbitrary")),
        name=f"dil_group{g}",
    )(*views)
    return o.reshape(nb, seq, gw), lse.reshape(nb, seq, LANES)


def _dil_mix_kernel(*refs):
    ng = len(DIL_CONFIGS)
    o_refs, l_refs, (x_ref, out_ref) = refs[:ng], refs[ng:2 * ng], refs[2 * ng:]
    lses = [r[0] for r in l_refs]
    mx = lses[0]
    for g in range(1, ng):
        mx = jnp.maximum(mx, lses[g])
    es = [jnp.exp2(l - mx) for l in lses]
    tot = es[0]
    for g in range(1, ng):
        tot = tot + es[g]
    for g in range(ng):
        w = es[g] / tot
        hi = w.astype(BF16)
        lo = (w - hi.astype(F32)).astype(BF16)
        wx = (jnp.dot(hi, x_ref[...], preferred_element_type=F32)
              + jnp.dot(lo, x_ref[...], preferred_element_type=F32))
        out_ref[0, :, g * DIL_GW:(g + 1) * DIL_GW] = (o_refs[g][0] * wx).astype(BF16)


def _dil_attention(pb, tm=512):
    nb, seq, _ = pb.shape
    ng = len(DIL_CONFIGS)
    outs, lses = zip(*[_dil_group(pb, g) for g in range(ng)])
    spread = np.zeros((LANES, DIL_GW), np.float32)
    for jj in range(DIL_HPG):
        spread[jj, jj * HALF:(jj + 1) * HALF] = 1.0
    tm = min(tm, seq)
    return pl.pallas_call(
        _dil_mix_kernel,
        grid=(nb, seq // tm),
        in_specs=([pl.BlockSpec((1, tm, DIL_GW), lambda b, i: (b, i, 0))] * ng
                  + [pl.BlockSpec((1, tm, LANES), lambda b, i: (b, i, 0))] * ng
                  + [pl.BlockSpec((LANES, DIL_GW), lambda b, i: (0, 0))]),
        out_specs=pl.BlockSpec((1, tm, DIL_W), lambda b, i: (b, i, 0)),
        out_shape=jax.ShapeDtypeStruct((nb, seq, DIL_W), BF16),
        compiler_params=_cparams(("parallel", "parallel")),
        name="dil_mix",
    )(*outs, *lses, jnp.asarray(spread, BF16))


def _gelu_tanh(x):
    return 0.5 * x * (1.0 + jnp.tanh(math.sqrt(2.0 / math.pi) * (x + 0.044715 * (x * x * x))))


def _compress_kernel(rk_ref, rv_ref, w1_ref, w2_ref, pos_ref, kc_ref, vc_ref):
    nc = rk_ref.shape[1]
    rowi = lax.broadcasted_iota(jnp.int32, (nc, LANES), 0)
    for kv, (r_ref, o_ref) in enumerate(((rk_ref, kc_ref), (rv_ref, vc_ref))):
        r = r_ref[0]
        a = jnp.dot(r, w1_ref[kv, 0], preferred_element_type=F32)
        b = jnp.dot(r, w1_ref[kv, 1], preferred_element_type=F32)
        pt = (jnp.dot(pos_ref[kv, 0], w1_ref[kv, 0], preferred_element_type=F32)
              + jnp.dot(pos_ref[kv, 1], w1_ref[kv, 1], preferred_element_type=F32))[0:1]
        pre = a + pltpu.roll(b, nc - 1, 0) + pt
        out = jnp.dot(_gelu_tanh(pre).astype(BF16), w2_ref[kv], preferred_element_type=F32)
        o_ref[0] = jnp.where(rowi < nc - 1, out, 0.0).astype(BF16)


def _compress_weights(cmp_w1, cmp_w2, cmp_pos):
    half = NSA_CMP_LEN // 2
    w1 = cmp_w1.reshape(2, 2, half, HEAD_DIM, HEAD_DIM)
    z = jnp.zeros_like(w1)
    w1x = jnp.concatenate([jnp.concatenate([w1, z], -1), jnp.concatenate([z, w1], -1)], -2)
    w1x = w1x.reshape(2, 2, half * LANES, LANES).astype(BF16)
    z2 = jnp.zeros_like(cmp_w2)
    w2x = jnp.concatenate([jnp.concatenate([cmp_w2, z2], -1), jnp.concatenate([z2, cmp_w2], -1)], -2).astype(BF16)
    pos = cmp_pos.reshape(2, 2, half, 1, HEAD_DIM)
    posx = jnp.broadcast_to(pos, (2, 2, half, NSA_GROUPS, HEAD_DIM)).reshape(2, 2, 1, half * LANES)
    posx = jnp.broadcast_to(posx, (2, 2, 8, half * LANES)).astype(BF16)
    return w1x, w2x, posx


def _nsa_compress(rk, rv, w1x, w2x, posx):
    nb, nc, width = rk.shape
    full = lambda a: pl.BlockSpec(a.shape, lambda b: (0,) * a.ndim)
    return pl.pallas_call(
        _compress_kernel,
        grid=(nb,),
        in_specs=[pl.BlockSpec((1, nc, width), lambda b: (b, 0, 0)), pl.BlockSpec((1, nc, width), lambda b: (b, 0, 0)),
                  full(w1x), full(w2x), full(posx)],
        out_specs=[pl.BlockSpec((1, nc, LANES), lambda b: (b, 0, 0))] * 2,
        out_shape=[jax.ShapeDtypeStruct((nb, nc, LANES), BF16)] * 2,
        compiler_params=_cparams(("parallel",)),
        name="nsa_compress",
    )(rk, rv, w1x, w2x, posx)


def _cmp_to_sel_matrix(n_cmp_pad, n_cmp, n_sel):
    a = NSA_SEL_BLOCK // NSA_CMP_STRIDE
    b = NSA_CMP_LEN // NSA_CMP_STRIDE
    j = np.arange(n_sel)[:, None, None]
    idx = a * j + np.arange(a)[None, :, None] - np.arange(b)[None, None, :]
    jj = np.broadcast_to(j, idx.shape)
    ok = (idx >= 0) & (idx < n_cmp)
    m = np.zeros((n_cmp_pad, LANES), np.float32)
    np.add.at(m, (idx[ok], jj[ok]), 1.0)
    return m


def _nsa_kernel(q_ref, kc_ref, vc_ref, ka0_ref, ka1_ref, vs_ref, kw_ref, vw_ref, pf_ref, sh_ref, coef_ref, m2t_ref,
                t2_ref, gx_ref, o_ref, qs_ref, qa_ref, flag_ref, ow_ref, m_ref, l_ref, acc_ref,
                *, tq, tk, seq, n_sel, n_top):
    qi = pl.program_id(1)
    qs = qi * tq
    nc = kc_ref.shape[1]
    heads = [(g, j) for g in range(NSA_GROUPS) for j in range(NSA_HPG)]

    def slope_of(g, j):
        return _alibi_slope(g * NSA_HPG + j, NSA_HEADS)

    for n, (g, j) in enumerate(heads):
        qs_ref[n] = _pick_half(q_ref[0, :, j * LANES:(j + 1) * LANES], g)

    cidx = lax.broadcasted_iota(jnp.int32, (tq, nc), 1)
    dist_c = (qs + lax.broadcasted_iota(jnp.int32, (tq, nc), 0)) - (cidx * NSA_CMP_STRIDE + (NSA_CMP_LEN - 1))
    valid_c = (dist_c | ((nc - 2) - cidx)) >= 0
    dist_cf = dist_c.astype(F32)
    kc = kc_ref[0]
    vc = vc_ref[0]
    o_c = {}
    psum = [None] * NSA_GROUPS
    any_c = (qs + lax.broadcasted_iota(jnp.int32, (tq, 1), 0)) >= NSA_CMP_LEN - 1
    for n, (g, j) in enumerate(heads):
        s = jnp.where(valid_c, _qk(qs_ref[n], kc) - slope_of(g, j) * dist_cf, NEG)
        m = jnp.max(s, axis=1, keepdims=True)
        p = jnp.exp2(s - m)
        l = jnp.sum(p, axis=1, keepdims=True)
        pn = p * jnp.where(any_c, 1.0 / l, 0.0)
        o_c[(g, j)] = jnp.dot(pn.astype(BF16), vc, preferred_element_type=F32)
        psum[g] = pn if psum[g] is None else psum[g] + pn

    wr = min(tq, NSA_WIN_ROWS)
    back = -(-NSA_WINDOW // wr) * wr
    width = min(back + wr, seq)
    for rh in range(tq // wr):
        rows = slice(rh * wr, (rh + 1) * wr)
        qs_h = qs + rh * wr
        start = _window_start(qs_h, back, width, seq, wr)
        dist = ((qs_h - start) + lax.broadcasted_iota(jnp.int32, (wr, width), 0)
                - lax.broadcasted_iota(jnp.int32, (wr, width), 1))
        valid_w = (dist | ((NSA_WINDOW - 1) - dist)) >= 0
        distf = dist.astype(F32)
        kw = kw_ref[0, pl.ds(start, width), :]
        vw = vw_ref[0, pl.ds(start, width), :]
        for n, (g, j) in enumerate(heads):
            s = jnp.where(valid_w, _qk(qs_ref[n, rows, :], kw) - slope_of(g, j) * distf, NEG)
            ow_ref[n, rows, :], _ = _softmax_pv(s, vw)

    rsel = -(-n_sel // 8) * 8
    blk = lax.broadcasted_iota(jnp.int32, (rsel, tq), 0)
    blkf = blk.astype(F32)
    tpos = qs + lax.broadcasted_iota(jnp.int32, (rsel, tq), 1)
    cur = tpos >> 6
    forced = (blk == 0) | (blk == cur) | (blk == cur - 1)
    causal_b = blk * NSA_SEL_BLOCK <= tpos
    sel_b = []
    for g in range(NSA_GROUPS):
        ph = psum[g].astype(BF16)
        plo = (psum[g] - ph.astype(F32)).astype(BF16)
        imp = (_qk(m2t_ref[...], ph) + _qk(m2t_ref[...], plo))[0:rsel]
        score = jnp.where(causal_b, jnp.where(forced, NSA_FORCE_SCORE, imp), NEG)
        score = jnp.where(blk < n_sel, score, PICKED)
        sel = jnp.zeros((rsel, tq), F32)
        for _ in range(n_top):
            mx = jnp.max(score, axis=0, keepdims=True)
            idx = jnp.min(jnp.where(score == mx, blkf, float(LANES)), axis=0, keepdims=True)
            hit = blkf == idx
            sel = jnp.where(hit, 1.0, sel)
            score = jnp.where(hit, PICKED, score)
        sel_b.append(sel.astype(BF16))

    ones = jnp.ones((tq, LANES), BF16)
    lane = lax.broadcasted_iota(jnp.int32, (tq, LANES), 1)
    tposf = (qs + lax.broadcasted_iota(jnp.int32, (tq, LANES), 0)).astype(F32)
    for g in range(NSA_GROUPS):
        per_block = jnp.dot(sel_b[g], ones, preferred_element_type=F32)
        flag_ref[g] = jnp.dot(t2_ref[:, 0:rsel], per_block.astype(BF16), preferred_element_type=F32)
        sel_q = lax.dot_general(sel_b[g], sh_ref[g], (((0,), (0,)), ((), ())),
                                preferred_element_type=F32)
        mask_part = ((sel_q - 1.0) * MASK_BIG).astype(BF16)
        for j in range(NSA_HPG):
            n = g * NSA_HPG + j
            first = jnp.where(_half_mask((tq, LANES), g), q_ref[0, :, j * LANES:(j + 1) * LANES], mask_part)
            a = tposf * (-slope_of(g, j))
            a_hi = a.astype(BF16)
            r1 = a - a_hi.astype(F32)
            a_mid = r1.astype(BF16)
            a_lo = (r1 - a_mid.astype(F32)).astype(BF16)
            feats = jnp.where(lane == 0, a_hi, jnp.where(lane == 1, a_mid, jnp.where(lane == 2, a_lo,
                              coef_ref[n].astype(BF16))))
            qa_ref[n] = jnp.concatenate([first, feats], axis=1)
    _flash_reset(m_ref, l_ref, acc_ref)

    def slc_tile(ki, last):
        ks = pl.multiple_of(ki * tk, tk)
        for g, ka_ref in enumerate((ka0_ref, ka1_ref)):
            def group_update(g=g, ka_ref=ka_ref):
                ka = ka_ref[0, pl.ds(ks, tk), :]
                v = vs_ref[0, pl.ds(ks, tk), :]
                for j in range(NSA_HPG):
                    n = g * NSA_HPG + j
                    s = _qk(qa_ref[n], ka)
                    if last:
                        s = jnp.where(lax.broadcasted_iota(jnp.int32, (tq, tk), 0)
                                      >= lax.broadcasted_iota(jnp.int32, (tq, tk), 1), s, NEG)
                    _flash_update(n, s, None, v, m_ref, l_ref, acc_ref)
            if last:
                group_update()
            else:
                pl.when(flag_ref[g, pl.ds(ki, 1), :][0, 0] > 0.5)(group_update)

    def slc_body(ki, c):
        slc_tile(ki, False)
        return c

    lax.fori_loop(0, qi, slc_body, 0)
    slc_tile(qi, True)

    gl = jax.nn.sigmoid(pf_ref[0])
    g_hi = gl.astype(BF16)
    g_lo = (gl - g_hi.astype(F32)).astype(BF16)

    def gate(j, r):
        x = gx_ref[j * 3 + r]
        return jnp.dot(g_hi, x, preferred_element_type=F32) + jnp.dot(g_lo, x, preferred_element_type=F32)

    for j in range(NSA_HPG):
        n0, n1 = j, NSA_HPG + j
        comb = (gate(j, 0) * _merge_halves(o_c[(0, j)], o_c[(1, j)])
                + gate(j, 1) * _merge_halves(acc_ref[n0] / l_ref[n0], acc_ref[n1] / l_ref[n1])
                + gate(j, 2) * _merge_halves(ow_ref[n0], ow_ref[n1]))
        o_ref[0, :, j * LANES:(j + 1) * LANES] = comb.astype(BF16)


MASK_BIG = 2.0 ** 100


def _bf16_parts(x):
    parts = []
    r = np.float32(x)
    for _ in range(3):
        p = np.float32(np.asarray(r, dtype=BF16))
        parts.append(float(p))
        r = np.float32(r - p)
    return parts


def _nsa_key_features(k_slc, seq):
    nb = k_slc.shape[0]
    pos = np.arange(seq)
    onehot = (pos[:, None] // NSA_SEL_BLOCK == np.arange(HALF)[None, :]).astype(np.float32)
    feats = np.zeros((seq, LANES), np.float32)
    feats[:, 0:3] = 1.0
    feats[:, 3:6] = (pos // 64)[:, None]
    feats[:, 6:9] = (pos % 64)[:, None]
    oh = jnp.broadcast_to(jnp.asarray(onehot, BF16), (nb, seq, HALF))
    ft = jnp.broadcast_to(jnp.asarray(feats, BF16), (nb, seq, LANES))
    return (jnp.concatenate([k_slc[:, :, :HALF], oh, ft], axis=-1),
            jnp.concatenate([oh, k_slc[:, :, HALF:], ft], axis=-1))


def _nsa_attention(pb, pf, kc, vc, tq=512, tk=512):
    nb, seq, _ = pb.shape
    tq = min(tq, seq)
    tk = tq
    nc = kc.shape[1]
    n_sel = seq // NSA_SEL_BLOCK
    assert n_sel <= HALF and seq <= 64 * 64
    n_top = min(NSA_TOP_N, n_sel)
    n_cmp = seq // NSA_CMP_STRIDE - NSA_CMP_LEN // NSA_CMP_STRIDE + 1
    m2t = jnp.asarray(_cmp_to_sel_matrix(nc, n_cmp, n_sel).T, BF16)
    n_tiles = seq // tk
    tile_rows = -(-n_tiles // 8) * 8
    blocks_per_tile = tk // NSA_SEL_BLOCK
    t2 = jnp.asarray(np.arange(tile_rows)[:, None] == (np.arange(LANES)[None, :] // blocks_per_tile), BF16)
    gx = np.zeros((NSA_HPG * 3, LANES, LANES), np.float32)
    for j in range(NSA_HPG):
        for r in range(3):
            for g in range(NSA_GROUPS):
                gx[j * 3 + r, GATE_COL + (g * NSA_HPG + j) * 3 + r, g * HALF:(g + 1) * HALF] = 1.0
    gx = jnp.asarray(gx, BF16)
    rsel = -(-n_sel // 8) * 8
    shift = np.zeros((NSA_GROUPS, rsel, LANES), np.float32)
    for g in range(NSA_GROUPS):
        shift[g, np.arange(n_sel), (1 - g) * HALF + np.arange(n_sel)] = 1.0
    coef = np.zeros((NSA_HEADS, 1, LANES), np.float32)
    for n in range(NSA_HEADS):
        sl = _alibi_slope(n, NSA_HEADS)
        coef[n, 0, 3:6] = _bf16_parts(sl * 64.0)
        coef[n, 0, 6:9] = _bf16_parts(sl)
    ka0, ka1 = _nsa_key_features(pb[:, :, P_NSAKV + 2 * LANES:P_NSAKV + 3 * LANES], seq)
    w = NSA_HPG * LANES
    kvb = P_NSAKV // LANES
    kv_spec = lambda i: pl.BlockSpec((1, seq, LANES), lambda b, q, i=i: (b, 0, kvb + i))
    ka_spec = pl.BlockSpec((1, seq, 2 * LANES), lambda b, q: (b, 0, 0))
    return pl.pallas_call(
        functools.partial(_nsa_kernel, tq=tq, tk=tk, seq=seq, n_sel=n_sel, n_top=n_top),
        grid=(nb, seq // tq),
        in_specs=[
            pl.BlockSpec((1, tq, w), lambda b, i: (b, i, P_NSAQ // w)),
            pl.BlockSpec((1, nc, LANES), lambda b, i: (b, 0, 0)),
            pl.BlockSpec((1, nc, LANES), lambda b, i: (b, 0, 0)),
            ka_spec, ka_spec, kv_spec(3), kv_spec(4), kv_spec(5),
            pl.BlockSpec((1, tq, LANES), lambda b, i: (b, i, 0)),
            pl.BlockSpec((NSA_GROUPS, rsel, LANES), lambda b, i: (0, 0, 0)),
            pl.BlockSpec((NSA_HEADS, 1, LANES), lambda b, i: (0, 0, 0)),
            pl.BlockSpec((LANES, nc), lambda b, i: (0, 0)),
            pl.BlockSpec((tile_rows, LANES), lambda b, i: (0, 0)),
            pl.BlockSpec((NSA_HPG * 3, LANES, LANES), lambda b, i: (0, 0, 0)),
        ],
        out_specs=pl.BlockSpec((1, tq, w), lambda b, i: (b, i, 0)),
        out_shape=jax.ShapeDtypeStruct((nb, seq, w), BF16),
        scratch_shapes=([pltpu.VMEM((NSA_HEADS, tq, LANES), BF16), pltpu.VMEM((NSA_HEADS, tq, 2 * LANES), BF16),
                         pltpu.VMEM((NSA_GROUPS, tile_rows, LANES), F32),
                         pltpu.VMEM((NSA_HEADS, tq, LANES), F32)] + _flash_scratch(NSA_HEADS, tq)),
        compiler_params=_cparams(("parallel", "arbitrary")),
        name="nsa_attention",
    )(pb, kc, vc, ka0, ka1, pb, pb, pb, pf, jnp.asarray(shift, BF16), jnp.asarray(coef), m2t, t2, gx)


def _mixers(pb, pf, f_bias, cmp_w1, cmp_w2, cmp_pos, lam_vecs, subln_g, lam_init):
    nb, seq, _ = pb.shape
    cum = _forget_cumsum(pf, f_bias)
    cumrow = jnp.transpose(cum[:, :, :8], (0, 2, 1))
    o_fox = _fox_attention(pb, cum, cumrow)
    nc = seq // NSA_CMP_STRIDE
    rk = pb[:, :, P_NSAKV:P_NSAKV + LANES].reshape(nb, nc, NSA_CMP_STRIDE * LANES)
    rv = pb[:, :, P_NSAKV + LANES:P_NSAKV + 2 * LANES].reshape(nb, nc, NSA_CMP_STRIDE * LANES)
    kc, vc = _nsa_compress(rk, rv, *_compress_weights(cmp_w1, cmp_w2, cmp_pos))
    o_nsa = _nsa_attention(pb, pf, kc, vc)
    o_dil = _dil_attention(pb)
    o_dif = _diff_attention(pb, lam_vecs, subln_g, lam_init)
    return [o_dil, o_fox, o_nsa, o_dif]


def kernel(x, c, ada_w, ada_b, norm_mix_g, norm_ffn_g, w_in, fox_f_bias, nsa_cmp_w1, nsa_cmp_w2, nsa_cmp_pos,
           diff_lambda, diff_subln_g, w_out, ffn_w_gate, ffn_w_up, ffn_w_down, final_norm_g):
    nb, seq, d = x.shape
    depth = ada_w.shape[0]
    m = nb * seq
    mods = _modulation(c, ada_w, ada_b)
    w_p = _relayout_w_in(w_in, _proj_colmap())
    w_pf = w_p[:, :, P_FGATE:P_FGATE + LANES] + w_p[:, :, P_NGATE:P_NGATE + LANES]
    w_o = _gather_axis(w_out, _out_rowmap(), 1).astype(BF16)
    w_g, w_u, w_d = ffn_w_gate.astype(BF16), ffn_w_up.astype(BF16), ffn_w_down.astype(BF16)
    h = x.reshape(m, d)
    for layer in range(depth):
        sh1, sc1, g1, sh2, sc2, g2 = [mods[layer, :, i * d:(i + 1) * d].reshape(nb, 1, d) for i in range(6)]
        lam_init = 0.8 - 0.6 * math.exp(-0.3 * layer)
        pb, pf = _norm_proj(h, norm_mix_g[layer], sc1, sh1, w_p, layer, w_pf[layer], seq)
        parts = _mixers(pb.reshape(nb, seq, N_PB), pf.reshape(nb, seq, LANES), fox_f_bias[layer],
                        nsa_cmp_w1[layer], nsa_cmp_w2[layer], nsa_cmp_pos[layer],
                        diff_lambda[layer], diff_subln_g[layer], lam_init)
        h = _matmul_gated_residual([p.reshape(m, p.shape[-1]) for p in parts], w_o, layer, h, g1, seq,
                                   name="out_proj_res")
        act = _norm_ffn_up(h, norm_ffn_g[layer], sc2, sh2, w_g, w_u, layer, seq)
        h = _matmul_gated_residual([act], w_d, layer, h, g2, seq, name="ffn_down_res")
    return _final_norm(h, final_norm_g).reshape(nb, seq, d)
```

```python
import functools
import math

import numpy as np
import jax
import jax.numpy as jnp
from jax import lax
from jax.experimental import pallas as pl
from jax.experimental.pallas import tpu as pltpu

F32 = jnp.float32
BF16 = jnp.bfloat16

LANES = 128
HEAD_DIM = 64
HALF = HEAD_DIM
RMS_EPS = 1e-6
NEG = -1e30
PICKED = -3e38
LOG2E = math.log2(math.e)
Q_SCALE = HEAD_DIM ** -0.5 * LOG2E

FOX_HEADS = 7
NSA_HEADS = 8
NSA_GROUPS = 2
NSA_HPG = NSA_HEADS // NSA_GROUPS
NSA_CMP_LEN = 32
NSA_CMP_STRIDE = 16
NSA_SEL_BLOCK = 64
NSA_TOP_N = 16
NSA_WINDOW = 512
NSA_WIN_ROWS = 256
NSA_FORCE_SCORE = 1e4
DIL_CONFIGS = ((128, 1), (512, 4), (2048, 16))
DIL_HPG = 3
DIL_HEADS = DIL_HPG * len(DIL_CONFIGS)
DIFF_HEADS = 4
ALIBI_MAX_EXP = 8.0

_SPLITS = (
    ("fox_q", 448), ("fox_k", 448), ("fox_v", 448), ("fox_f", 7),
    ("nsa_q", 512), ("nsa_cmp_k", 128), ("nsa_cmp_v", 128), ("nsa_slc_k", 128),
    ("nsa_slc_v", 128), ("nsa_win_k", 128), ("nsa_win_v", 128), ("nsa_gate", 24),
    ("dil_q", 576), ("dil_k", 576), ("dil_v", 576),
    ("diff_q", 512), ("diff_k", 512), ("diff_v", 512),
)
_SRC = {}
_o = 0
for _n, _w in _SPLITS:
    _SRC[_n] = _o
    _o += _w
N_IN = _o

DIL_GW = 256
DIL_W = 3 * DIL_GW
P_FOXQ, P_FOXK, P_FOXV = 0, 512, 1024
P_NSAQ = 1536
P_NSAKV = 2048
P_DILQ, P_DILK, P_DILV = 2816, 3584, 4352
P_DIFQ, P_DIFK, P_DIFV = 5120, 5632, 6144
N_PB = 6656
GATE_COL = 8
RELAYOUT_TILE = 512
P_FGATE, P_NGATE = N_PB, N_PB + RELAYOUT_TILE
N_WP = N_PB + 2 * RELAYOUT_TILE


def _proj_colmap():
    pb = -np.ones(N_WP, np.int64)
    ar = np.arange
    for name, dst in (("fox_q", P_FOXQ), ("fox_k", P_FOXK), ("fox_v", P_FOXV)):
        pb[dst:dst + 448] = _SRC[name] + ar(448)
    for j in range(NSA_HPG):
        for g in range(NSA_GROUPS):
            d = P_NSAQ + j * LANES + g * HALF
            pb[d:d + HALF] = _SRC["nsa_q"] + (g * NSA_HPG + j) * HEAD_DIM + ar(HALF)
    for i, name in enumerate(("nsa_cmp_k", "nsa_cmp_v", "nsa_slc_k", "nsa_slc_v", "nsa_win_k", "nsa_win_v")):
        pb[P_NSAKV + i * LANES:P_NSAKV + (i + 1) * LANES] = _SRC[name] + ar(LANES)
    for name, dst in (("dil_q", P_DILQ), ("dil_k", P_DILK), ("dil_v", P_DILV)):
        for g in range(len(DIL_CONFIGS)):
            for jj in range(DIL_HPG):
                d = dst + (2 * g + jj // 2) * LANES + (jj % 2) * HALF
                pb[d:d + HALF] = _SRC[name] + (g * DIL_HPG + jj) * HEAD_DIM + ar(HALF)
    for name, dst in (("diff_q", P_DIFQ), ("diff_k", P_DIFK), ("diff_v", P_DIFV)):
        pb[dst:dst + 512] = _SRC[name] + ar(512)
    pb[P_FGATE:P_FGATE + FOX_HEADS] = _SRC["fox_f"] + ar(FOX_HEADS)
    pb[P_NGATE + GATE_COL:P_NGATE + GATE_COL + 24] = _SRC["nsa_gate"] + ar(24)
    return pb


O_DIL, O_FOX, O_NSA, O_DIF = 0, 768, 1280, 1792
N_O = 2304


def _out_rowmap():
    m = -np.ones(N_O, np.int64)
    ar = np.arange
    m[O_FOX:O_FOX + 448] = ar(448)
    for j in range(NSA_HPG):
        for g in range(NSA_GROUPS):
            d = O_NSA + j * LANES + g * HALF
            m[d:d + HALF] = 448 + (g * NSA_HPG + j) * HEAD_DIM + ar(HALF)
    for g in range(len(DIL_CONFIGS)):
        for jj in range(DIL_HPG):
            d = O_DIL + (2 * g + jj // 2) * LANES + (jj % 2) * HALF
            m[d:d + HALF] = 960 + (g * DIL_HPG + jj) * HEAD_DIM + ar(HALF)
    m[O_DIF:O_DIF + 512] = 1536 + ar(512)
    return m


def _gather_axis(x, idx, axis):
    pieces = []
    i = 0
    n = len(idx)
    while i < n:
        j = i + 1
        if idx[i] < 0:
            while j < n and idx[j] < 0:
                j += 1
            shape = list(x.shape)
            shape[axis] = j - i
            pieces.append(jnp.zeros(shape, x.dtype))
        else:
            while j < n and idx[j] == idx[j - 1] + 1:
                j += 1
            pieces.append(lax.slice_in_dim(x, int(idx[i]), int(idx[j - 1]) + 1, axis=axis))
        i = j
    return jnp.concatenate(pieces, axis=axis)


def _cparams(sem, vmem_mb=48):
    return pltpu.CompilerParams(dimension_semantics=sem, vmem_limit_bytes=vmem_mb * 1024 * 1024)


def _alibi_slope(k, n_heads):
    return float(2.0 ** (-ALIBI_MAX_EXP * (k + 1) / n_heads)) * LOG2E


def _src_col_scale(n_pad):
    sc = np.ones(n_pad, np.float32)
    for name, width in (("fox_q", 448), ("nsa_q", 512), ("dil_q", 576), ("diff_q", 512)):
        sc[_SRC[name]:_SRC[name] + width] = Q_SCALE
    return sc


def _relayout_plan(colmap, n_src, tile):
    n_tiles = len(colmap) // tile
    last = (n_src - 1) // tile
    win = np.zeros(n_tiles, np.int32)
    local = -np.ones((n_tiles, 1, tile), np.int32)
    for t in range(n_tiles):
        src = colmap[t * tile:(t + 1) * tile]
        used = src[src >= 0]
        a = min(int(used.min()) // tile, max(last - 1, 0))
        assert int(used.max()) < (a + 2) * tile, "tile sources exceed the two-block window"
        win[t] = a
        local[t, 0] = np.where(src >= 0, src - a * tile, -1)
    return win, local


def _relayout_kernel(win_ref, idx_ref, sa_ref, sb_ref, wa_ref, wb_ref, o_ref, *, n_src, tile):
    a = win_ref[pl.program_id(1)]
    idx = idx_ref[0]
    row = lax.broadcasted_iota(jnp.int32, (tile, tile), 0)
    col = lax.broadcasted_iota(jnp.int32, wa_ref.shape[1:], 1)
    acc = None
    for part, (w_ref, s_ref) in enumerate(((wa_ref, sa_ref), (wb_ref, sb_ref))):
        sel = jnp.where(row + part * tile == idx, 1.0, 0.0).astype(BF16)
        w = (jnp.where(col + (a + part) * tile < n_src, w_ref[0], 0.0) * s_ref[0]).astype(BF16)
        d = jnp.dot(w, sel, preferred_element_type=F32)
        acc = d if acc is None else acc + d
    o_ref[0] = acc.astype(BF16)


def _relayout_w_in(w_in, colmap, tile=RELAYOUT_TILE):
    depth, d, n_src = w_in.shape
    win, local = _relayout_plan(colmap, n_src, tile)
    n_tiles = len(win)
    n_blk = -(-n_src // tile)
    scale = jnp.asarray(_src_col_scale(n_blk * tile).reshape(n_blk, 1, tile))
    grid_spec = pltpu.PrefetchScalarGridSpec(
        num_scalar_prefetch=1,
        grid=(depth, n_tiles),
        in_specs=[
            pl.BlockSpec((1, 1, tile), lambda l, t, win: (t, 0, 0)),
            pl.BlockSpec((1, 1, tile), lambda l, t, win: (win[t], 0, 0)),
            pl.BlockSpec((1, 1, tile), lambda l, t, win: (win[t] + 1, 0, 0)),
            pl.BlockSpec((1, d, tile), lambda l, t, win: (l, 0, win[t])),
            pl.BlockSpec((1, d, tile), lambda l, t, win: (l, 0, win[t] + 1)),
        ],
        out_specs=pl.BlockSpec((1, d, tile), lambda l, t, win: (l, 0, t)),
    )
    return pl.pallas_call(
        functools.partial(_relayout_kernel, n_src=n_src, tile=tile),
        grid_spec=grid_spec,
        out_shape=jax.ShapeDtypeStruct((depth, d, n_tiles * tile), BF16),
        compiler_params=_cparams(("parallel", "arbitrary")),
        name="w_in_relayout",
    )(jnp.asarray(win), jnp.asarray(local), scale, scale, w_in, w_in)


def _mod_kernel(c_ref, w_ref, b_ref, o_ref):
    nb = c_ref.shape[0]
    tn = w_ref.shape[2]
    for b in range(nb):
        cb = c_ref[b]
        ca = cb * jax.nn.sigmoid(cb)
        for j in range(tn // LANES):
            sl = slice(j * LANES, (j + 1) * LANES)
            w = w_ref[0, :, sl]
            o_ref[0, b:b + 1, sl] = jnp.sum(w * ca, axis=0, keepdims=True) + b_ref[0, :, sl]


def _modulation(c, ada_w, ada_b, tn=512):
    depth, d, n = ada_w.shape
    nb = c.shape[0]
    c_rep = jnp.broadcast_to(c[:, :, None], (nb, d, LANES))
    return pl.pallas_call(
        _mod_kernel,
        grid=(depth, n // tn),
        in_specs=[
            pl.BlockSpec((nb, d, LANES), lambda l, j: (0, 0, 0)),
            pl.BlockSpec((1, d, tn), lambda l, j: (l, 0, j)),
            pl.BlockSpec((1, 1, tn), lambda l, j: (l, 0, j)),
        ],
        out_specs=pl.BlockSpec((1, nb, tn), lambda l, j: (l, 0, j)),
        out_shape=jax.ShapeDtypeStruct((depth, nb, n), F32),
        compiler_params=_cparams(("parallel", "parallel")),
        name="adaln_mod",
    )(c_rep, ada_w, ada_b.reshape(depth, 1, n))


NORM_CHUNKS = 4


def _norm_mod(h_ref, g_ref, sc_ref, sh_ref, rows=slice(None)):
    x = h_ref[rows, :]
    ms = jnp.mean(x * x, axis=-1, keepdims=True)
    y = x * lax.rsqrt(ms + RMS_EPS) * g_ref[...]
    return (y * (1.0 + sc_ref[0]) + sh_ref[0]).astype(BF16)


def _row_chunks(n):
    step = n // NORM_CHUNKS
    return [slice(c * step, (c + 1) * step) for c in range(NORM_CHUNKS)]


def _proj_kernel(h_ref, g_ref, sc_ref, sh_ref, w_ref, wf_ref, pb_ref, pf_ref, u_ref):
    first = pl.program_id(1) == 0

    @pl.when(first)
    def _():
        for rows in _row_chunks(h_ref.shape[0]):
            u = _norm_mod(h_ref, g_ref, sc_ref, sh_ref, rows)
            u_ref[rows, :] = u
            pf_ref[rows, :] = jnp.dot(u, wf_ref[...], preferred_element_type=F32)
            pb_ref[rows, :] = jnp.dot(u, w_ref[0], preferred_element_type=F32).astype(BF16)

    @pl.when(jnp.logical_not(first))
    def _():
        pb_ref[...] = jnp.dot(u_ref[...], w_ref[0], preferred_element_type=F32).astype(BF16)


def _norm_proj(h, gain, sc, sh, w_all, layer, wf, seq, tm=1024, tn=512):
    m, d = h.shape
    per = seq // tm
    return pl.pallas_call(
        _proj_kernel,
        grid=(m // tm, N_PB // tn),
        in_specs=[
            pl.BlockSpec((tm, d), lambda i, j: (i, 0)),
            pl.BlockSpec((1, d), lambda i, j: (0, 0)),
            pl.BlockSpec((1, 1, d), lambda i, j: (i // per, 0, 0)),
            pl.BlockSpec((1, 1, d), lambda i, j: (i // per, 0, 0)),
            pl.BlockSpec((1, d, tn), lambda i, j: (layer, 0, j)),
            pl.BlockSpec((d, LANES), lambda i, j: (0, 0)),
        ],
        out_specs=[
            pl.BlockSpec((tm, tn), lambda i, j: (i, j)),
            pl.BlockSpec((tm, LANES), lambda i, j: (i, 0)),
        ],
        out_shape=[jax.ShapeDtypeStruct((m, N_PB), BF16), jax.ShapeDtypeStruct((m, LANES), F32)],
        scratch_shapes=[pltpu.VMEM((tm, d), BF16)],
        compiler_params=_cparams(("parallel", "arbitrary")),
        name="norm_in_proj",
    )(h, gain.reshape(1, d), sc, sh, w_all, wf)


def _ffn_up_kernel(h_ref, g_ref, sc_ref, sh_ref, wg_ref, wu_ref, a_ref, u_ref):
    def swiglu(u):
        gate = jnp.dot(u, wg_ref[0], preferred_element_type=F32)
        up = jnp.dot(u, wu_ref[0], preferred_element_type=F32)
        return (gate * jax.nn.sigmoid(gate) * up).astype(BF16)

    first = pl.program_id(1) == 0

    @pl.when(first)
    def _():
        for rows in _row_chunks(h_ref.shape[0]):
            u = _norm_mod(h_ref, g_ref, sc_ref, sh_ref, rows)
            u_ref[rows, :] = u
            a_ref[rows, :] = swiglu(u)

    @pl.when(jnp.logical_not(first))
    def _():
        a_ref[...] = swiglu(u_ref[...])


def _norm_ffn_up(h, gain, sc, sh, wg, wu, layer, seq, tm=1024, tn=512):
    m, d = h.shape
    n = wg.shape[2]
    per = seq // tm
    return pl.pallas_call(
        _ffn_up_kernel,
        grid=(m // tm, n // tn),
        in_specs=[
            pl.BlockSpec((tm, d), lambda i, j: (i, 0)),
            pl.BlockSpec((1, d), lambda i, j: (0, 0)),
            pl.BlockSpec((1, 1, d), lambda i, j: (i // per, 0, 0)),
            pl.BlockSpec((1, 1, d), lambda i, j: (i // per, 0, 0)),
            pl.BlockSpec((1, d, tn), lambda i, j: (layer, 0, j)),
            pl.BlockSpec((1, d, tn), lambda i, j: (layer, 0, j)),
        ],
        out_specs=pl.BlockSpec((tm, tn), lambda i, j: (i, j)),
        out_shape=jax.ShapeDtypeStruct((m, n), BF16),
        scratch_shapes=[pltpu.VMEM((tm, d), BF16)],
        compiler_params=_cparams(("parallel", "arbitrary")),
        name="norm_ffn_up",
    )(h, gain.reshape(1, d), sc, sh, wg, wu)


def _mm_res_kernel(*refs, widths):
    n = len(widths)
    a_refs, (w_ref, h_ref, g_ref, o_ref) = refs[:n], refs[n:]
    acc = None
    off = 0
    for a_ref, width in zip(a_refs, widths):
        d = jnp.dot(a_ref[...], w_ref[0, off:off + width, :], preferred_element_type=F32)
        acc = d if acc is None else acc + d
        off += width
    o_ref[...] = h_ref[...] + g_ref[0] * acc


def _matmul_gated_residual(a_parts, w, layer, h, gate, seq, tm=1024, tn=512, name="mm_res"):
    m = h.shape[0]
    widths = tuple(a.shape[1] for a in a_parts)
    k, n = w.shape[1], w.shape[2]
    assert sum(widths) == k
    per = seq // tm
    block_bytes = tm * k * 2 + k * tn * 2 + 2 * tm * tn * 4
    vmem_mb = (2 * block_bytes + 4 * tm * tn * 4) // (1024 * 1024) + 1
    return pl.pallas_call(
        functools.partial(_mm_res_kernel, widths=widths),
        grid=(m // tm, n // tn),
        in_specs=[pl.BlockSpec((tm, wd), lambda i, j: (i, 0)) for wd in widths] + [
            pl.BlockSpec((1, k, tn), lambda i, j: (layer, 0, j)),
            pl.BlockSpec((tm, tn), lambda i, j: (i, j)),
            pl.BlockSpec((1, 1, tn), lambda i, j: (i // per, 0, j)),
        ],
        out_specs=pl.BlockSpec((tm, tn), lambda i, j: (i, j)),
        out_shape=jax.ShapeDtypeStruct((m, n), F32),
        compiler_params=_cparams(("parallel", "arbitrary"), vmem_mb=vmem_mb),
        name=name,
    )(*a_parts, w, h, gate)


def _final_norm_kernel(h_ref, g_ref, o_ref):
    x = h_ref[...]
    ms = jnp.mean(x * x, axis=-1, keepdims=True)
    o_ref[...] = x * lax.rsqrt(ms + RMS_EPS) * g_ref[...]


def _final_norm(h, gain, tm=512):
    m, d = h.shape
    return pl.pallas_call(
        _final_norm_kernel,
        grid=(m // tm,),
        in_specs=[pl.BlockSpec((tm, d), lambda i: (i, 0)), pl.BlockSpec((1, d), lambda i: (0, 0))],
        out_specs=pl.BlockSpec((tm, d), lambda i: (i, 0)),
        out_shape=jax.ShapeDtypeStruct((m, d), F32),
        compiler_params=_cparams(("parallel",)),
        name="final_norm",
    )(h, gain.reshape(1, d))


def _half_mask(shape, half):
    lane = lax.broadcasted_iota(jnp.int32, shape, 1)
    return (lane < HALF) if half == 0 else (lane >= HALF)


def _pick_half(qp, half):
    return jnp.where(_half_mask(qp.shape, half), qp, jnp.zeros_like(qp))


def _lane_tile(x, n):
    return x if n == 1 else jnp.concatenate([x] * n, axis=1)


def _qk(qh, k):
    return lax.dot_general(qh, k, (((1,), (1,)), ((), ())), preferred_element_type=F32)


def _merge_halves(lo, hi):
    if hi is None:
        hi = jnp.zeros_like(lo)
    return jnp.where(_half_mask(lo.shape, 0), lo, hi)


def _flash_scratch(n_heads, tq):
    return [pltpu.VMEM((n_heads, tq, LANES), F32), pltpu.VMEM((n_heads, tq, LANES), F32),
            pltpu.VMEM((n_heads, tq, LANES), F32)]


def _flash_reset(m_ref, l_ref, acc_ref):
    m_ref[...] = jnp.full(m_ref.shape, NEG, F32)
    l_ref[...] = jnp.zeros(l_ref.shape, F32)
    acc_ref[...] = jnp.zeros(acc_ref.shape, F32)


def _flash_update(h, s, valid, v, m_ref, l_ref, acc_ref):
    m_old = m_ref[h]
    m_new = jnp.maximum(m_old, jnp.max(s, axis=1, keepdims=True))
    alpha = jnp.exp2(m_old - m_new)
    p = jnp.exp2(s - _lane_tile(m_new, s.shape[1] // LANES))
    if valid is not None:
        p = jnp.where(valid, p, 0.0)
    l_ref[h] = alpha * l_ref[h] + jnp.sum(p, axis=1, keepdims=True)
    acc_ref[h] = alpha * acc_ref[h] + jnp.dot(p.astype(BF16), v, preferred_element_type=F32)
    m_ref[h] = m_new


def _softmax_pv(s, v):
    m = jnp.max(s, axis=1, keepdims=True)
    p = jnp.exp2(s - m)
    l = jnp.sum(p, axis=1, keepdims=True)
    return jnp.dot(p.astype(BF16), v, preferred_element_type=F32) / l, m + jnp.log2(l)


def _causal_steps(qi, t, step):
    def wide(k2, c):
        step(pl.multiple_of(k2 * (2 * t), 2 * t), 2 * t, False)
        return c

    lax.fori_loop(0, lax.shift_right_logical(qi, 1), wide, 0)

    @pl.when((qi & 1) == 1)
    def _():
        step(pl.multiple_of((qi - 1) * t, t), t, False)

    step(pl.multiple_of(qi * t, t), t, True)


def _bf16_parts(x):
    parts = []
    r = np.float32(x)
    for _ in range(3):
        p = np.float32(np.asarray(r, dtype=BF16))
        parts.append(float(p))
        r = np.float32(r - p)
    return parts


def _alibi_key_lanes(seq):
    assert seq <= 64 * 64
    pos = np.arange(seq)
    feats = np.zeros((seq, LANES), np.float32)
    feats[:, 0:3] = 1.0
    feats[:, 3:6] = (pos // 64)[:, None]
    feats[:, 6:9] = (pos % 64)[:, None]
    return feats


def _alibi_coef_lanes(slopes):
    coef = np.zeros((len(slopes), 1, LANES), np.float32)
    for n, sl in enumerate(slopes):
        coef[n, 0, 3:6] = _bf16_parts(sl * 64.0)
        coef[n, 0, 6:9] = _bf16_parts(sl)
    return coef


def _alibi_query_lanes(tposf, slope, coef_row):
    lane = lax.broadcasted_iota(jnp.int32, tposf.shape, 1)
    a = tposf * (-slope)
    a_hi = a.astype(BF16)
    r1 = a - a_hi.astype(F32)
    a_mid = r1.astype(BF16)
    a_lo = (r1 - a_mid.astype(F32)).astype(BF16)
    return jnp.where(lane == 0, a_hi, jnp.where(lane == 1, a_mid, jnp.where(lane == 2, a_lo, coef_row.astype(BF16))))


def _window_start(qs, back, width, seq, align):
    return pl.multiple_of(jnp.clip(qs - back, 0, seq - width), align)


def _cumsum_kernel(pf_ref, fb_ref, o_ref, *, chunk):
    seq = pf_ref.shape[1]
    r = lax.broadcasted_iota(jnp.int32, (chunk, chunk), 0)
    c = lax.broadcasted_iota(jnp.int32, (chunk, chunk), 1)
    tri = jnp.where(r >= c, 1.0, 0.0).astype(BF16)

    def body(i, carry):
        st = pl.multiple_of(i * chunk, chunk)
        x = pf_ref[0, pl.ds(st, chunk), :] + fb_ref[...]
        ls = jnp.minimum(x, 0.0) - jnp.log1p(jnp.exp(-jnp.abs(x)))
        hi = ls.astype(BF16)
        r1 = ls - hi.astype(F32)
        mid = r1.astype(BF16)
        lo = (r1 - mid.astype(F32)).astype(BF16)
        cs = (jnp.dot(tri, hi, preferred_element_type=F32)
              + jnp.dot(tri, mid, preferred_element_type=F32)
              + jnp.dot(tri, lo, preferred_element_type=F32))
        out = cs + carry
        o_ref[0, pl.ds(st, chunk), :] = out * LOG2E
        return out[chunk - 1:chunk, :]

    lax.fori_loop(0, seq // chunk, body, jnp.zeros((1, LANES), F32))


def _forget_cumsum(pf, f_bias, chunk=256):
    nb, seq, _ = pf.shape
    fb = jnp.zeros((1, LANES), F32).at[0, :FOX_HEADS].set(f_bias.astype(F32))
    return pl.pallas_call(
        functools.partial(_cumsum_kernel, chunk=chunk),
        grid=(nb,),
        in_specs=[pl.BlockSpec((1, seq, LANES), lambda b: (b, 0, 0)), pl.BlockSpec((1, LANES), lambda b: (0, 0))],
        out_specs=pl.BlockSpec((1, seq, LANES), lambda b: (b, 0, 0)),
        out_shape=jax.ShapeDtypeStruct((nb, seq, LANES), F32),
        compiler_params=_cparams(("parallel",)),
        name="forget_cumsum",
    )(pf, fb)


def _fox_kernel(q_ref, k_ref, v_ref, cc_ref, cr_ref, o_ref, qs_ref, cq_ref, m_ref, l_ref, acc_ref, *, t):
    qi = pl.program_id(1)
    _flash_reset(m_ref, l_ref, acc_ref)
    for h in range(FOX_HEADS):
        qs_ref[h] = _pick_half(q_ref[0, :, (h // 2) * LANES:(h // 2 + 1) * LANES], h % 2)
        cq_ref[h] = jnp.broadcast_to(cc_ref[0, :, h:h + 1], (t, LANES))

    def step(ks, width, masked):
        if masked:
            causal = (lax.broadcasted_iota(jnp.int32, (t, t), 0) >= lax.broadcasted_iota(jnp.int32, (t, t), 1))
        for h in range(FOX_HEADS):
            sl = slice((h // 2) * LANES, (h // 2 + 1) * LANES)
            k = k_ref[0, pl.ds(ks, width), sl]
            v = v_ref[0, pl.ds(ks, width), sl]
            s = ((_qk(qs_ref[h], k) + _lane_tile(cq_ref[h], width // LANES))
                 - cr_ref[0, h:h + 1, pl.ds(ks, width)])
            if masked:
                s = jnp.where(causal, s, NEG)
            _flash_update(h, s, None, v, m_ref, l_ref, acc_ref)

    _causal_steps(qi, t, step)
    for p in range(4):
        lo = acc_ref[2 * p] / l_ref[2 * p]
        hi = acc_ref[2 * p + 1] / l_ref[2 * p + 1] if 2 * p + 1 < FOX_HEADS else None
        o_ref[0, :, p * LANES:(p + 1) * LANES] = _merge_halves(lo, hi).astype(BF16)


def _fox_attention(pb, cum, cumrow, t=512):
    nb, seq, _ = pb.shape
    t = min(t, seq)
    w = 4 * LANES
    return pl.pallas_call(
        functools.partial(_fox_kernel, t=t),
        grid=(nb, seq // t),
        in_specs=[
            pl.BlockSpec((1, t, w), lambda b, i: (b, i, P_FOXQ // w)),
            pl.BlockSpec((1, seq, w), lambda b, i: (b, 0, P_FOXK // w)),
            pl.BlockSpec((1, seq, w), lambda b, i: (b, 0, P_FOXV // w)),
            pl.BlockSpec((1, t, LANES), lambda b, i: (b, i, 0)),
            pl.BlockSpec((1, 8, seq), lambda b, i: (b, 0, 0)),
        ],
        out_specs=pl.BlockSpec((1, t, w), lambda b, i: (b, i, 0)),
        out_shape=jax.ShapeDtypeStruct((nb, seq, w), BF16),
        scratch_shapes=([pltpu.VMEM((FOX_HEADS, t, LANES), BF16), pltpu.VMEM((FOX_HEADS, t, LANES), F32)]
                        + _flash_scratch(FOX_HEADS, t)),
        compiler_params=_cparams(("parallel", "arbitrary")),
        name="fox_attention",
    )(pb, pb, pb, cum, cumrow)


def _diff_kernel(q_ref, k_ref, v_ref, coef_ref, lam_ref, g_ref, o_ref, qs_ref, m_ref, l_ref, acc_ref,
                 *, t, lam_init):
    qi = pl.program_id(1)
    n_slots = 2 * DIFF_HEADS
    _flash_reset(m_ref, l_ref, acc_ref)
    tposf = (qi * t + lax.broadcasted_iota(jnp.int32, (t, LANES), 0)).astype(F32)
    for n in range(n_slots):
        hd = n // 2
        qs_ref[n] = jnp.concatenate(
            [_pick_half(q_ref[0, :, hd * LANES:(hd + 1) * LANES], n % 2),
             _alibi_query_lanes(tposf, _alibi_slope(hd, DIFF_HEADS), coef_ref[hd])], axis=1)

    def step(ks, width, masked):
        if masked:
            causal = (lax.broadcasted_iota(jnp.int32, (t, t), 0) >= lax.broadcasted_iota(jnp.int32, (t, t), 1))
        for n in range(n_slots):
            hd = n // 2
            sl = slice(hd * LANES, (hd + 1) * LANES)
            k = k_ref[0, pl.ds(ks, width), 2 * hd * LANES:2 * (hd + 1) * LANES]
            v = v_ref[0, pl.ds(ks, width), sl]
            s = _qk(qs_ref[n], k)
            if masked:
                s = jnp.where(causal, s, NEG)
            _flash_update(n, s, None, v, m_ref, l_ref, acc_ref)

    _causal_steps(qi, t, step)
    lv = lam_ref[...]
    lam = (jnp.exp(jnp.sum(lv[0:1] * lv[1:2], axis=1, keepdims=True))
           - jnp.exp(jnp.sum(lv[2:3] * lv[3:4], axis=1, keepdims=True)) + lam_init)
    for hd in range(DIFF_HEADS):
        o = acc_ref[2 * hd] / l_ref[2 * hd] - lam * (acc_ref[2 * hd + 1] / l_ref[2 * hd + 1])
        ms = jnp.mean(o * o, axis=-1, keepdims=True)
        y = o * lax.rsqrt(ms + RMS_EPS) * g_ref[...]
        o_ref[0, :, hd * LANES:(hd + 1) * LANES] = (y * (1.0 - lam_init)).astype(BF16)


def _diff_attention(pb, lam_vecs, subln_g, lam_init, t=512):
    nb, seq, _ = pb.shape
    t = min(t, seq)
    w = DIFF_HEADS * LANES
    ft = jnp.broadcast_to(jnp.asarray(_alibi_key_lanes(seq), BF16), (nb, seq, LANES))
    k_aug = jnp.concatenate(
        [x for hd in range(DIFF_HEADS)
         for x in (pb[:, :, P_DIFK + hd * LANES:P_DIFK + (hd + 1) * LANES], ft)], axis=-1)
    coef = jnp.asarray(_alibi_coef_lanes([_alibi_slope(hd, DIFF_HEADS) for hd in range(DIFF_HEADS)]))
    return pl.pallas_call(
        functools.partial(_diff_kernel, t=t, lam_init=lam_init),
        grid=(nb, seq // t),
        in_specs=[
            pl.BlockSpec((1, t, w), lambda b, i: (b, i, P_DIFQ // w)),
            pl.BlockSpec((1, seq, 2 * w), lambda b, i: (b, 0, 0)),
            pl.BlockSpec((1, seq, w), lambda b, i: (b, 0, P_DIFV // w)),
            pl.BlockSpec((DIFF_HEADS, 1, LANES), lambda b, i: (0, 0, 0)),
            pl.BlockSpec((4, HEAD_DIM), lambda b, i: (0, 0)),
            pl.BlockSpec((1, LANES), lambda b, i: (0, 0)),
        ],
        out_specs=pl.BlockSpec((1, t, w), lambda b, i: (b, i, 0)),
        out_shape=jax.ShapeDtypeStruct((nb, seq, w), BF16),
        scratch_shapes=[pltpu.VMEM((2 * DIFF_HEADS, t, 2 * LANES), BF16)] + _flash_scratch(2 * DIFF_HEADS, t),
        compiler_params=_cparams(("parallel", "arbitrary"), vmem_mb=56),
        name="diff_attention",
    )(pb, k_aug, pb, coef, lam_vecs.astype(F32), subln_g.astype(F32).reshape(1, LANES))


DIL_SUBWINDOW = 128


def _dil_group_kernel(q_ref, k_ref, v_ref, o_ref, lse_ref, *, tq, sub_len, dilation, head0):
    qs = pl.program_id(2) * tq
    width = min(2 * tq, sub_len)
    start = _window_start(qs, tq, width, sub_len, tq)
    dist = ((qs - start) + lax.broadcasted_iota(jnp.int32, (tq, width), 0)
            - lax.broadcasted_iota(jnp.int32, (tq, width), 1))
    valid = (dist | (DIL_SUBWINDOW - dist)) >= 0
    distf = dist.astype(F32)
    lane = lax.broadcasted_iota(jnp.int32, (tq, LANES), 1)
    lse_blk = jnp.zeros((tq, LANES), F32)
    outs = []
    for jj in range(DIL_HPG):
        sl = slice((jj // 2) * LANES, (jj // 2 + 1) * LANES)
        slope = _alibi_slope(head0 + jj, DIL_HEADS) * dilation
        k = k_ref[0, pl.ds(start, width), sl]
        v = v_ref[0, pl.ds(start, width), sl]
        s = jnp.where(valid, _qk(_pick_half(q_ref[0, :, sl], jj % 2), k) - slope * distf, NEG)
        o, lse = _softmax_pv(s, v)
        outs.append(o)
        lse_blk = jnp.where(lane == jj, lse, lse_blk)
    o_ref[0, :, 0:LANES] = _merge_halves(outs[0], outs[1])
    o_ref[0, :, LANES:2 * LANES] = _merge_halves(outs[2], None)
    lse_ref[0] = lse_blk


def _dil_group(pb, g, tq=256):
    nb, seq, n_pb = pb.shape
    window, dilation = DIL_CONFIGS[g]
    assert window == DIL_SUBWINDOW * dilation
    sub_len = seq // dilation
    tq = min(tq, sub_len)
    gw = DIL_GW
    if dilation == 1:
        views = (pb, pb, pb)
        cols = [base // gw + g for base in (P_DILQ, P_DILK, P_DILV)]
        per_tok = 0
    else:
        views = tuple(lax.slice_in_dim(pb, base + g * gw, base + (g + 1) * gw, axis=2)
                      .reshape(nb, sub_len, dilation * gw) for base in (P_DILQ, P_DILK, P_DILV))
        cols = [0, 0, 0]
        per_tok = 1
    o, lse = pl.pallas_call(
        functools.partial(_dil_group_kernel, tq=tq, sub_len=sub_len, dilation=dilation, head0=g * DIL_HPG),
        grid=(nb, dilation, sub_len // tq),
        in_specs=[
            pl.BlockSpec((1, tq, gw), lambda b, r, i: (b, i, r * per_tok + cols[0])),
            pl.BlockSpec((1, sub_len, gw), lambda b, r, i: (b, 0, r * per_tok + cols[1])),
            pl.BlockSpec((1, sub_len, gw), lambda b, r, i: (b, 0, r * per_tok + cols[2])),
        ],
        out_specs=[pl.BlockSpec((1, tq, gw), lambda b, r, i: (b, i, r)),
                   pl.BlockSpec((1, tq, LANES), lambda b, r, i: (b, i, r))],
        out_shape=[jax.ShapeDtypeStruct((nb, sub_len, dilation * gw), F32),
                   jax.ShapeDtypeStruct((nb, sub_len, dilation * LANES), F32)],
        compiler_params=_cparams(("parallel", "parallel", "arbitrary")),
        name=f"dil_group{g}",
    )(*views)
    return o.reshape(nb, seq, gw), lse.reshape(nb, seq, LANES)


def _dil_mix_kernel(*refs):
    ng = len(DIL_CONFIGS)
    o_refs, l_refs, (x_ref, out_ref) = refs[:ng], refs[ng:2 * ng], refs[2 * ng:]
    lses = [r[0] for r in l_refs]
    mx = lses[0]
    for g in range(1, ng):
        mx = jnp.maximum(mx, lses[g])
    es = [jnp.exp2(l - mx) for l in lses]
    tot = es[0]
    for g in range(1, ng):
        tot = tot + es[g]
    for g in range(ng):
        w = es[g] / tot
        hi = w.astype(BF16)
        lo = (w - hi.astype(F32)).astype(BF16)
        wx = (jnp.dot(hi, x_ref[...], preferred_element_type=F32)
              + jnp.dot(lo, x_ref[...], preferred_element_type=F32))
        out_ref[0, :, g * DIL_GW:(g + 1) * DIL_GW] = (o_refs[g][0] * wx).astype(BF16)


def _dil_attention(pb, tm=512):
    nb, seq, _ = pb.shape
    ng = len(DIL_CONFIGS)
    outs, lses = zip(*[_dil_group(pb, g) for g in range(ng)])
    spread = np.zeros((LANES, DIL_GW), np.float32)
    for jj in range(DIL_HPG):
        spread[jj, jj * HALF:(jj + 1) * HALF] = 1.0
    tm = min(tm, seq)
    return pl.pallas_call(
        _dil_mix_kernel,
        grid=(nb, seq // tm),
        in_specs=([pl.BlockSpec((1, tm, DIL_GW), lambda b, i: (b, i, 0))] * ng
                  + [pl.BlockSpec((1, tm, LANES), lambda b, i: (b, i, 0))] * ng
                  + [pl.BlockSpec((LANES, DIL_GW), lambda b, i: (0, 0))]),
        out_specs=pl.BlockSpec((1, tm, DIL_W), lambda b, i: (b, i, 0)),
        out_shape=jax.ShapeDtypeStruct((nb, seq, DIL_W), BF16),
        compiler_params=_cparams(("parallel", "parallel")),
        name="dil_mix",
    )(*outs, *lses, jnp.asarray(spread, BF16))


def _gelu_tanh(x):
    return 0.5 * x * (1.0 + jnp.tanh(math.sqrt(2.0 / math.pi) * (x + 0.044715 * (x * x * x))))


def _compress_kernel(rk_ref, rv_ref, w1_ref, w2_ref, pos_ref, kc_ref, vc_ref):
    nc = rk_ref.shape[1]
    rowi = lax.broadcasted_iota(jnp.int32, (nc, LANES), 0)
    for kv, (r_ref, o_ref) in enumerate(((rk_ref, kc_ref), (rv_ref, vc_ref))):
        r = r_ref[0]
        a = jnp.dot(r, w1_ref[kv, 0], preferred_element_type=F32)
        b = jnp.dot(r, w1_ref[kv, 1], preferred_element_type=F32)
        pt = (jnp.dot(pos_ref[kv, 0], w1_ref[kv, 0], preferred_element_type=F32)
              + jnp.dot(pos_ref[kv, 1], w1_ref[kv, 1], preferred_element_type=F32))[0:1]
        pre = a + pltpu.roll(b, nc - 1, 0) + pt
        out = jnp.dot(_gelu_tanh(pre).astype(BF16), w2_ref[kv], preferred_element_type=F32)
        o_ref[0] = jnp.where(rowi < nc - 1, out, 0.0).astype(BF16)


def _compress_weights(cmp_w1, cmp_w2, cmp_pos):
    half = NSA_CMP_LEN // 2
    w1 = cmp_w1.reshape(2, 2, half, HEAD_DIM, HEAD_DIM)
    z = jnp.zeros_like(w1)
    w1x = jnp.concatenate([jnp.concatenate([w1, z], -1), jnp.concatenate([z, w1], -1)], -2)
    w1x = w1x.reshape(2, 2, half * LANES, LANES).astype(BF16)
    z2 = jnp.zeros_like(cmp_w2)
    w2x = jnp.concatenate([jnp.concatenate([cmp_w2, z2], -1), jnp.concatenate([z2, cmp_w2], -1)], -2).astype(BF16)
    pos = cmp_pos.reshape(2, 2, half, 1, HEAD_DIM)
    posx = jnp.broadcast_to(pos, (2, 2, half, NSA_GROUPS, HEAD_DIM)).reshape(2, 2, 1, half * LANES)
    posx = jnp.broadcast_to(posx, (2, 2, 8, half * LANES)).astype(BF16)
    return w1x, w2x, posx


def _nsa_compress(rk, rv, w1x, w2x, posx):
    nb, nc, width = rk.shape
    full = lambda a: pl.BlockSpec(a.shape, lambda b: (0,) * a.ndim)
    return pl.pallas_call(
        _compress_kernel,
        grid=(nb,),
        in_specs=[pl.BlockSpec((1, nc, width), lambda b: (b, 0, 0)), pl.BlockSpec((1, nc, width), lambda b: (b, 0, 0)),
                  full(w1x), full(w2x), full(posx)],
        out_specs=[pl.BlockSpec((1, nc, LANES), lambda b: (b, 0, 0))] * 2,
        out_shape=[jax.ShapeDtypeStruct((nb, nc, LANES), BF16)] * 2,
        compiler_params=_cparams(("parallel",)),
        name="nsa_compress",
    )(rk, rv, w1x, w2x, posx)


def _cmp_to_sel_matrix(n_cmp_pad, n_cmp, n_sel):
    a = NSA_SEL_BLOCK // NSA_CMP_STRIDE
    b = NSA_CMP_LEN // NSA_CMP_STRIDE
    j = np.arange(n_sel)[:, None, None]
    idx = a * j + np.arange(a)[None, :, None] - np.arange(b)[None, None, :]
    jj = np.broadcast_to(j, idx.shape)
    ok = (idx >= 0) & (idx < n_cmp)
    m = np.zeros((n_cmp_pad, LANES), np.float32)
    np.add.at(m, (idx[ok], jj[ok]), 1.0)
    return m


def _nsa_kernel(q_ref, kc_ref, vc_ref, ka0_ref, ka1_ref, vs_ref, kw_ref, vw_ref, pf_ref, sh_ref, coef_ref, m2t_ref,
                t2_ref, gx_ref, o_ref, qs_ref, qa_ref, flag_ref, ow_ref, m_ref, l_ref, acc_ref,
                *, tq, tk, seq, n_sel, n_top):
    qi = pl.program_id(1)
    qs = qi * tq
    nc = kc_ref.shape[1]
    heads = [(g, j) for g in range(NSA_GROUPS) for j in range(NSA_HPG)]

    def slope_of(g, j):
        return _alibi_slope(g * NSA_HPG + j, NSA_HEADS)

    for n, (g, j) in enumerate(heads):
        qs_ref[n] = _pick_half(q_ref[0, :, j * LANES:(j + 1) * LANES], g)

    cidx = lax.broadcasted_iota(jnp.int32, (tq, nc), 1)
    dist_c = (qs + lax.broadcasted_iota(jnp.int32, (tq, nc), 0)) - (cidx * NSA_CMP_STRIDE + (NSA_CMP_LEN - 1))
    valid_c = (dist_c | ((nc - 2) - cidx)) >= 0
    dist_cf = dist_c.astype(F32)
    kc = kc_ref[0]
    vc = vc_ref[0]
    o_c = {}
    psum = [None] * NSA_GROUPS
    any_c = (qs + lax.broadcasted_iota(jnp.int32, (tq, 1), 0)) >= NSA_CMP_LEN - 1
    for n, (g, j) in enumerate(heads):
        s = jnp.where(valid_c, _qk(qs_ref[n], kc) - slope_of(g, j) * dist_cf, NEG)
        m = jnp.max(s, axis=1, keepdims=True)
        p = jnp.exp2(s - m)
        l = jnp.sum(p, axis=1, keepdims=True)
        pn = p * jnp.where(any_c, 1.0 / l, 0.0)
        o_c[(g, j)] = jnp.dot(pn.astype(BF16), vc, preferred_element_type=F32)
        psum[g] = pn if psum[g] is None else psum[g] + pn

    wr = min(tq, NSA_WIN_ROWS)
    back = -(-NSA_WINDOW // wr) * wr
    width = min(back + wr, seq)
    for rh in range(tq // wr):
        rows = slice(rh * wr, (rh + 1) * wr)
        qs_h = qs + rh * wr
        start = _window_start(qs_h, back, width, seq, wr)
        dist = ((qs_h - start) + lax.broadcasted_iota(jnp.int32, (wr, width), 0)
                - lax.broadcasted_iota(jnp.int32, (wr, width), 1))
        valid_w = (dist | ((NSA_WINDOW - 1) - dist)) >= 0
        distf = dist.astype(F32)
        kw = kw_ref[0, pl.ds(start, width), :]
        vw = vw_ref[0, pl.ds(start, width), :]
        for n, (g, j) in enumerate(heads):
            s = jnp.where(valid_w, _qk(qs_ref[n, rows, :], kw) - slope_of(g, j) * distf, NEG)
            ow_ref[n, rows, :], _ = _softmax_pv(s, vw)

    rsel = -(-n_sel // 8) * 8
    blk = lax.broadcasted_iota(jnp.int32, (rsel, tq), 0)
    blkf = blk.astype(F32)
    tpos = qs + lax.broadcasted_iota(jnp.int32, (rsel, tq), 1)
    cur = tpos >> 6
    forced = (blk == 0) | (blk == cur) | (blk == cur - 1)
    causal_b = blk * NSA_SEL_BLOCK <= tpos
    sel_b = []
    for g in range(NSA_GROUPS):
        ph = psum[g].astype(BF16)
        plo = (psum[g] - ph.astype(F32)).astype(BF16)
        imp = (_qk(m2t_ref[...], ph) + _qk(m2t_ref[...], plo))[0:rsel]
        score = jnp.where(causal_b, jnp.where(forced, NSA_FORCE_SCORE, imp), NEG)
        score = jnp.where(blk < n_sel, score, PICKED)
        sel = jnp.zeros((rsel, tq), F32)
        for _ in range(n_top):
            mx = jnp.max(score, axis=0, keepdims=True)
            idx = jnp.min(jnp.where(score == mx, blkf, float(LANES)), axis=0, keepdims=True)
            hit = blkf == idx
            sel = jnp.where(hit, 1.0, sel)
            score = jnp.where(hit, PICKED, score)
        sel_b.append(sel.astype(BF16))

    ones = jnp.ones((tq, LANES), BF16)
    tposf = (qs + lax.broadcasted_iota(jnp.int32, (tq, LANES), 0)).astype(F32)
    for g in range(NSA_GROUPS):
        per_block = jnp.dot(sel_b[g], ones, preferred_element_type=F32)
        flag_ref[g] = jnp.dot(t2_ref[:, 0:rsel], per_block.astype(BF16), preferred_element_type=F32)
        sel_q = lax.dot_general(sel_b[g], sh_ref[g], (((0,), (0,)), ((), ())),
                                preferred_element_type=F32)
        mask_part = ((sel_q - 1.0) * MASK_BIG).astype(BF16)
        for j in range(NSA_HPG):
            n = g * NSA_HPG + j
            first = jnp.where(_half_mask((tq, LANES), g), q_ref[0, :, j * LANES:(j + 1) * LANES], mask_part)
            qa_ref[n] = jnp.concatenate([first, _alibi_query_lanes(tposf, slope_of(g, j), coef_ref[n])], axis=1)
    _flash_reset(m_ref, l_ref, acc_ref)

    def slc_tile(ki, last):
        ks = pl.multiple_of(ki * tk, tk)
        for g, ka_ref in enumerate((ka0_ref, ka1_ref)):
            def group_update(g=g, ka_ref=ka_ref):
                ka = ka_ref[0, pl.ds(ks, tk), :]
                v = vs_ref[0, pl.ds(ks, tk), :]
                for j in range(NSA_HPG):
                    n = g * NSA_HPG + j
                    s = _qk(qa_ref[n], ka)
                    if last:
                        s = jnp.where(lax.broadcasted_iota(jnp.int32, (tq, tk), 0)
                                      >= lax.broadcasted_iota(jnp.int32, (tq, tk), 1), s, NEG)
                    _flash_update(n, s, None, v, m_ref, l_ref, acc_ref)
            if last:
                group_update()
            else:
                pl.when(flag_ref[g, pl.ds(ki, 1), :][0, 0] > 0.5)(group_update)

    def slc_body(ki, c):
        slc_tile(ki, False)
        return c

    lax.fori_loop(0, qi, slc_body, 0)
    slc_tile(qi, True)

    gl = jax.nn.sigmoid(pf_ref[0])
    g_hi = gl.astype(BF16)
    g_lo = (gl - g_hi.astype(F32)).astype(BF16)

    def gate(j, r):
        x = gx_ref[j * 3 + r]
        return jnp.dot(g_hi, x, preferred_element_type=F32) + jnp.dot(g_lo, x, preferred_element_type=F32)

    for j in range(NSA_HPG):
        n0, n1 = j, NSA_HPG + j
        comb = (gate(j, 0) * _merge_halves(o_c[(0, j)], o_c[(1, j)])
                + gate(j, 1) * _merge_halves(acc_ref[n0] / l_ref[n0], acc_ref[n1] / l_ref[n1])
                + gate(j, 2) * _merge_halves(ow_ref[n0], ow_ref[n1]))
        o_ref[0, :, j * LANES:(j + 1) * LANES] = comb.astype(BF16)


MASK_BIG = 2.0 ** 100


def _nsa_key_features(k_slc, seq):
    nb = k_slc.shape[0]
    pos = np.arange(seq)
    onehot = (pos[:, None] // NSA_SEL_BLOCK == np.arange(HALF)[None, :]).astype(np.float32)
    feats = _alibi_key_lanes(seq)
    oh = jnp.broadcast_to(jnp.asarray(onehot, BF16), (nb, seq, HALF))
    ft = jnp.broadcast_to(jnp.asarray(feats, BF16), (nb, seq, LANES))
    return (jnp.concatenate([k_slc[:, :, :HALF], oh, ft], axis=-1),
            jnp.concatenate([oh, k_slc[:, :, HALF:], ft], axis=-1))


def _nsa_attention(pb, pf, kc, vc, tq=512, tk=512):
    nb, seq, _ = pb.shape
    tq = min(tq, seq)
    tk = tq
    nc = kc.shape[1]
    n_sel = seq // NSA_SEL_BLOCK
    assert n_sel <= HALF and seq <= 64 * 64
    n_top = min(NSA_TOP_N, n_sel)
    n_cmp = seq // NSA_CMP_STRIDE - NSA_CMP_LEN // NSA_CMP_STRIDE + 1
    m2t = jnp.asarray(_cmp_to_sel_matrix(nc, n_cmp, n_sel).T, BF16)
    n_tiles = seq // tk
    tile_rows = -(-n_tiles // 8) * 8
    blocks_per_tile = tk // NSA_SEL_BLOCK
    t2 = jnp.asarray(np.arange(tile_rows)[:, None] == (np.arange(LANES)[None, :] // blocks_per_tile), BF16)
    gx = np.zeros((NSA_HPG * 3, LANES, LANES), np.float32)
    for j in range(NSA_HPG):
        for r in range(3):
            for g in range(NSA_GROUPS):
                gx[j * 3 + r, GATE_COL + (g * NSA_HPG + j) * 3 + r, g * HALF:(g + 1) * HALF] = 1.0
    gx = jnp.asarray(gx, BF16)
    rsel = -(-n_sel // 8) * 8
    shift = np.zeros((NSA_GROUPS, rsel, LANES), np.float32)
    for g in range(NSA_GROUPS):
        shift[g, np.arange(n_sel), (1 - g) * HALF + np.arange(n_sel)] = 1.0
    coef = _alibi_coef_lanes([_alibi_slope(n, NSA_HEADS) for n in range(NSA_HEADS)])
    ka0, ka1 = _nsa_key_features(pb[:, :, P_NSAKV + 2 * LANES:P_NSAKV + 3 * LANES], seq)
    w = NSA_HPG * LANES
    kvb = P_NSAKV // LANES
    kv_spec = lambda i: pl.BlockSpec((1, seq, LANES), lambda b, q, i=i: (b, 0, kvb + i))
    ka_spec = pl.BlockSpec((1, seq, 2 * LANES), lambda b, q: (b, 0, 0))
    return pl.pallas_call(
        functools.partial(_nsa_kernel, tq=tq, tk=tk, seq=seq, n_sel=n_sel, n_top=n_top),
        grid=(nb, seq // tq),
        in_specs=[
            pl.BlockSpec((1, tq, w), lambda b, i: (b, i, P_NSAQ // w)),
            pl.BlockSpec((1, nc, LANES), lambda b, i: (b, 0, 0)),
            pl.BlockSpec((1, nc, LANES), lambda b, i: (b, 0, 0)),
            ka_spec, ka_spec, kv_spec(3), kv_spec(4), kv_spec(5),
            pl.BlockSpec((1, tq, LANES), lambda b, i: (b, i, 0)),
            pl.BlockSpec((NSA_GROUPS, rsel, LANES), lambda b, i: (0, 0, 0)),
            pl.BlockSpec((NSA_HEADS, 1, LANES), lambda b, i: (0, 0, 0)),
            pl.BlockSpec((LANES, nc), lambda b, i: (0, 0)),
            pl.BlockSpec((tile_rows, LANES), lambda b, i: (0, 0)),
            pl.BlockSpec((NSA_HPG * 3, LANES, LANES), lambda b, i: (0, 0, 0)),
        ],
        out_specs=pl.BlockSpec((1, tq, w), lambda b, i: (b, i, 0)),
        out_shape=jax.ShapeDtypeStruct((nb, seq, w), BF16),
        scratch_shapes=([pltpu.VMEM((NSA_HEADS, tq, LANES), BF16), pltpu.VMEM((NSA_HEADS, tq, 2 * LANES), BF16),
                         pltpu.VMEM((NSA_GROUPS, tile_rows, LANES), F32),
                         pltpu.VMEM((NSA_HEADS, tq, LANES), F32)] + _flash_scratch(NSA_HEADS, tq)),
        compiler_params=_cparams(("parallel", "arbitrary")),
        name="nsa_attention",
    )(pb, kc, vc, ka0, ka1, pb, pb, pb, pf, jnp.asarray(shift, BF16), jnp.asarray(coef), m2t, t2, gx)


def _mixers(pb, pf, f_bias, cmp_w1, cmp_w2, cmp_pos, lam_vecs, subln_g, lam_init):
    nb, seq, _ = pb.shape
    cum = _forget_cumsum(pf, f_bias)
    cumrow = jnp.transpose(cum[:, :, :8], (0, 2, 1))
    o_fox = _fox_attention(pb, cum, cumrow)
    nc = seq // NSA_CMP_STRIDE
    rk = pb[:, :, P_NSAKV:P_NSAKV + LANES].reshape(nb, nc, NSA_CMP_STRIDE * LANES)
    rv = pb[:, :, P_NSAKV + LANES:P_NSAKV + 2 * LANES].reshape(nb, nc, NSA_CMP_STRIDE * LANES)
    kc, vc = _nsa_compress(rk, rv, *_compress_weights(cmp_w1, cmp_w2, cmp_pos))
    o_nsa = _nsa_attention(pb, pf, kc, vc)
    o_dil = _dil_attention(pb)
    o_dif = _diff_attention(pb, lam_vecs, subln_g, lam_init)
    return [o_dil, o_fox, o_nsa, o_dif]


def kernel(x, c, ada_w, ada_b, norm_mix_g, norm_ffn_g, w_in, fox_f_bias, nsa_cmp_w1, nsa_cmp_w2, nsa_cmp_pos,
           diff_lambda, diff_subln_g, w_out, ffn_w_gate, ffn_w_up, ffn_w_down, final_norm_g):
    nb, seq, d = x.shape
    depth = ada_w.shape[0]
    m = nb * seq
    mods = _modulation(c, ada_w, ada_b)
    w_p = _relayout_w_in(w_in, _proj_colmap())
    w_pf = w_p[:, :, P_FGATE:P_FGATE + LANES] + w_p[:, :, P_NGATE:P_NGATE + LANES]
    w_o = _gather_axis(w_out, _out_rowmap(), 1).astype(BF16)
    w_g, w_u, w_d = ffn_w_gate.astype(BF16), ffn_w_up.astype(BF16), ffn_w_down.astype(BF16)
    h = x.reshape(m, d)
    for layer in range(depth):
        sh1, sc1, g1, sh2, sc2, g2 = [mods[layer, :, i * d:(i + 1) * d].reshape(nb, 1, d) for i in range(6)]
        lam_init = 0.8 - 0.6 * math.exp(-0.3 * layer)
        pb, pf = _norm_proj(h, norm_mix_g[layer], sc1, sh1, w_p, layer, w_pf[layer], seq)
        parts = _mixers(pb.reshape(nb, seq, N_PB), pf.reshape(nb, seq, LANES), fox_f_bias[layer],
                        nsa_cmp_w1[layer], nsa_cmp_w2[layer], nsa_cmp_pos[layer],
                        diff_lambda[layer], diff_subln_g[layer], lam_init)
        h = _matmul_gated_residual([p.reshape(m, p.shape[-1]) for p in parts], w_o, layer, h, g1, seq,
                                   name="out_proj_res")
        act = _norm_ffn_up(h, norm_ffn_g[layer], sc2, sh2, w_g, w_u, layer, seq)
        h = _matmul_gated_residual([act], w_d, layer, h, g2, seq, name="ffn_down_res")
    return _final_norm(h, final_norm_g).reshape(nb, seq, d)
```

```python
import functools
import math

import numpy as np
import jax
import jax.numpy as jnp
from jax import lax
from jax.experimental import pallas as pl
from jax.experimental.pallas import tpu as pltpu

F32 = jnp.float32
BF16 = jnp.bfloat16

LANES = 128
HEAD_DIM = 64
HALF = HEAD_DIM
RMS_EPS = 1e-6
NEG = -1e30
PICKED = -3e38
LOG2E = math.log2(math.e)
Q_SCALE = HEAD_DIM ** -0.5 * LOG2E

FOX_HEADS = 7
NSA_HEADS = 8
NSA_GROUPS = 2
NSA_HPG = NSA_HEADS // NSA_GROUPS
NSA_CMP_LEN = 32
NSA_CMP_STRIDE = 16
NSA_SEL_BLOCK = 64
NSA_TOP_N = 16
NSA_WINDOW = 512
NSA_WIN_ROWS = 256
NSA_FORCE_SCORE = 1e4
DIL_CONFIGS = ((128, 1), (512, 4), (2048, 16))
DIL_HPG = 3
DIL_HEADS = DIL_HPG * len(DIL_CONFIGS)
DIFF_HEADS = 4
ALIBI_MAX_EXP = 8.0

_SPLITS = (
    ("fox_q", 448), ("fox_k", 448), ("fox_v", 448), ("fox_f", 7),
    ("nsa_q", 512), ("nsa_cmp_k", 128), ("nsa_cmp_v", 128), ("nsa_slc_k", 128),
    ("nsa_slc_v", 128), ("nsa_win_k", 128), ("nsa_win_v", 128), ("nsa_gate", 24),
    ("dil_q", 576), ("dil_k", 576), ("dil_v", 576),
    ("diff_q", 512), ("diff_k", 512), ("diff_v", 512),
)
_SRC = {}
_o = 0
for _n, _w in _SPLITS:
    _SRC[_n] = _o
    _o += _w
N_IN = _o

DIL_GW = 256
DIL_W = 3 * DIL_GW
P_FOXQ, P_FOXK, P_FOXV = 0, 512, 1024
P_NSAQ = 1536
P_NSAKV = 2048
P_DILQ, P_DILK, P_DILV = 2816, 3584, 4352
P_DIFQ, P_DIFK, P_DIFV = 5120, 5632, 6144
N_PB = 6656
GATE_COL = 8
RELAYOUT_TILE = 512
P_FGATE, P_NGATE = N_PB, N_PB + RELAYOUT_TILE
N_WP = N_PB + 2 * RELAYOUT_TILE


def _proj_colmap():
    pb = -np.ones(N_WP, np.int64)
    ar = np.arange
    for name, dst in (("fox_q", P_FOXQ), ("fox_k", P_FOXK), ("fox_v", P_FOXV)):
        pb[dst:dst + 448] = _SRC[name] + ar(448)
    for j in range(NSA_HPG):
        for g in range(NSA_GROUPS):
            d = P_NSAQ + j * LANES + g * HALF
            pb[d:d + HALF] = _SRC["nsa_q"] + (g * NSA_HPG + j) * HEAD_DIM + ar(HALF)
    for i, name in enumerate(("nsa_cmp_k", "nsa_cmp_v", "nsa_slc_k", "nsa_slc_v", "nsa_win_k", "nsa_win_v")):
        pb[P_NSAKV + i * LANES:P_NSAKV + (i + 1) * LANES] = _SRC[name] + ar(LANES)
    for name, dst in (("dil_q", P_DILQ), ("dil_k", P_DILK), ("dil_v", P_DILV)):
        for g in range(len(DIL_CONFIGS)):
            for jj in range(DIL_HPG):
                d = dst + (2 * g + jj // 2) * LANES + (jj % 2) * HALF
                pb[d:d + HALF] = _SRC[name] + (g * DIL_HPG + jj) * HEAD_DIM + ar(HALF)
    for name, dst in (("diff_q", P_DIFQ), ("diff_k", P_DIFK), ("diff_v", P_DIFV)):
        pb[dst:dst + 512] = _SRC[name] + ar(512)
    pb[P_FGATE:P_FGATE + FOX_HEADS] = _SRC["fox_f"] + ar(FOX_HEADS)
    pb[P_NGATE + GATE_COL:P_NGATE + GATE_COL + 24] = _SRC["nsa_gate"] + ar(24)
    return pb


O_DIL, O_FOX, O_NSA, O_DIF = 0, 768, 1280, 1792
N_O = 2304


def _out_rowmap():
    m = -np.ones(N_O, np.int64)
    ar = np.arange
    m[O_FOX:O_FOX + 448] = ar(448)
    for j in range(NSA_HPG):
        for g in range(NSA_GROUPS):
            d = O_NSA + j * LANES + g * HALF
            m[d:d + HALF] = 448 + (g * NSA_HPG + j) * HEAD_DIM + ar(HALF)
    for g in range(len(DIL_CONFIGS)):
        for jj in range(DIL_HPG):
            d = O_DIL + (2 * g + jj // 2) * LANES + (jj % 2) * HALF
            m[d:d + HALF] = 960 + (g * DIL_HPG + jj) * HEAD_DIM + ar(HALF)
    m[O_DIF:O_DIF + 512] = 1536 + ar(512)
    return m


def _gather_axis(x, idx, axis):
    pieces = []
    i = 0
    n = len(idx)
    while i < n:
        j = i + 1
        if idx[i] < 0:
            while j < n and idx[j] < 0:
                j += 1
            shape = list(x.shape)
            shape[axis] = j - i
            pieces.append(jnp.zeros(shape, x.dtype))
        else:
            while j < n and idx[j] == idx[j - 1] + 1:
                j += 1
            pieces.append(lax.slice_in_dim(x, int(idx[i]), int(idx[j - 1]) + 1, axis=axis))
        i = j
    return jnp.concatenate(pieces, axis=axis)


def _cparams(sem, vmem_mb=48):
    return pltpu.CompilerParams(dimension_semantics=sem, vmem_limit_bytes=vmem_mb * 1024 * 1024)


def _alibi_slope(k, n_heads):
    return float(2.0 ** (-ALIBI_MAX_EXP * (k + 1) / n_heads)) * LOG2E


def _src_col_scale(n_pad):
    sc = np.ones(n_pad, np.float32)
    for name, width in (("fox_q", 448), ("nsa_q", 512), ("dil_q", 576), ("diff_q", 512)):
        sc[_SRC[name]:_SRC[name] + width] = Q_SCALE
    return sc


def _relayout_plan(colmap, n_src, tile):
    n_tiles = len(colmap) // tile
    last = (n_src - 1) // tile
    win = np.zeros(n_tiles, np.int32)
    local = -np.ones((n_tiles, 1, tile), np.int32)
    for t in range(n_tiles):
        src = colmap[t * tile:(t + 1) * tile]
        used = src[src >= 0]
        a = min(int(used.min()) // tile, max(last - 1, 0))
        assert int(used.max()) < (a + 2) * tile, "tile sources exceed the two-block window"
        win[t] = a
        local[t, 0] = np.where(src >= 0, src - a * tile, -1)
    return win, local


def _relayout_kernel(win_ref, idx_ref, sa_ref, sb_ref, wa_ref, wb_ref, o_ref, *, n_src, tile):
    a = win_ref[pl.program_id(1)]
    idx = idx_ref[0]
    row = lax.broadcasted_iota(jnp.int32, (tile, tile), 0)
    col = lax.broadcasted_iota(jnp.int32, wa_ref.shape[1:], 1)
    acc = None
    for part, (w_ref, s_ref) in enumerate(((wa_ref, sa_ref), (wb_ref, sb_ref))):
        sel = jnp.where(row + part * tile == idx, 1.0, 0.0).astype(BF16)
        w = (jnp.where(col + (a + part) * tile < n_src, w_ref[0], 0.0) * s_ref[0]).astype(BF16)
        d = jnp.dot(w, sel, preferred_element_type=F32)
        acc = d if acc is None else acc + d
    o_ref[0] = acc.astype(BF16)


def _relayout_w_in(w_in, colmap, tile=RELAYOUT_TILE):
    depth, d, n_src = w_in.shape
    win, local = _relayout_plan(colmap, n_src, tile)
    n_tiles = len(win)
    n_blk = -(-n_src // tile)
    scale = jnp.asarray(_src_col_scale(n_blk * tile).reshape(n_blk, 1, tile))
    grid_spec = pltpu.PrefetchScalarGridSpec(
        num_scalar_prefetch=1,
        grid=(depth, n_tiles),
        in_specs=[
            pl.BlockSpec((1, 1, tile), lambda l, t, win: (t, 0, 0)),
            pl.BlockSpec((1, 1, tile), lambda l, t, win: (win[t], 0, 0)),
            pl.BlockSpec((1, 1, tile), lambda l, t, win: (win[t] + 1, 0, 0)),
            pl.BlockSpec((1, d, tile), lambda l, t, win: (l, 0, win[t])),
            pl.BlockSpec((1, d, tile), lambda l, t, win: (l, 0, win[t] + 1)),
        ],
        out_specs=pl.BlockSpec((1, d, tile), lambda l, t, win: (l, 0, t)),
    )
    return pl.pallas_call(
        functools.partial(_relayout_kernel, n_src=n_src, tile=tile),
        grid_spec=grid_spec,
        out_shape=jax.ShapeDtypeStruct((depth, d, n_tiles * tile), BF16),
        compiler_params=_cparams(("parallel", "arbitrary")),
        name="w_in_relayout",
    )(jnp.asarray(win), jnp.asarray(local), scale, scale, w_in, w_in)


def _mod_kernel(c_ref, w_ref, b_ref, o_ref):
    nb = c_ref.shape[0]
    tn = w_ref.shape[2]
    for b in range(nb):
        cb = c_ref[b]
        ca = cb * jax.nn.sigmoid(cb)
        for j in range(tn // LANES):
            sl = slice(j * LANES, (j + 1) * LANES)
            w = w_ref[0, :, sl]
            o_ref[0, b:b + 1, sl] = jnp.sum(w * ca, axis=0, keepdims=True) + b_ref[0, :, sl]


def _modulation(c, ada_w, ada_b, tn=512):
    depth, d, n = ada_w.shape
    nb = c.shape[0]
    c_rep = jnp.broadcast_to(c[:, :, None], (nb, d, LANES))
    return pl.pallas_call(
        _mod_kernel,
        grid=(depth, n // tn),
        in_specs=[
            pl.BlockSpec((nb, d, LANES), lambda l, j: (0, 0, 0)),
            pl.BlockSpec((1, d, tn), lambda l, j: (l, 0, j)),
            pl.BlockSpec((1, 1, tn), lambda l, j: (l, 0, j)),
        ],
        out_specs=pl.BlockSpec((1, nb, tn), lambda l, j: (l, 0, j)),
        out_shape=jax.ShapeDtypeStruct((depth, nb, n), F32),
        compiler_params=_cparams(("parallel", "parallel")),
        name="adaln_mod",
    )(c_rep, ada_w, ada_b.reshape(depth, 1, n))


NORM_CHUNKS = 4


def _norm_mod(h_ref, g_ref, sc_ref, sh_ref, rows=slice(None)):
    x = h_ref[rows, :]
    ms = jnp.mean(x * x, axis=-1, keepdims=True)
    y = x * lax.rsqrt(ms + RMS_EPS) * g_ref[...]
    return (y * (1.0 + sc_ref[0]) + sh_ref[0]).astype(BF16)


def _row_chunks(n):
    step = n // NORM_CHUNKS
    return [slice(c * step, (c + 1) * step) for c in range(NORM_CHUNKS)]


def _proj_kernel(h_ref, g_ref, sc_ref, sh_ref, w_ref, wf_ref, pb_ref, pf_ref, u_ref):
    first = pl.program_id(1) == 0

    @pl.when(first)
    def _():
        for rows in _row_chunks(h_ref.shape[0]):
            u = _norm_mod(h_ref, g_ref, sc_ref, sh_ref, rows)
            u_ref[rows, :] = u
            pf_ref[rows, :] = jnp.dot(u, wf_ref[...], preferred_element_type=F32)
            pb_ref[rows, :] = jnp.dot(u, w_ref[0], preferred_element_type=F32).astype(BF16)

    @pl.when(jnp.logical_not(first))
    def _():
        pb_ref[...] = jnp.dot(u_ref[...], w_ref[0], preferred_element_type=F32).astype(BF16)


def _norm_proj(h, gain, sc, sh, w_all, layer, wf, seq, tm=1024, tn=512):
    m, d = h.shape
    per = seq // tm
    return pl.pallas_call(
        _proj_kernel,
        grid=(m // tm, N_PB // tn),
        in_specs=[
            pl.BlockSpec((tm, d), lambda i, j: (i, 0)),
            pl.BlockSpec((1, d), lambda i, j: (0, 0)),
            pl.BlockSpec((1, 1, d), lambda i, j: (i // per, 0, 0)),
            pl.BlockSpec((1, 1, d), lambda i, j: (i // per, 0, 0)),
            pl.BlockSpec((1, d, tn), lambda i, j: (layer, 0, j)),
            pl.BlockSpec((d, LANES), lambda i, j: (0, 0)),
        ],
        out_specs=[
            pl.BlockSpec((tm, tn), lambda i, j: (i, j)),
            pl.BlockSpec((tm, LANES), lambda i, j: (i, 0)),
        ],
        out_shape=[jax.ShapeDtypeStruct((m, N_PB), BF16), jax.ShapeDtypeStruct((m, LANES), F32)],
        scratch_shapes=[pltpu.VMEM((tm, d), BF16)],
        compiler_params=_cparams(("parallel", "arbitrary")),
        name="norm_in_proj",
    )(h, gain.reshape(1, d), sc, sh, w_all, wf)


def _ffn_up_kernel(h_ref, g_ref, sc_ref, sh_ref, wg_ref, wu_ref, a_ref, u_ref):
    def swiglu(u):
        gate = jnp.dot(u, wg_ref[0], preferred_element_type=F32)
        up = jnp.dot(u, wu_ref[0], preferred_element_type=F32)
        return (gate * jax.nn.sigmoid(gate) * up).astype(BF16)

    first = pl.program_id(1) == 0

    @pl.when(first)
    def _():
        for rows in _row_chunks(h_ref.shape[0]):
            u = _norm_mod(h_ref, g_ref, sc_ref, sh_ref, rows)
            u_ref[rows, :] = u
            a_ref[rows, :] = swiglu(u)

    @pl.when(jnp.logical_not(first))
    def _():
        a_ref[...] = swiglu(u_ref[...])


def _norm_ffn_up(h, gain, sc, sh, wg, wu, layer, seq, tm=1024, tn=512):
    m, d = h.shape
    n = wg.shape[2]
    per = seq // tm
    return pl.pallas_call(
        _ffn_up_kernel,
        grid=(m // tm, n // tn),
        in_specs=[
            pl.BlockSpec((tm, d), lambda i, j: (i, 0)),
            pl.BlockSpec((1, d), lambda i, j: (0, 0)),
            pl.BlockSpec((1, 1, d), lambda i, j: (i // per, 0, 0)),
            pl.BlockSpec((1, 1, d), lambda i, j: (i // per, 0, 0)),
            pl.BlockSpec((1, d, tn), lambda i, j: (layer, 0, j)),
            pl.BlockSpec((1, d, tn), lambda i, j: (layer, 0, j)),
        ],
        out_specs=pl.BlockSpec((tm, tn), lambda i, j: (i, j)),
        out_shape=jax.ShapeDtypeStruct((m, n), BF16),
        scratch_shapes=[pltpu.VMEM((tm, d), BF16)],
        compiler_params=_cparams(("parallel", "arbitrary")),
        name="norm_ffn_up",
    )(h, gain.reshape(1, d), sc, sh, wg, wu)


def _mm_res_kernel(*refs, widths):
    n = len(widths)
    a_refs, (w_ref, h_ref, g_ref, o_ref) = refs[:n], refs[n:]
    acc = None
    off = 0
    for a_ref, width in zip(a_refs, widths):
        d = jnp.dot(a_ref[...], w_ref[0, off:off + width, :], preferred_element_type=F32)
        acc = d if acc is None else acc + d
        off += width
    o_ref[...] = h_ref[...] + g_ref[0] * acc


def _matmul_gated_residual(a_parts, w, layer, h, gate, seq, tm=1024, tn=512, name="mm_res"):
    m = h.shape[0]
    widths = tuple(a.shape[1] for a in a_parts)
    k, n = w.shape[1], w.shape[2]
    assert sum(widths) == k
    per = seq // tm
    block_bytes = tm * k * 2 + k * tn * 2 + 2 * tm * tn * 4
    vmem_mb = (2 * block_bytes + 4 * tm * tn * 4) // (1024 * 1024) + 1
    return pl.pallas_call(
        functools.partial(_mm_res_kernel, widths=widths),
        grid=(m // tm, n // tn),
        in_specs=[pl.BlockSpec((tm, wd), lambda i, j: (i, 0)) for wd in widths] + [
            pl.BlockSpec((1, k, tn), lambda i, j: (layer, 0, j)),
            pl.BlockSpec((tm, tn), lambda i, j: (i, j)),
            pl.BlockSpec((1, 1, tn), lambda i, j: (i // per, 0, j)),
        ],
        out_specs=pl.BlockSpec((tm, tn), lambda i, j: (i, j)),
        out_shape=jax.ShapeDtypeStruct((m, n), F32),
        compiler_params=_cparams(("parallel", "arbitrary"), vmem_mb=vmem_mb),
        name=name,
    )(*a_parts, w, h, gate)


def _final_norm_kernel(h_ref, g_ref, o_ref):
    x = h_ref[...]
    ms = jnp.mean(x * x, axis=-1, keepdims=True)
    o_ref[...] = x * lax.rsqrt(ms + RMS_EPS) * g_ref[...]


def _final_norm(h, gain, tm=512):
    m, d = h.shape
    return pl.pallas_call(
        _final_norm_kernel,
        grid=(m // tm,),
        in_specs=[pl.BlockSpec((tm, d), lambda i: (i, 0)), pl.BlockSpec((1, d), lambda i: (0, 0))],
        out_specs=pl.BlockSpec((tm, d), lambda i: (i, 0)),
        out_shape=jax.ShapeDtypeStruct((m, d), F32),
        compiler_params=_cparams(("parallel",)),
        name="final_norm",
    )(h, gain.reshape(1, d))


def _half_mask(shape, half):
    lane = lax.broadcasted_iota(jnp.int32, shape, 1)
    return (lane < HALF) if half == 0 else (lane >= HALF)


def _pick_half(qp, half):
    return jnp.where(_half_mask(qp.shape, half), qp, jnp.zeros_like(qp))


def _lane_tile(x, n):
    return x if n == 1 else jnp.concatenate([x] * n, axis=1)


def _qk(qh, k):
    return lax.dot_general(qh, k, (((1,), (1,)), ((), ())), preferred_element_type=F32)


def _merge_halves(lo, hi):
    if hi is None:
        hi = jnp.zeros_like(lo)
    return jnp.where(_half_mask(lo.shape, 0), lo, hi)


def _flash_scratch(n_heads, tq):
    return [pltpu.VMEM((n_heads, tq, LANES), F32), pltpu.VMEM((n_heads, tq, LANES), F32),
            pltpu.VMEM((n_heads, tq, LANES), F32)]


def _flash_reset(m_ref, l_ref, acc_ref):
    m_ref[...] = jnp.full(m_ref.shape, NEG, F32)
    l_ref[...] = jnp.zeros(l_ref.shape, F32)
    acc_ref[...] = jnp.zeros(acc_ref.shape, F32)


def _flash_update(h, s, valid, v, m_ref, l_ref, acc_ref):
    m_old = m_ref[h]
    m_new = jnp.maximum(m_old, jnp.max(s, axis=1, keepdims=True))
    alpha = jnp.exp2(m_old - m_new)
    p = jnp.exp2(s - _lane_tile(m_new, s.shape[1] // LANES))
    if valid is not None:
        p = jnp.where(valid, p, 0.0)
    l_ref[h] = alpha * l_ref[h] + jnp.sum(p, axis=1, keepdims=True)
    acc_ref[h] = alpha * acc_ref[h] + jnp.dot(p.astype(BF16), v, preferred_element_type=F32)
    m_ref[h] = m_new


def _softmax_pv(s, v):
    m = jnp.max(s, axis=1, keepdims=True)
    p = jnp.exp2(s - m)
    l = jnp.sum(p, axis=1, keepdims=True)
    return jnp.dot(p.astype(BF16), v, preferred_element_type=F32) / l, m + jnp.log2(l)


def _causal_steps(qi, t, step):
    def wide(k2, c):
        step(pl.multiple_of(k2 * (2 * t), 2 * t), 2 * t, False)
        return c

    lax.fori_loop(0, lax.shift_right_logical(qi, 1), wide, 0)

    @pl.when((qi & 1) == 1)
    def _():
        step(pl.multiple_of((qi - 1) * t, t), t, False)

    step(pl.multiple_of(qi * t, t), t, True)


def _bf16_parts(x):
    parts = []
    r = np.float32(x)
    for _ in range(3):
        p = np.float32(np.asarray(r, dtype=BF16))
        parts.append(float(p))
        r = np.float32(r - p)
    return parts


def _alibi_key_lanes(seq):
    assert seq <= 64 * 64
    pos = np.arange(seq)
    feats = np.zeros((seq, LANES), np.float32)
    feats[:, 0:3] = 1.0
    feats[:, 3:6] = (pos // 64)[:, None]
    feats[:, 6:9] = (pos % 64)[:, None]
    return feats


def _alibi_coef_lanes(slopes):
    coef = np.zeros((len(slopes), 1, LANES), np.float32)
    for n, sl in enumerate(slopes):
        coef[n, 0, 3:6] = _bf16_parts(sl * 64.0)
        coef[n, 0, 6:9] = _bf16_parts(sl)
    return coef


def _alibi_query_lanes(tposf, slope, coef_row):
    lane = lax.broadcasted_iota(jnp.int32, tposf.shape, 1)
    a = tposf * (-slope)
    a_hi = a.astype(BF16)
    r1 = a - a_hi.astype(F32)
    a_mid = r1.astype(BF16)
    a_lo = (r1 - a_mid.astype(F32)).astype(BF16)
    return jnp.where(lane == 0, a_hi, jnp.where(lane == 1, a_mid, jnp.where(lane == 2, a_lo, coef_row.astype(BF16))))


def _window_start(qs, back, width, seq, align):
    return pl.multiple_of(jnp.clip(qs - back, 0, seq - width), align)


def _cumsum_kernel(pf_ref, fb_ref, o_ref, *, chunk):
    seq = pf_ref.shape[1]
    r = lax.broadcasted_iota(jnp.int32, (chunk, chunk), 0)
    c = lax.broadcasted_iota(jnp.int32, (chunk, chunk), 1)
    tri = jnp.where(r >= c, 1.0, 0.0).astype(BF16)

    def body(i, carry):
        st = pl.multiple_of(i * chunk, chunk)
        x = pf_ref[0, pl.ds(st, chunk), :] + fb_ref[...]
        ls = jnp.minimum(x, 0.0) - jnp.log1p(jnp.exp(-jnp.abs(x)))
        hi = ls.astype(BF16)
        r1 = ls - hi.astype(F32)
        mid = r1.astype(BF16)
        lo = (r1 - mid.astype(F32)).astype(BF16)
        cs = (jnp.dot(tri, hi, preferred_element_type=F32)
              + jnp.dot(tri, mid, preferred_element_type=F32)
              + jnp.dot(tri, lo, preferred_element_type=F32))
        out = cs + carry
        o_ref[0, pl.ds(st, chunk), :] = out * LOG2E
        return out[chunk - 1:chunk, :]

    lax.fori_loop(0, seq // chunk, body, jnp.zeros((1, LANES), F32))


def _forget_cumsum(pf, f_bias, chunk=256):
    nb, seq, _ = pf.shape
    fb = jnp.zeros((1, LANES), F32).at[0, :FOX_HEADS].set(f_bias.astype(F32))
    return pl.pallas_call(
        functools.partial(_cumsum_kernel, chunk=chunk),
        grid=(nb,),
        in_specs=[pl.BlockSpec((1, seq, LANES), lambda b: (b, 0, 0)), pl.BlockSpec((1, LANES), lambda b: (0, 0))],
        out_specs=pl.BlockSpec((1, seq, LANES), lambda b: (b, 0, 0)),
        out_shape=jax.ShapeDtypeStruct((nb, seq, LANES), F32),
        compiler_params=_cparams(("parallel",)),
        name="forget_cumsum",
    )(pf, fb)


def _fox_kernel(q_ref, k_ref, v_ref, cc_ref, cr_ref, o_ref, qs_ref, cq_ref, m_ref, l_ref, acc_ref, *, t):
    qi = pl.program_id(1)
    _flash_reset(m_ref, l_ref, acc_ref)
    for h in range(FOX_HEADS):
        qs_ref[h] = _pick_half(q_ref[0, :, (h // 2) * LANES:(h // 2 + 1) * LANES], h % 2)
        cq_ref[h] = jnp.broadcast_to(cc_ref[0, :, h:h + 1], (t, LANES))

    def step(ks, width, masked):
        if masked:
            causal = (lax.broadcasted_iota(jnp.int32, (t, t), 0) >= lax.broadcasted_iota(jnp.int32, (t, t), 1))
        for h in range(FOX_HEADS):
            sl = slice((h // 2) * LANES, (h // 2 + 1) * LANES)
            k = k_ref[0, pl.ds(ks, width), sl]
            v = v_ref[0, pl.ds(ks, width), sl]
            s = ((_qk(qs_ref[h], k) + _lane_tile(cq_ref[h], width // LANES))
                 - cr_ref[0, h:h + 1, pl.ds(ks, width)])
            if masked:
                s = jnp.where(causal, s, NEG)
            _flash_update(h, s, None, v, m_ref, l_ref, acc_ref)

    _causal_steps(qi, t, step)
    for p in range(4):
        lo = acc_ref[2 * p] / l_ref[2 * p]
        hi = acc_ref[2 * p + 1] / l_ref[2 * p + 1] if 2 * p + 1 < FOX_HEADS else None
        o_ref[0, :, p * LANES:(p + 1) * LANES] = _merge_halves(lo, hi).astype(BF16)


def _fox_attention(pb, cum, cumrow, t=512):
    nb, seq, _ = pb.shape
    t = min(t, seq)
    w = 4 * LANES
    return pl.pallas_call(
        functools.partial(_fox_kernel, t=t),
        grid=(nb, seq // t),
        in_specs=[
            pl.BlockSpec((1, t, w), lambda b, i: (b, i, P_FOXQ // w)),
            pl.BlockSpec((1, seq, w), lambda b, i: (b, 0, P_FOXK // w)),
            pl.BlockSpec((1, seq, w), lambda b, i: (b, 0, P_FOXV // w)),
            pl.BlockSpec((1, t, LANES), lambda b, i: (b, i, 0)),
            pl.BlockSpec((1, 8, seq), lambda b, i: (b, 0, 0)),
        ],
        out_specs=pl.BlockSpec((1, t, w), lambda b, i: (b, i, 0)),
        out_shape=jax.ShapeDtypeStruct((nb, seq, w), BF16),
        scratch_shapes=([pltpu.VMEM((FOX_HEADS, t, LANES), BF16), pltpu.VMEM((FOX_HEADS, t, LANES), F32)]
                        + _flash_scratch(FOX_HEADS, t)),
        compiler_params=_cparams(("parallel", "arbitrary")),
        name="fox_attention",
    )(pb, pb, pb, cum, cumrow)


def _diff_kernel(q_ref, k_ref, v_ref, coef_ref, lam_ref, g_ref, o_ref, qs_ref, m_ref, l_ref, acc_ref,
                 *, t, lam_init):
    qi = pl.program_id(1)
    n_slots = 2 * DIFF_HEADS
    _flash_reset(m_ref, l_ref, acc_ref)
    tposf = (qi * t + lax.broadcasted_iota(jnp.int32, (t, LANES), 0)).astype(F32)
    for n in range(n_slots):
        hd = n // 2
        qs_ref[n] = jnp.concatenate(
            [_pick_half(q_ref[0, :, hd * LANES:(hd + 1) * LANES], n % 2),
             _alibi_query_lanes(tposf, _alibi_slope(hd, DIFF_HEADS), coef_ref[hd])], axis=1)

    def step(ks, width, masked):
        if masked:
            causal = (lax.broadcasted_iota(jnp.int32, (t, t), 0) >= lax.broadcasted_iota(jnp.int32, (t, t), 1))
        for n in range(n_slots):
            hd = n // 2
            sl = slice(hd * LANES, (hd + 1) * LANES)
            k = k_ref[0, pl.ds(ks, width), 2 * hd * LANES:2 * (hd + 1) * LANES]
            v = v_ref[0, pl.ds(ks, width), sl]
            s = _qk(qs_ref[n], k)
            if masked:
                s = jnp.where(causal, s, NEG)
            _flash_update(n, s, None, v, m_ref, l_ref, acc_ref)

    _causal_steps(qi, t, step)
    lv = lam_ref[...]
    lam = (jnp.exp(jnp.sum(lv[0:1] * lv[1:2], axis=1, keepdims=True))
           - jnp.exp(jnp.sum(lv[2:3] * lv[3:4], axis=1, keepdims=True)) + lam_init)
    for hd in range(DIFF_HEADS):
        o = acc_ref[2 * hd] / l_ref[2 * hd] - lam * (acc_ref[2 * hd + 1] / l_ref[2 * hd + 1])
        ms = jnp.mean(o * o, axis=-1, keepdims=True)
        y = o * lax.rsqrt(ms + RMS_EPS) * g_ref[...]
        o_ref[0, :, hd * LANES:(hd + 1) * LANES] = (y * (1.0 - lam_init)).astype(BF16)


def _diff_attention(pb, lam_vecs, subln_g, lam_init, t=512):
    nb, seq, _ = pb.shape
    t = min(t, seq)
    w = DIFF_HEADS * LANES
    ft = jnp.broadcast_to(jnp.asarray(_alibi_key_lanes(seq), BF16), (nb, seq, LANES))
    k_aug = jnp.concatenate(
        [x for hd in range(DIFF_HEADS)
         for x in (pb[:, :, P_DIFK + hd * LANES:P_DIFK + (hd + 1) * LANES], ft)], axis=-1)
    coef = jnp.asarray(_alibi_coef_lanes([_alibi_slope(hd, DIFF_HEADS) for hd in range(DIFF_HEADS)]))
    return pl.pallas_call(
        functools.partial(_diff_kernel, t=t, lam_init=lam_init),
        grid=(nb, seq // t),
        in_specs=[
            pl.BlockSpec((1, t, w), lambda b, i: (b, i, P_DIFQ // w)),
            pl.BlockSpec((1, seq, 2 * w), lambda b, i: (b, 0, 0)),
            pl.BlockSpec((1, seq, w), lambda b, i: (b, 0, P_DIFV // w)),
            pl.BlockSpec((DIFF_HEADS, 1, LANES), lambda b, i: (0, 0, 0)),
            pl.BlockSpec((4, HEAD_DIM), lambda b, i: (0, 0)),
            pl.BlockSpec((1, LANES), lambda b, i: (0, 0)),
        ],
        out_specs=pl.BlockSpec((1, t, w), lambda b, i: (b, i, 0)),
        out_shape=jax.ShapeDtypeStruct((nb, seq, w), BF16),
        scratch_shapes=[pltpu.VMEM((2 * DIFF_HEADS, t, 2 * LANES), BF16)] + _flash_scratch(2 * DIFF_HEADS, t),
        compiler_params=_cparams(("parallel", "arbitrary"), vmem_mb=56),
        name="diff_attention",
    )(pb, k_aug, pb, coef, lam_vecs.astype(F32), subln_g.astype(F32).reshape(1, LANES))


DIL_SUBWINDOW = 128
DIL_DENSE_MAX = 4


def _dil_group_kernel(q_ref, k_ref, v_ref, o_ref, lse_ref, *, tq, sub_len, window, step, row_stride, head0):
    qs = pl.program_id(2) * tq
    back = -(-window // tq) * tq
    width = min(back + tq, sub_len)
    start = _window_start(qs, back, width, sub_len, tq)
    dist = ((qs - start) + lax.broadcasted_iota(jnp.int32, (tq, width), 0)
            - lax.broadcasted_iota(jnp.int32, (tq, width), 1))
    valid = ((dist & (step - 1)) | ((dist | (window - dist)) >> 31)) == 0
    distf = dist.astype(F32)
    lane = lax.broadcasted_iota(jnp.int32, (tq, LANES), 1)
    lse_blk = jnp.zeros((tq, LANES), F32)
    outs = []
    for jj in range(DIL_HPG):
        sl = slice((jj // 2) * LANES, (jj // 2 + 1) * LANES)
        slope = _alibi_slope(head0 + jj, DIL_HEADS) * row_stride
        k = k_ref[0, pl.ds(start, width), sl]
        v = v_ref[0, pl.ds(start, width), sl]
        s = jnp.where(valid, _qk(_pick_half(q_ref[0, :, sl], jj % 2), k) - slope * distf, NEG)
        o, lse = _softmax_pv(s, v)
        outs.append(o)
        lse_blk = jnp.where(lane == jj, lse, lse_blk)
    o_ref[0, :, 0:LANES] = _merge_halves(outs[0], outs[1])
    o_ref[0, :, LANES:2 * LANES] = _merge_halves(outs[2], None)
    lse_ref[0] = lse_blk


def _dil_group(pb, g, tq=256):
    nb, seq, n_pb = pb.shape
    window, dilation = DIL_CONFIGS[g]
    assert window == DIL_SUBWINDOW * dilation
    gw = DIL_GW
    if dilation <= DIL_DENSE_MAX:
        views = (pb, pb, pb)
        cols = [base // gw + g for base in (P_DILQ, P_DILK, P_DILV)]
        per_tok = 0
        dilation, step = 1, dilation
    else:
        views = tuple(lax.slice_in_dim(pb, base + g * gw, base + (g + 1) * gw, axis=2)
                      .reshape(nb, seq // dilation, dilation * gw) for base in (P_DILQ, P_DILK, P_DILV))
        cols = [0, 0, 0]
        per_tok = 1
        window, step = DIL_SUBWINDOW, 1
    sub_len = seq // dilation
    tq = min(tq, sub_len)
    o, lse = pl.pallas_call(
        functools.partial(_dil_group_kernel, tq=tq, sub_len=sub_len, window=window, step=step,
                          row_stride=dilation, head0=g * DIL_HPG),
        grid=(nb, dilation, sub_len // tq),
        in_specs=[
            pl.BlockSpec((1, tq, gw), lambda b, r, i: (b, i, r * per_tok + cols[0])),
            pl.BlockSpec((1, sub_len, gw), lambda b, r, i: (b, 0, r * per_tok + cols[1])),
            pl.BlockSpec((1, sub_len, gw), lambda b, r, i: (b, 0, r * per_tok + cols[2])),
        ],
        out_specs=[pl.BlockSpec((1, tq, gw), lambda b, r, i: (b, i, r)),
                   pl.BlockSpec((1, tq, LANES), lambda b, r, i: (b, i, r))],
        out_shape=[jax.ShapeDtypeStruct((nb, sub_len, dilation * gw), F32),
                   jax.ShapeDtypeStruct((nb, sub_len, dilation * LANES), F32)],
        compiler_params=_cparams(("parallel", "parallel", "arbitrary")),
        name=f"dil_group{g}",
    )(*views)
    return o.reshape(nb, seq, gw), lse.reshape(nb, seq, LANES)


def _dil_mix_kernel(*refs):
    ng = len(DIL_CONFIGS)
    o_refs, l_refs, (x_ref, out_ref) = refs[:ng], refs[ng:2 * ng], refs[2 * ng:]
    lses = [r[0] for r in l_refs]
    mx = lses[0]
    for g in range(1, ng):
        mx = jnp.maximum(mx, lses[g])
    es = [jnp.exp2(l - mx) for l in lses]
    tot = es[0]
    for g in range(1, ng):
        tot = tot + es[g]
    for g in range(ng):
        w = es[g] / tot
        hi = w.astype(BF16)
        lo = (w - hi.astype(F32)).astype(BF16)
        wx = (jnp.dot(hi, x_ref[...], preferred_element_type=F32)
              + jnp.dot(lo, x_ref[...], preferred_element_type=F32))
        out_ref[0, :, g * DIL_GW:(g + 1) * DIL_GW] = (o_refs[g][0] * wx).astype(BF16)


def _dil_attention(pb, tm=512):
    nb, seq, _ = pb.shape
    ng = len(DIL_CONFIGS)
    outs, lses = zip(*[_dil_group(pb, g) for g in range(ng)])
    spread = np.zeros((LANES, DIL_GW), np.float32)
    for jj in range(DIL_HPG):
        spread[jj, jj * HALF:(jj + 1) * HALF] = 1.0
    tm = min(tm, seq)
    return pl.pallas_call(
        _dil_mix_kernel,
        grid=(nb, seq // tm),
        in_specs=([pl.BlockSpec((1, tm, DIL_GW), lambda b, i: (b, i, 0))] * ng
                  + [pl.BlockSpec((1, tm, LANES), lambda b, i: (b, i, 0))] * ng
                  + [pl.BlockSpec((LANES, DIL_GW), lambda b, i: (0, 0))]),
        out_specs=pl.BlockSpec((1, tm, DIL_W), lambda b, i: (b, i, 0)),
        out_shape=jax.ShapeDtypeStruct((nb, seq, DIL_W), BF16),
        compiler_params=_cparams(("parallel", "parallel")),
        name="dil_mix",
    )(*outs, *lses, jnp.asarray(spread, BF16))


def _gelu_tanh(x):
    return 0.5 * x * (1.0 + jnp.tanh(math.sqrt(2.0 / math.pi) * (x + 0.044715 * (x * x * x))))


def _compress_kernel(rk_ref, rv_ref, w1_ref, w2_ref, pos_ref, kc_ref, vc_ref):
    nc = rk_ref.shape[1]
    rowi = lax.broadcasted_iota(jnp.int32, (nc, LANES), 0)
    for kv, (r_ref, o_ref) in enumerate(((rk_ref, kc_ref), (rv_ref, vc_ref))):
        r = r_ref[0]
        a = jnp.dot(r, w1_ref[kv, 0], preferred_element_type=F32)
        b = jnp.dot(r, w1_ref[kv, 1], preferred_element_type=F32)
        pt = (jnp.dot(pos_ref[kv, 0], w1_ref[kv, 0], preferred_element_type=F32)
              + jnp.dot(pos_ref[kv, 1], w1_ref[kv, 1], preferred_element_type=F32))[0:1]
        pre = a + pltpu.roll(b, nc - 1, 0) + pt
        out = jnp.dot(_gelu_tanh(pre).astype(BF16), w2_ref[kv], preferred_element_type=F32)
        o_ref[0] = jnp.where(rowi < nc - 1, out, 0.0).astype(BF16)


def _compress_weights(cmp_w1, cmp_w2, cmp_pos):
    half = NSA_CMP_LEN // 2
    w1 = cmp_w1.reshape(2, 2, half, HEAD_DIM, HEAD_DIM)
    z = jnp.zeros_like(w1)
    w1x = jnp.concatenate([jnp.concatenate([w1, z], -1), jnp.concatenate([z, w1], -1)], -2)
    w1x = w1x.reshape(2, 2, half * LANES, LANES).astype(BF16)
    z2 = jnp.zeros_like(cmp_w2)
    w2x = jnp.concatenate([jnp.concatenate([cmp_w2, z2], -1), jnp.concatenate([z2, cmp_w2], -1)], -2).astype(BF16)
    pos = cmp_pos.reshape(2, 2, half, 1, HEAD_DIM)
    posx = jnp.broadcast_to(pos, (2, 2, half, NSA_GROUPS, HEAD_DIM)).reshape(2, 2, 1, half * LANES)
    posx = jnp.broadcast_to(posx, (2, 2, 8, half * LANES)).astype(BF16)
    return w1x, w2x, posx


def _nsa_compress(rk, rv, w1x, w2x, posx):
    nb, nc, width = rk.shape
    full = lambda a: pl.BlockSpec(a.shape, lambda b: (0,) * a.ndim)
    return pl.pallas_call(
        _compress_kernel,
        grid=(nb,),
        in_specs=[pl.BlockSpec((1, nc, width), lambda b: (b, 0, 0)), pl.BlockSpec((1, nc, width), lambda b: (b, 0, 0)),
                  full(w1x), full(w2x), full(posx)],
        out_specs=[pl.BlockSpec((1, nc, LANES), lambda b: (b, 0, 0))] * 2,
        out_shape=[jax.ShapeDtypeStruct((nb, nc, LANES), BF16)] * 2,
        compiler_params=_cparams(("parallel",)),
        name="nsa_compress",
    )(rk, rv, w1x, w2x, posx)


def _cmp_to_sel_matrix(n_cmp_pad, n_cmp, n_sel):
    a = NSA_SEL_BLOCK // NSA_CMP_STRIDE
    b = NSA_CMP_LEN // NSA_CMP_STRIDE
    j = np.arange(n_sel)[:, None, None]
    idx = a * j + np.arange(a)[None, :, None] - np.arange(b)[None, None, :]
    jj = np.broadcast_to(j, idx.shape)
    ok = (idx >= 0) & (idx < n_cmp)
    m = np.zeros((n_cmp_pad, LANES), np.float32)
    np.add.at(m, (idx[ok], jj[ok]), 1.0)
    return m


def _nsa_kernel(q_ref, kc_ref, vc_ref, ka0_ref, ka1_ref, vs_ref, kw_ref, vw_ref, pf_ref, sh_ref, coef_ref, m2t_ref,
                t2_ref, gx_ref, o_ref, qs_ref, qa_ref, flag_ref, ow_ref, m_ref, l_ref, acc_ref,
                *, tq, tk, seq, n_sel, n_top):
    qi = pl.program_id(1)
    qs = qi * tq
    nc = kc_ref.shape[1]
    heads = [(g, j) for g in range(NSA_GROUPS) for j in range(NSA_HPG)]

    def slope_of(g, j):
        return _alibi_slope(g * NSA_HPG + j, NSA_HEADS)

    for n, (g, j) in enumerate(heads):
        qs_ref[n] = _pick_half(q_ref[0, :, j * LANES:(j + 1) * LANES], g)

    cidx = lax.broadcasted_iota(jnp.int32, (tq, nc), 1)
    dist_c = (qs + lax.broadcasted_iota(jnp.int32, (tq, nc), 0)) - (cidx * NSA_CMP_STRIDE + (NSA_CMP_LEN - 1))
    valid_c = (dist_c | ((nc - 2) - cidx)) >= 0
    dist_cf = dist_c.astype(F32)
    kc = kc_ref[0]
    vc = vc_ref[0]
    o_c = {}
    psum = [None] * NSA_GROUPS
    any_c = (qs + lax.broadcasted_iota(jnp.int32, (tq, 1), 0)) >= NSA_CMP_LEN - 1
    for n, (g, j) in enumerate(heads):
        s = jnp.where(valid_c, _qk(qs_ref[n], kc) - slope_of(g, j) * dist_cf, NEG)
        m = jnp.max(s, axis=1, keepdims=True)
        p = jnp.exp2(s - m)
        l = jnp.sum(p, axis=1, keepdims=True)
        pn = p * jnp.where(any_c, 1.0 / l, 0.0)
        o_c[(g, j)] = jnp.dot(pn.astype(BF16), vc, preferred_element_type=F32)
        psum[g] = pn if psum[g] is None else psum[g] + pn

    wr = min(tq, NSA_WIN_ROWS)
    back = -(-NSA_WINDOW // wr) * wr
    width = min(back + wr, seq)
    for rh in range(tq // wr):
        rows = slice(rh * wr, (rh + 1) * wr)
        qs_h = qs + rh * wr
        start = _window_start(qs_h, back, width, seq, wr)
        dist = ((qs_h - start) + lax.broadcasted_iota(jnp.int32, (wr, width), 0)
                - lax.broadcasted_iota(jnp.int32, (wr, width), 1))
        valid_w = (dist | ((NSA_WINDOW - 1) - dist)) >= 0
        distf = dist.astype(F32)
        kw = kw_ref[0, pl.ds(start, width), :]
        vw = vw_ref[0, pl.ds(start, width), :]
        for n, (g, j) in enumerate(heads):
            s = jnp.where(valid_w, _qk(qs_ref[n, rows, :], kw) - slope_of(g, j) * distf, NEG)
            ow_ref[n, rows, :], _ = _softmax_pv(s, vw)

    rsel = -(-n_sel // 8) * 8
    blk = lax.broadcasted_iota(jnp.int32, (rsel, tq), 0)
    blkf = blk.astype(F32)
    tpos = qs + lax.broadcasted_iota(jnp.int32, (rsel, tq), 1)
    cur = tpos >> 6
    forced = (blk == 0) | (blk == cur) | (blk == cur - 1)
    causal_b = blk * NSA_SEL_BLOCK <= tpos
    sel_b = []
    for g in range(NSA_GROUPS):
        ph = psum[g].astype(BF16)
        plo = (psum[g] - ph.astype(F32)).astype(BF16)
        imp = (_qk(m2t_ref[...], ph) + _qk(m2t_ref[...], plo))[0:rsel]
        score = jnp.where(causal_b, jnp.where(forced, NSA_FORCE_SCORE, imp), NEG)
        score = jnp.where(blk < n_sel, score, PICKED)
        sel = jnp.zeros((rsel, tq), F32)
        for _ in range(n_top):
            mx = jnp.max(score, axis=0, keepdims=True)
            idx = jnp.min(jnp.where(score == mx, blkf, float(LANES)), axis=0, keepdims=True)
            hit = blkf == idx
            sel = jnp.where(hit, 1.0, sel)
            score = jnp.where(hit, PICKED, score)
        sel_b.append(sel.astype(BF16))

    ones = jnp.ones((tq, LANES), BF16)
    tposf = (qs + lax.broadcasted_iota(jnp.int32, (tq, LANES), 0)).astype(F32)
    for g in range(NSA_GROUPS):
        per_block = jnp.dot(sel_b[g], ones, preferred_element_type=F32)
        flag_ref[g] = jnp.dot(t2_ref[:, 0:rsel], per_block.astype(BF16), preferred_element_type=F32)
        sel_q = lax.dot_general(sel_b[g], sh_ref[g], (((0,), (0,)), ((), ())),
                                preferred_element_type=F32)
        mask_part = ((sel_q - 1.0) * MASK_BIG).astype(BF16)
        for j in range(NSA_HPG):
            n = g * NSA_HPG + j
            first = jnp.where(_half_mask((tq, LANES), g), q_ref[0, :, j * LANES:(j + 1) * LANES], mask_part)
            qa_ref[n] = jnp.concatenate([first, _alibi_query_lanes(tposf, slope_of(g, j), coef_ref[n])], axis=1)
    _flash_reset(m_ref, l_ref, acc_ref)

    def slc_tile(ki, last):
        ks = pl.multiple_of(ki * tk, tk)
        for g, ka_ref in enumerate((ka0_ref, ka1_ref)):
            def group_update(g=g, ka_ref=ka_ref):
                ka = ka_ref[0, pl.ds(ks, tk), :]
                v = vs_ref[0, pl.ds(ks, tk), :]
                for j in range(NSA_HPG):
                    n = g * NSA_HPG + j
                    s = _qk(qa_ref[n], ka)
                    if last:
                        s = jnp.where(lax.broadcasted_iota(jnp.int32, (tq, tk), 0)
                                      >= lax.broadcasted_iota(jnp.int32, (tq, tk), 1), s, NEG)
                    _flash_update(n, s, None, v, m_ref, l_ref, acc_ref)
            if last:
                group_update()
            else:
                pl.when(flag_ref[g, pl.ds(ki, 1), :][0, 0] > 0.5)(group_update)

    def slc_body(ki, c):
        slc_tile(ki, False)
        return c

    lax.fori_loop(0, qi, slc_body, 0)
    slc_tile(qi, True)

    gl = jax.nn.sigmoid(pf_ref[0])
    g_hi = gl.astype(BF16)
    g_lo = (gl - g_hi.astype(F32)).astype(BF16)

    def gate(j, r):
        x = gx_ref[j * 3 + r]
        return jnp.dot(g_hi, x, preferred_element_type=F32) + jnp.dot(g_lo, x, preferred_element_type=F32)

    for j in range(NSA_HPG):
        n0, n1 = j, NSA_HPG + j
        comb = (gate(j, 0) * _merge_halves(o_c[(0, j)], o_c[(1, j)])
                + gate(j, 1) * _merge_halves(acc_ref[n0] / l_ref[n0], acc_ref[n1] / l_ref[n1])
                + gate(j, 2) * _merge_halves(ow_ref[n0], ow_ref[n1]))
        o_ref[0, :, j * LANES:(j + 1) * LANES] = comb.astype(BF16)


MASK_BIG = 2.0 ** 100


def _nsa_key_features(k_slc, seq):
    nb = k_slc.shape[0]
    pos = np.arange(seq)
    onehot = (pos[:, None] // NSA_SEL_BLOCK == np.arange(HALF)[None, :]).astype(np.float32)
    feats = _alibi_key_lanes(seq)
    oh = jnp.broadcast_to(jnp.asarray(onehot, BF16), (nb, seq, HALF))
    ft = jnp.broadcast_to(jnp.asarray(feats, BF16), (nb, seq, LANES))
    return (jnp.concatenate([k_slc[:, :, :HALF], oh, ft], axis=-1),
            jnp.concatenate([oh, k_slc[:, :, HALF:], ft], axis=-1))


def _nsa_attention(pb, pf, kc, vc, tq=512):
    nb, seq, _ = pb.shape
    tq = min(tq, seq)
    tk = tq
    nc = kc.shape[1]
    n_sel = seq // NSA_SEL_BLOCK
    assert n_sel <= HALF and seq <= 64 * 64
    n_top = min(NSA_TOP_N, n_sel)
    n_cmp = seq // NSA_CMP_STRIDE - NSA_CMP_LEN // NSA_CMP_STRIDE + 1
    m2t = jnp.asarray(_cmp_to_sel_matrix(nc, n_cmp, n_sel).T, BF16)
    n_tiles = seq // tk
    tile_rows = -(-n_tiles // 8) * 8
    blocks_per_tile = tk // NSA_SEL_BLOCK
    t2 = jnp.asarray(np.arange(tile_rows)[:, None] == (np.arange(LANES)[None, :] // blocks_per_tile), BF16)
    gx = np.zeros((NSA_HPG * 3, LANES, LANES), np.float32)
    for j in range(NSA_HPG):
        for r in range(3):
            for g in range(NSA_GROUPS):
                gx[j * 3 + r, GATE_COL + (g * NSA_HPG + j) * 3 + r, g * HALF:(g + 1) * HALF] = 1.0
    gx = jnp.asarray(gx, BF16)
    rsel = -(-n_sel // 8) * 8
    shift = np.zeros((NSA_GROUPS, rsel, LANES), np.float32)
    for g in range(NSA_GROUPS):
        shift[g, np.arange(n_sel), (1 - g) * HALF + np.arange(n_sel)] = 1.0
    coef = _alibi_coef_lanes([_alibi_slope(n, NSA_HEADS) for n in range(NSA_HEADS)])
    ka0, ka1 = _nsa_key_features(pb[:, :, P_NSAKV + 2 * LANES:P_NSAKV + 3 * LANES], seq)
    w = NSA_HPG * LANES
    kvb = P_NSAKV // LANES
    kv_spec = lambda i: pl.BlockSpec((1, seq, LANES), lambda b, q, i=i: (b, 0, kvb + i))
    ka_spec = pl.BlockSpec((1, seq, 2 * LANES), lambda b, q: (b, 0, 0))
    return pl.pallas_call(
        functools.partial(_nsa_kernel, tq=tq, tk=tk, seq=seq, n_sel=n_sel, n_top=n_top),
        grid=(nb, seq // tq),
        in_specs=[
            pl.BlockSpec((1, tq, w), lambda b, i: (b, i, P_NSAQ // w)),
            pl.BlockSpec((1, nc, LANES), lambda b, i: (b, 0, 0)),
            pl.BlockSpec((1, nc, LANES), lambda b, i: (b, 0, 0)),
            ka_spec, ka_spec, kv_spec(3), kv_spec(4), kv_spec(5),
            pl.BlockSpec((1, tq, LANES), lambda b, i: (b, i, 0)),
            pl.BlockSpec((NSA_GROUPS, rsel, LANES), lambda b, i: (0, 0, 0)),
            pl.BlockSpec((NSA_HEADS, 1, LANES), lambda b, i: (0, 0, 0)),
            pl.BlockSpec((LANES, nc), lambda b, i: (0, 0)),
            pl.BlockSpec((tile_rows, LANES), lambda b, i: (0, 0)),
            pl.BlockSpec((NSA_HPG * 3, LANES, LANES), lambda b, i: (0, 0, 0)),
        ],
        out_specs=pl.BlockSpec((1, tq, w), lambda b, i: (b, i, 0)),
        out_shape=jax.ShapeDtypeStruct((nb, seq, w), BF16),
        scratch_shapes=([pltpu.VMEM((NSA_HEADS, tq, LANES), BF16), pltpu.VMEM((NSA_HEADS, tq, 2 * LANES), BF16),
                         pltpu.VMEM((NSA_GROUPS, tile_rows, LANES), F32),
                         pltpu.VMEM((NSA_HEADS, tq, LANES), F32)] + _flash_scratch(NSA_HEADS, tq)),
        compiler_params=_cparams(("parallel", "arbitrary")),
        name="nsa_attention",
    )(pb, kc, vc, ka0, ka1, pb, pb, pb, pf, jnp.asarray(shift, BF16), jnp.asarray(coef), m2t, t2, gx)


def _mixers(pb, pf, f_bias, cmp_w1, cmp_w2, cmp_pos, lam_vecs, subln_g, lam_init):
    nb, seq, _ = pb.shape
    cum = _forget_cumsum(pf, f_bias)
    cumrow = jnp.transpose(cum[:, :, :8], (0, 2, 1))
    o_fox = _fox_attention(pb, cum, cumrow)
    nc = seq // NSA_CMP_STRIDE
    rk = pb[:, :, P_NSAKV:P_NSAKV + LANES].reshape(nb, nc, NSA_CMP_STRIDE * LANES)
    rv = pb[:, :, P_NSAKV + LANES:P_NSAKV + 2 * LANES].reshape(nb, nc, NSA_CMP_STRIDE * LANES)
    kc, vc = _nsa_compress(rk, rv, *_compress_weights(cmp_w1, cmp_w2, cmp_pos))
    o_nsa = _nsa_attention(pb, pf, kc, vc)
    o_dil = _dil_attention(pb)
    o_dif = _diff_attention(pb, lam_vecs, subln_g, lam_init)
    return [o_dil, o_fox, o_nsa, o_dif]


def kernel(x, c, ada_w, ada_b, norm_mix_g, norm_ffn_g, w_in, fox_f_bias, nsa_cmp_w1, nsa_cmp_w2, nsa_cmp_pos,
           diff_lambda, diff_subln_g, w_out, ffn_w_gate, ffn_w_up, ffn_w_down, final_norm_g):
    nb, seq, d = x.shape
    depth = ada_w.shape[0]
    m = nb * seq
    mods = _modulation(c, ada_w, ada_b)
    w_p = _relayout_w_in(w_in, _proj_colmap())
    w_pf = w_p[:, :, P_FGATE:P_FGATE + LANES] + w_p[:, :, P_NGATE:P_NGATE + LANES]
    w_o = _gather_axis(w_out, _out_rowmap(), 1).astype(BF16)
    w_g, w_u, w_d = ffn_w_gate.astype(BF16), ffn_w_up.astype(BF16), ffn_w_down.astype(BF16)
    h = x.reshape(m, d)
    for layer in range(depth):
        sh1, sc1, g1, sh2, sc2, g2 = [mods[layer, :, i * d:(i + 1) * d].reshape(nb, 1, d) for i in range(6)]
        lam_init = 0.8 - 0.6 * math.exp(-0.3 * layer)
        pb, pf = _norm_proj(h, norm_mix_g[layer], sc1, sh1, w_p, layer, w_pf[layer], seq)
        parts = _mixers(pb.reshape(nb, seq, N_PB), pf.reshape(nb, seq, LANES), fox_f_bias[layer],
                        nsa_cmp_w1[layer], nsa_cmp_w2[layer], nsa_cmp_pos[layer],
                        diff_lambda[layer], diff_subln_g[layer], lam_init)
        h = _matmul_gated_residual([p.reshape(m, p.shape[-1]) for p in parts], w_o, layer, h, g1, seq,
                                   name="out_proj_res")
        act = _norm_ffn_up(h, norm_ffn_g[layer], sc2, sh2, w_g, w_u, layer, seq)
        h = _matmul_gated_residual([act], w_d, layer, h, g2, seq, name="ffn_down_res")
    return _final_norm(h, final_norm_g).reshape(nb, seq, d)
```
